```python
import math
import jax
import jax.numpy as jnp
from jax import lax
import numpy as np

D_MODEL = 2048
BATCH = 8
SEQ = 2048
DEPTH = 4

GRID_W = 64
CTX_LEN = 256
N_MIXERS = 3
EPS = 1e-6
ROPE_THETA = 10000.0
Q_BLOCK = 128

GQA_HEADS = 16
GQA_KV_HEADS = 4
GQA_HEAD_DIM = D_MODEL // GQA_HEADS
GQA_Q_W = GQA_HEADS * GQA_HEAD_DIM
GQA_KV_W = GQA_KV_HEADS * GQA_HEAD_DIM

MLA_HEADS = 16
MLA_Q_RANK = 512
MLA_KV_RANK = 512
MLA_NOPE_DIM = 128
MLA_ROPE_DIM = 64
MLA_V_DIM = 128
MLA_QK_DIM = MLA_NOPE_DIM + MLA_ROPE_DIM

NA_HEADS = 16
NA_HEAD_DIM = D_MODEL // NA_HEADS
NA_WIN_ROWS = 8
NA_WIN_COLS = 16

N_EXPERTS = 16
N_GROUPS = 4
EXPERTS_PER_GROUP = N_EXPERTS // N_GROUPS
TOPK_GROUPS = 1
TOP_K = 2
D_EXPERT = 512

kernel_name = "hybrid_interleaved_gqa_mla_natten_moe_dit"


def rmsnorm(x, gain):
    xf = x.astype(jnp.float32)
    y = xf * lax.rsqrt(jnp.mean(xf * xf, axis=-1, keepdims=True) + EPS)
    return (y * gain.astype(jnp.float32)).astype(x.dtype)


def axial_rope(n_tok, rot_dim):
    n_freq = rot_dim // 4
    inv_freq = 1.0 / (ROPE_THETA ** (jnp.arange(n_freq, dtype=jnp.float32) / n_freq))
    t = jnp.arange(n_tok, dtype=jnp.int32)
    row = (t // GRID_W).astype(jnp.float32)
    col = (t % GRID_W).astype(jnp.float32)
    ang = jnp.concatenate([row[:, None] * inv_freq, col[:, None] * inv_freq], axis=-1)
    return jnp.cos(ang), jnp.sin(ang)


def apply_rope(x, cos, sin):
    half = x.shape[-1] // 2
    xf = x.astype(jnp.float32)
    x1, x2 = xf[..., :half], xf[..., half:]
    cs, sn = cos[None, :, None, :], sin[None, :, None, :]
    return jnp.concatenate([x1 * cs - x2 * sn, x1 * sn + x2 * cs], axis=-1).astype(x.dtype)


def modulation(cond, w_mod, b_mod):
    m = jax.nn.silu(cond) @ w_mod + b_mod
    return jnp.split(m, 6, axis=-1)


def _attend(q, k, v, scale):
    s = jnp.einsum('bqhgd,bkhd->bhgqk', q, k).astype(jnp.float32) * scale
    p = jax.nn.softmax(s, axis=-1).astype(v.dtype)
    return jnp.einsum('bhgqk,bkhd->bqhgd', p, v)


def joint_latent_attention(q, k_lat, v_lat, k_ctx, v_ctx, scale):
    b, s = q.shape[:2]
    nb = s // Q_BLOCK
    k_all = jnp.concatenate([k_ctx, k_lat], axis=1)
    v_all = jnp.concatenate([v_ctx, v_lat], axis=1)
    qb = jnp.moveaxis(q.reshape(b, nb, Q_BLOCK, *q.shape[2:]), 1, 0)
    o = lax.map(lambda qq: _attend(qq, k_all, v_all, scale), qb)
    return jnp.moveaxis(o, 0, 1).reshape(b, s, *o.shape[3:])


def gqa_mixer(uc, ux, w_qkv, q_gain, k_gain, w_o, cos, sin, ctx_out):
    g = GQA_HEADS // GQA_KV_HEADS

    def proj(u):
        bsz, t = u.shape[:2]
        q, k, v = jnp.split(u @ w_qkv, [GQA_Q_W, GQA_Q_W + GQA_KV_W], axis=-1)
        q = rmsnorm(q.reshape(bsz, t, GQA_HEADS, GQA_HEAD_DIM), q_gain)
        k = rmsnorm(k.reshape(bsz, t, GQA_KV_HEADS, GQA_HEAD_DIM), k_gain)
        return q, k, v.reshape(bsz, t, GQA_KV_HEADS, GQA_HEAD_DIM)

    qc, kc, vc = proj(uc)
    qx, kx, vx = proj(ux)
    qx = apply_rope(qx, cos, sin)
    kx = apply_rope(kx, cos, sin)
    b, s = ux.shape[:2]
    l = uc.shape[1]
    scale = GQA_HEAD_DIM ** -0.5
    ox = joint_latent_attention(qx.reshape(b, s, GQA_KV_HEADS, g, GQA_HEAD_DIM), kx, vx, kc, vc, scale)
    yx = ox.reshape(b, s, GQA_Q_W) @ w_o
    yc = None
    if ctx_out:
        oc = _attend(qc.reshape(b, l, GQA_KV_HEADS, g, GQA_HEAD_DIM), kc, vc, scale)
        yc = oc.reshape(b, l, GQA_Q_W) @ w_o
    return yc, yx


def mla_mixer(uc, ux, w_a, q_gain, kv_gain, w_uq, w_ukv, w_o, cos, sin, ctx_out):
    def proj(u, rotate):
        bsz, t = u.shape[:2]
        cq, ckv, k_pe = jnp.split(u @ w_a, [MLA_Q_RANK, MLA_Q_RANK + MLA_KV_RANK], axis=-1)
        q = (rmsnorm(cq, q_gain) @ w_uq).reshape(bsz, t, MLA_HEADS, MLA_QK_DIM)
        kv = (rmsnorm(ckv, kv_gain) @ w_ukv).reshape(bsz, t, MLA_HEADS, MLA_NOPE_DIM + MLA_V_DIM)
        q_nope, q_pe = jnp.split(q, [MLA_NOPE_DIM], axis=-1)
        k_nope, v = jnp.split(kv, [MLA_NOPE_DIM], axis=-1)
        k_pe = k_pe[:, :, None, :]
        if rotate:
            q_pe = apply_rope(q_pe, cos, sin)
            k_pe = apply_rope(k_pe, cos, sin)
        q = jnp.concatenate([q_nope, q_pe], axis=-1)[:, :, :, None, :]
        k = jnp.concatenate([k_nope, jnp.broadcast_to(k_pe, (bsz, t, MLA_HEADS, MLA_ROPE_DIM))], axis=-1)
        return q, k, v

    qc, kc, vc = proj(uc, False)
    qx, kx, vx = proj(ux, True)
    b, s = ux.shape[:2]
    l = uc.shape[1]
    scale = MLA_QK_DIM ** -0.5
    ox = joint_latent_attention(qx, kx, vx, kc, vc, scale)
    yx = ox.reshape(b, s, MLA_HEADS * MLA_V_DIM) @ w_o
    yc = None
    if ctx_out:
        yc = _attend(qc, kc, vc, scale).reshape(b, l, MLA_HEADS * MLA_V_DIM) @ w_o
    return yc, yx


def na_mixer(uc, ux, w_qkv, rpb, w_o, ctx_out):
    b, s = ux.shape[:2]
    l = uc.shape[1]
    rows = s // GRID_W
    kr = min(NA_WIN_ROWS, rows)
    scale = NA_HEAD_DIM ** -0.5

    def proj(u):
        bsz, t = u.shape[:2]
        q, k, v = jnp.split(u @ w_qkv, 3, axis=-1)
        shp = (bsz, t, NA_HEADS, NA_HEAD_DIM)
        return q.reshape(shp), k.reshape(shp), v.reshape(shp)

    qc, kc, vc = proj(uc)
    qx, kx, vx = proj(ux)
    qg = jnp.moveaxis(qx.reshape(b, rows, GRID_W, NA_HEADS, NA_HEAD_DIM), 1, 0)
    kg = kx.reshape(b, rows, GRID_W, NA_HEADS, NA_HEAD_DIM)
    vg = vx.reshape(b, rows, GRID_W, NA_HEADS, NA_HEAD_DIM)
    col = jnp.arange(GRID_W)
    col_start = jnp.clip(col - NA_WIN_COLS // 2, 0, GRID_W - NA_WIN_COLS)
    col_mask = (col[None, :] >= col_start[:, None]) & (col[None, :] < col_start[:, None] + NA_WIN_COLS)
    col_bias_idx = jnp.clip(col[None, :] - col[:, None] + NA_WIN_COLS - 1, 0, 2 * NA_WIN_COLS - 2)

    def row_block(args):
        r, qrow = args
        r0 = jnp.clip(r - kr // 2, 0, rows - kr)
        krows = lax.dynamic_slice_in_dim(kg, r0, kr, axis=1)
        vrows = lax.dynamic_slice_in_dim(vg, r0, kr, axis=1)
        dr = r0 + jnp.arange(kr) - r
        bias = rpb[:, dr + NA_WIN_ROWS - 1][:, :, col_bias_idx]
        bias = jnp.transpose(bias, (0, 2, 1, 3)).astype(jnp.float32)
        s_loc = jnp.einsum('bqhd,brkhd->bhqrk', qrow, krows).astype(jnp.float32) * scale + bias[None]
        s_loc = jnp.where(col_mask[None, None, :, None, :], s_loc, -jnp.inf)
        s_ctx = jnp.einsum('bqhd,bchd->bhqc', qrow, kc).astype(jnp.float32) * scale
        sc = jnp.concatenate([s_ctx, s_loc.reshape(b, NA_HEADS, GRID_W, kr * GRID_W)], axis=-1)
        p = jax.nn.softmax(sc, axis=-1).astype(vrows.dtype)
        p_loc = p[..., l:].reshape(b, NA_HEADS, GRID_W, kr, GRID_W)
        return (jnp.einsum('bhqc,bchd->bqhd', p[..., :l], vc)
                + jnp.einsum('bhqrk,brkhd->bqhd', p_loc, vrows))

    o = lax.map(row_block, (jnp.arange(rows), qg))
    yx = jnp.moveaxis(o, 0, 1).reshape(b, s, NA_HEADS * NA_HEAD_DIM) @ w_o
    yc = None
    if ctx_out:
        oc = _attend(qc[:, :, :, None, :], kc, vc, scale)
        yc = oc.reshape(b, l, NA_HEADS * NA_HEAD_DIM) @ w_o
    return yc, yx


def grouped_moe(h, router_w, router_bias, w1, w3, w2):
    n = h.shape[0]
    scores = jax.nn.sigmoid((h @ router_w).astype(jnp.float32))
    sel = scores + router_bias.astype(jnp.float32)
    grp_score = lax.top_k(sel.reshape(n, N_GROUPS, EXPERTS_PER_GROUP), TOP_K)[0].sum(-1)
    _, gidx = lax.top_k(grp_score, TOPK_GROUPS)
    gmask = jax.nn.one_hot(gidx, N_GROUPS, dtype=jnp.float32).sum(-2) > 0
    emask = jnp.repeat(gmask, EXPERTS_PER_GROUP, axis=-1)
    _, eidx = lax.top_k(jnp.where(emask, sel, -jnp.inf), TOP_K)
    w = jnp.take_along_axis(scores, eidx, axis=-1)
    w = w / jnp.sum(w, axis=-1, keepdims=True)
    gates = jnp.einsum('nk,nke->ne', w, jax.nn.one_hot(eidx, N_EXPERTS, dtype=jnp.float32)).astype(h.dtype)
    y = jnp.zeros_like(h)
    for e in range(N_EXPERTS):
        he = jax.nn.silu(h @ w1[e]) * (h @ w3[e])
        y = y + gates[:, e:e + 1] * (he @ w2[e])
    return y


def setup_inputs(seed: int = 0) -> dict:
    key = jax.random.key(seed)
    ks = iter(jax.random.split(key, 32))
    n_a = len(range(0, DEPTH, N_MIXERS))
    n_b = len(range(1, DEPTH, N_MIXERS))
    n_c = len(range(2, DEPTH, N_MIXERS))
    f32 = jnp.float32

    def nrm(shape, scale):
        return jax.random.normal(next(ks), shape, f32) * scale

    def gain(shape):
        return 1.0 + nrm(shape, 0.02)

    d = D_MODEL
    return {
        "x": nrm((BATCH, SEQ, d), 1.0),
        "c": nrm((BATCH, d), 1.0),
        "ctx": nrm((BATCH, CTX_LEN, d), 1.0),
        "c_ctx": nrm((d,), 1.0),
        "mod_w": nrm((DEPTH, d, 6 * d), 0.5 * d ** -0.5),
        "mod_b": nrm((DEPTH, 6 * d), 0.05),
        "norm_mix": gain((DEPTH, d)),
        "norm_ffn": gain((DEPTH, d)),
        "gqa_wqkv": nrm((n_a, d, GQA_Q_W + 2 * GQA_KV_W), d ** -0.5),
        "gqa_q_gain": gain((n_a, GQA_HEAD_DIM)),
        "gqa_k_gain": gain((n_a, GQA_HEAD_DIM)),
        "gqa_wo": nrm((n_a, GQA_Q_W, d), GQA_Q_W ** -0.5),
        "mla_wa": nrm((n_b, d, MLA_Q_RANK + MLA_KV_RANK + MLA_ROPE_DIM), d ** -0.5),
        "mla_q_gain": gain((n_b, MLA_Q_RANK)),
        "mla_kv_gain": gain((n_b, MLA_KV_RANK)),
        "mla_wuq": nrm((n_b, MLA_Q_RANK, MLA_HEADS * MLA_QK_DIM), MLA_Q_RANK ** -0.5),
        "mla_wukv": nrm((n_b, MLA_KV_RANK, MLA_HEADS * (MLA_NOPE_DIM + MLA_V_DIM)), MLA_KV_RANK ** -0.5),
        "mla_wo": nrm((n_b, MLA_HEADS * MLA_V_DIM, d), (MLA_HEADS * MLA_V_DIM) ** -0.5),
        "na_wqkv": nrm((n_c, d, 3 * NA_HEADS * NA_HEAD_DIM), d ** -0.5),
        "na_rpb": nrm((n_c, NA_HEADS, 2 * NA_WIN_ROWS - 1, 2 * NA_WIN_COLS - 1), 0.1),
        "na_wo": nrm((n_c, NA_HEADS * NA_HEAD_DIM, d), (NA_HEADS * NA_HEAD_DIM) ** -0.5),
        "router_w": nrm((d, N_EXPERTS), d ** -0.5),
        "router_bias": nrm((N_EXPERTS,), 0.01),
        "moe_w1": nrm((DEPTH, N_EXPERTS, d, D_EXPERT), d ** -0.5),
        "moe_w3": nrm((DEPTH, N_EXPERTS, d, D_EXPERT), d ** -0.5),
        "moe_w2": nrm((DEPTH, N_EXPERTS, D_EXPERT, d), D_EXPERT ** -0.5),
        "final_gain": gain((d,)),
    }


def reference(x, c, ctx, c_ctx, mod_w, mod_b, norm_mix, norm_ffn,
              gqa_wqkv, gqa_q_gain, gqa_k_gain, gqa_wo,
              mla_wa, mla_q_gain, mla_kv_gain, mla_wuq, mla_wukv, mla_wo,
              na_wqkv, na_rpb, na_wo,
              router_w, router_bias, moe_w1, moe_w3, moe_w2, final_gain):
    b, s, d = x.shape
    l = ctx.shape[1]
    cos_a, sin_a = axial_rope(s, GQA_HEAD_DIM)
    cos_b, sin_b = axial_rope(s, MLA_ROPE_DIM)
    hx, hc = x, ctx
    cond_x = c[:, None, :]
    cond_c = c_ctx[None, None, :]
    for i in range(DEPTH):
        ctx_out = i < DEPTH - 1
        sh1x, sc1x, g1x, sh2x, sc2x, g2x = modulation(cond_x, mod_w[i], mod_b[i])
        sh1c, sc1c, g1c, sh2c, sc2c, g2c = modulation(cond_c, mod_w[i], mod_b[i])
        ux = rmsnorm(hx, norm_mix[i]) * (1.0 + sc1x) + sh1x
        uc = rmsnorm(hc, norm_mix[i]) * (1.0 + sc1c) + sh1c
        kind, j = i % N_MIXERS, i // N_MIXERS
        if kind == 0:
            yc, yx = gqa_mixer(uc, ux, gqa_wqkv[j], gqa_q_gain[j], gqa_k_gain[j], gqa_wo[j],
                               cos_a, sin_a, ctx_out)
        elif kind == 1:
            yc, yx = mla_mixer(uc, ux, mla_wa[j], mla_q_gain[j], mla_kv_gain[j], mla_wuq[j],
                               mla_wukv[j], mla_wo[j], cos_b, sin_b, ctx_out)
        else:
            yc, yx = na_mixer(uc, ux, na_wqkv[j], na_rpb[j], na_wo[j], ctx_out)
        hx = hx + g1x * yx
        vx = rmsnorm(hx, norm_ffn[i]) * (1.0 + sc2x) + sh2x
        if ctx_out:
            hc = hc + g1c * yc
            vc = rmsnorm(hc, norm_ffn[i]) * (1.0 + sc2c) + sh2c
            tok = jnp.concatenate([vc.reshape(b * l, d), vx.reshape(b * s, d)], axis=0)
            y = grouped_moe(tok, router_w, router_bias, moe_w1[i], moe_w3[i], moe_w2[i])
            hc = hc + g2c * y[:b * l].reshape(b, l, d)
            hx = hx + g2x * y[b * l:].reshape(b, s, d)
        else:
            y = grouped_moe(vx.reshape(b * s, d), router_w, router_bias, moe_w1[i], moe_w3[i], moe_w2[i])
            hx = hx + g2x * y.reshape(b, s, d)
    return rmsnorm(hx, final_gain)
```

```python
import functools

import numpy as np
import jax
import jax.numpy as jnp
from jax import lax
from jax.experimental import pallas as pl
from jax.experimental.pallas import tpu as pltpu

F32 = jnp.float32
BF16 = jnp.bfloat16

D_MODEL = 2048
SEQ = 2048
GRID_W = 64
GRID_ROWS = SEQ // GRID_W
CTX_LEN = 256
TOK = CTX_LEN + SEQ
DEPTH = 4
N_MIXERS = 3
EPS = 1e-6
ROPE_THETA = 10000.0

GQA_HEADS = 16
GQA_KV_HEADS = 4
HEAD_DIM = 128
GQA_GROUP = GQA_HEADS // GQA_KV_HEADS

MLA_HEADS = 16
MLA_RANK = 512
MLA_NOPE = 128
MLA_ROPE = 64
MLA_QK = MLA_NOPE + MLA_ROPE

NA_HEADS = 16
NA_WIN_ROWS = 8
NA_WIN_COLS = 16
NA_Q_ROWS = 4
NA_K_ROWS = 12
NA_KEYS = NA_K_ROWS * GRID_W

N_EXPERTS = 16
N_GROUPS = 4
GROUP_SIZE = N_EXPERTS // N_GROUPS
N_PAIRS = 6
N_PARTS = N_GROUPS * N_PAIRS
D_EXPERT = 512
PAIR_LO = (0, 0, 0, 1, 1, 2)
PAIR_HI = (1, 2, 3, 2, 3, 3)

LANES = 128
TM = 256
TILES_PER_BATCH = TOK // TM
MOD_ROWS = 16
CTX_MOD_ROW = 8
NEG_BIG = -1e30
VMEM_LIMIT = 56 * 1024 * 1024


def _params():
    return pltpu.CompilerParams(vmem_limit_bytes=VMEM_LIMIT)


def _mod_row(i):
    return jnp.where(i % TILES_PER_BATCH == 0, CTX_MOD_ROW, i // TILES_PER_BATCH)


def _silu(x):
    return x / (1.0 + jnp.exp(-x))


def _mod_kernel(cond_ref, w_ref, b_ref, o_ref):
    s = _silu(cond_ref[...]).astype(BF16)
    o_ref[...] = jnp.dot(s, w_ref[...].astype(BF16), preferred_element_type=F32) + b_ref[...]


def _modulation(cond, mod_w, mod_b):
    tn = 1024
    n_out = mod_w.shape[-1]
    return pl.pallas_call(
        _mod_kernel,
        grid=(DEPTH, n_out // tn),
        in_specs=[
            pl.BlockSpec((MOD_ROWS, D_MODEL), lambda l, j: (0, 0)),
            pl.BlockSpec((None, D_MODEL, tn), lambda l, j: (l, 0, j)),
            pl.BlockSpec((None, 1, tn), lambda l, j: (l, 0, j)),
        ],
        out_specs=pl.BlockSpec((None, MOD_ROWS, tn), lambda l, j: (l, 0, j)),
        out_shape=jax.ShapeDtypeStruct((DEPTH, MOD_ROWS, n_out), F32),
        compiler_params=_params(),
        name="modulation",
    )(cond, mod_w, mod_b.reshape(DEPTH, 1, n_out))


def _rms(x, gain):
    return x * lax.rsqrt(jnp.mean(x * x, axis=-1, keepdims=True) + EPS) * gain


def _rope(x, c, s):
    return x * c + pltpu.roll(x, LANES // 2, 1) * s


def _runs(kinds):
    out, start = [], 0
    for k in range(1, len(kinds) + 1):
        if k == len(kinds) or kinds[k] != kinds[start] or kinds[start][0] in ("rope", "normrope"):
            out.append((start, k, kinds[start]))
            start = k
    return out


def _proj_kernel(*refs, prologue, kinds, chunk, has_tab, has_hgain, resid):
    it = iter(refs)
    x_ref = next(it)
    gain_ref = sc_ref = sh_ref = None
    if prologue == "normmod":
        gain_ref, sc_ref, sh_ref = next(it), next(it), next(it)
    elif prologue == "rms":
        gain_ref = next(it)
    w_ref = next(it)
    ctab_ref = stab_ref = hgain_ref = res_ref = gate_ref = None
    if has_tab:
        ctab_ref, stab_ref = next(it), next(it)
    if has_hgain:
        hgain_ref = next(it)
    if resid:
        res_ref, gate_ref = next(it), next(it)
    o_ref = next(it)

    if prologue == "none":
        u = x_ref[...]
    else:
        y = _rms(x_ref[...], gain_ref[...])
        if prologue == "normmod":
            y = y * (1.0 + sc_ref[...]) + sh_ref[...]
        u = y.astype(BF16)

    n_out = o_ref.shape[-1]
    for c0 in range(0, n_out, chunk):
        acc = jnp.dot(u, w_ref[:, c0:c0 + chunk], preferred_element_type=F32)
        b0 = c0 // LANES
        for lo, hi, kind in _runs(kinds[b0:b0 + chunk // LANES]):
            a = acc[:, lo * LANES:hi * LANES]
            cols = slice(c0 + lo * LANES, c0 + hi * LANES)
            if kind[0] == "scale":
                a = a * kind[1]
            elif kind[0] == "rope":
                t = kind[1]
                a = _rope(a, ctab_ref[:, t * LANES:(t + 1) * LANES], stab_ref[:, t * LANES:(t + 1) * LANES])
            elif kind[0] == "normrope":
                t, g = kind[1], kind[2]
                a = _rms(a, hgain_ref[g:g + 1, :])
                a = _rope(a, ctab_ref[:, t * LANES:(t + 1) * LANES], stab_ref[:, t * LANES:(t + 1) * LANES])
            elif kind[0] == "resid":
                a = res_ref[:, cols] + gate_ref[:, cols] * a
            o_ref[:, cols] = a.astype(o_ref.dtype)


def _proj(x, w, *, out_dtype, kinds, prologue="none", x_col_block=0, gain=None, mod=None,
          mod_sc=None, mod_sh=None, tabs=None, hgain=None, res=None, mod_gate=None, chunk=512):
    n = x.shape[0]
    k, n_out = w.shape
    assert n % TM == 0 and n_out % chunk == 0 and len(kinds) == n_out // LANES
    args = [x]
    specs = [pl.BlockSpec((TM, k), lambda i: (i, x_col_block))]
    if prologue in ("normmod", "rms"):
        args.append(gain.reshape(1, k))
        specs.append(pl.BlockSpec((1, k), lambda i: (0, 0)))
    if prologue == "normmod":
        for comp in (mod_sc, mod_sh):
            args.append(mod)
            specs.append(pl.BlockSpec((None, 1, k), lambda i, comp=comp: (comp * MOD_ROWS + _mod_row(i), 0, 0)))
    args.append(w)
    specs.append(pl.BlockSpec((k, n_out), lambda i: (0, 0), pipeline_mode=pl.Buffered(1)))
    if tabs is not None:
        for t in tabs:
            args.append(t)
            specs.append(pl.BlockSpec((TM, t.shape[1]), lambda i: (i % TILES_PER_BATCH, 0)))
    if hgain is not None:
        args.append(hgain)
        specs.append(pl.BlockSpec(hgain.shape, lambda i: (0, 0)))
    if res is not None:
        args.append(res)
        specs.append(pl.BlockSpec((TM, n_out), lambda i: (i, 0)))
        args.append(mod)
        specs.append(pl.BlockSpec((None, 1, n_out), lambda i: (mod_gate * MOD_ROWS + _mod_row(i), 0, 0)))
    kern = functools.partial(_proj_kernel, prologue=prologue, kinds=tuple(kinds), chunk=chunk,
                             has_tab=tabs is not None, has_hgain=hgain is not None, resid=res is not None)
    return pl.pallas_call(
        kern,
        grid=(n // TM,),
        in_specs=specs,
        out_specs=pl.BlockSpec((TM, n_out), lambda i: (i, 0)),
        out_shape=jax.ShapeDtypeStruct((n, n_out), out_dtype),
        compiler_params=_params(),
        name="proj",
    )(*args)


def _softmax_pv(parts):
    m = functools.reduce(jnp.maximum, [jnp.max(s, axis=-1, keepdims=True) for s, _ in parts])
    num, den = None, None
    for s, v in parts:
        p = jnp.exp(s - m)
        l = jnp.sum(p, axis=-1, keepdims=True)
        o = jnp.dot(p.astype(BF16), v, preferred_element_type=F32)
        num = o if num is None else num + o
        den = l if den is None else den + l
    return num / den


def _qk(q, k):
    return lax.dot_general(q, k, (((1,), (1,)), ((), ())), preferred_element_type=F32)


def _attn_kernel(*refs, heads, dq, has_kpe):
    if has_kpe:
        q_ref, k_ref, kpe_ref, v_ref, o_ref = refs
    else:
        q_ref, k_ref, v_ref, o_ref = refs
        kpe_ref = None
    qt = pl.program_id(2)

    def attend(nk):
        k = k_ref[0:nk, :]
        if has_kpe:
            k = jnp.concatenate([k, kpe_ref[0:nk, :].astype(BF16)], axis=1)
        v = v_ref[0:nk, :]
        for h in range(heads):
            q = q_ref[:, h * dq:(h + 1) * dq]
            o = _softmax_pv([(_qk(q, k), v)])
            o_ref[:, h * HEAD_DIM:(h + 1) * HEAD_DIM] = o.astype(o_ref.dtype)

    @pl.when(qt == 0)
    def _():
        attend(CTX_LEN)

    @pl.when(qt > 0)
    def _():
        attend(TOK)


def _attention(q_arr, k_arr, v_arr, *, batch, kv_heads, heads_per_step, dq, q_blk0, k_blk0, v_blk0,
               kpe_arr=None, kpe_blk=0):
    qw = heads_per_step * dq
    ow = heads_per_step * HEAD_DIM
    args = [q_arr, k_arr]
    specs = [
        pl.BlockSpec((TM, qw), lambda b, h, t: (b * TILES_PER_BATCH + t, q_blk0 + h)),
        pl.BlockSpec((TOK, HEAD_DIM), lambda b, h, t: (b, k_blk0 + h)),
    ]
    if kpe_arr is not None:
        args.append(kpe_arr)
        specs.append(pl.BlockSpec((TOK, LANES), lambda b, h, t: (b, kpe_blk)))
    args.append(v_arr)
    specs.append(pl.BlockSpec((TOK, HEAD_DIM), lambda b, h, t: (b, v_blk0 + h)))
    kern = functools.partial(_attn_kernel, heads=heads_per_step, dq=dq, has_kpe=kpe_arr is not None)
    return pl.pallas_call(
        kern,
        grid=(batch, kv_heads, TILES_PER_BATCH),
        in_specs=specs,
        out_specs=pl.BlockSpec((TM, ow), lambda b, h, t: (b * TILES_PER_BATCH + t, h)),
        out_shape=jax.ShapeDtypeStruct((batch * TOK, kv_heads * ow), BF16),
        compiler_params=_params(),
        name="attention",
    )(*args)


def _na_kernel(q_ref, k_ref, v_ref, tbl_ref, o_ref):
    t = pl.program_id(2)
    n_lat_tiles = GRID_ROWS // NA_Q_ROWS

    @pl.when(t == n_lat_tiles)
    def _():
        q = q_ref[...]
        o = _softmax_pv([(_qk(q, k_ref[0:CTX_LEN, :]), v_ref[0:CTX_LEN, :])])
        o_ref[...] = o.astype(o_ref.dtype)

    @pl.when(t < n_lat_tiles)
    def _():
        q = q_ref[...]
        k_row0 = jnp.clip(NA_Q_ROWS * t - NA_WIN_ROWS // 2, 0, GRID_ROWS - NA_K_ROWS)
        start = pl.multiple_of(CTX_LEN + k_row0 * GRID_W, GRID_W)
        which = jnp.where(t == 0, 0, jnp.where(t == n_lat_tiles - 1, 2, 1))
        s_ctx = _qk(q, k_ref[0:CTX_LEN, :])
        s_loc = _qk(q, k_ref[pl.ds(start, NA_KEYS), :]) + tbl_ref[which]
        o = _softmax_pv([(s_ctx, v_ref[0:CTX_LEN, :]), (s_loc, v_ref[pl.ds(start, NA_KEYS), :])])
        o_ref[...] = o.astype(o_ref.dtype)


def _na_bias_index():
    n_lat_tiles = GRID_ROWS // NA_Q_ROWS
    dr_all, cb_all, ok_all = [], [], []
    qi = np.arange(TM)
    ki = np.arange(NA_KEYS)
    for t in (0, 1, n_lat_tiles - 1):
        k_row0 = int(np.clip(NA_Q_ROWS * t - NA_WIN_ROWS // 2, 0, GRID_ROWS - NA_K_ROWS))
        qr, qc = NA_Q_ROWS * t + qi // GRID_W, qi % GRID_W
        kr, kc = k_row0 + ki // GRID_W, ki % GRID_W
        r0 = np.clip(qr - NA_WIN_ROWS // 2, 0, GRID_ROWS - NA_WIN_ROWS)
        c0 = np.clip(qc - NA_WIN_COLS // 2, 0, GRID_W - NA_WIN_COLS)
        ok = ((kr[None, :] >= r0[:, None]) & (kr[None, :] < r0[:, None] + NA_WIN_ROWS)
              & (kc[None, :] >= c0[:, None]) & (kc[None, :] < c0[:, None] + NA_WIN_COLS))
        dr_all.append(np.clip(kr[None, :] - qr[:, None] + NA_WIN_ROWS - 1, 0, 2 * NA_WIN_ROWS - 2))
        cb_all.append(np.clip(kc[None, :] - qc[:, None] + NA_WIN_COLS - 1, 0, 2 * NA_WIN_COLS - 2))
        ok_all.append(ok)
    return np.stack(dr_all), np.stack(cb_all), np.stack(ok_all)


def _na_attention(qkv, rpb, batch):
    dr, cb, ok = _na_bias_index()
    tbl = jnp.where(ok[None], rpb[:, dr, cb].astype(F32), NEG_BIG)
    n_lat_tiles = GRID_ROWS // NA_Q_ROWS

    def q_tile(b, t):
        return b * TILES_PER_BATCH + jnp.where(t == n_lat_tiles, 0, t + 1)

    return pl.pallas_call(
        _na_kernel,
        grid=(NA_HEADS, batch, n_lat_tiles + 1),
        in_specs=[
            pl.BlockSpec((TM, HEAD_DIM), lambda h, b, t: (q_tile(b, t), h)),
            pl.BlockSpec((TOK, HEAD_DIM), lambda h, b, t: (b, NA_HEADS + h)),
            pl.BlockSpec((TOK, HEAD_DIM), lambda h, b, t: (b, 2 * NA_HEADS + h)),
            pl.BlockSpec((None, 3, TM, NA_KEYS), lambda h, b, t: (h, 0, 0, 0)),
        ],
        out_specs=pl.BlockSpec((TM, HEAD_DIM), lambda h, b, t: (q_tile(b, t), h)),
        out_shape=jax.ShapeDtypeStruct((batch * TOK, NA_HEADS * HEAD_DIM), BF16),
        compiler_params=_params(),
        name="na_attention",
    )(qkv, qkv, qkv, tbl)


def _route(logits, bias):
    scores = 1.0 / (1.0 + jnp.exp(-logits))
    sel = scores + bias
    srow = [sel[e:e + 1, :] for e in range(N_EXPERTS)]
    crow = [scores[e:e + 1, :] for e in range(N_EXPERTS)]
    gscore = []
    for g in range(N_GROUPS):
        a, b, c, d = srow[GROUP_SIZE * g:GROUP_SIZE * (g + 1)]
        gscore.append(jnp.maximum(jnp.maximum(jnp.maximum(a + b, a + c), jnp.maximum(a + d, b + c)),
                                  jnp.maximum(b + d, c + d)))
    best, gidx = gscore[0], jnp.zeros(gscore[0].shape, jnp.int32)
    for g in range(1, N_GROUPS):
        better = gscore[g] > best
        gidx = jnp.where(better, g, gidx)
        best = jnp.where(better, gscore[g], best)
    s, c = [], []
    for j in range(GROUP_SIZE):
        sj, cj = srow[j], crow[j]
        for g in range(1, N_GROUPS):
            sj = jnp.where(gidx == g, srow[GROUP_SIZE * g + j], sj)
            cj = jnp.where(gidx == g, crow[GROUP_SIZE * g + j], cj)
        s.append(sj)
        c.append(cj)
    chosen = []
    for j in range(GROUP_SIZE):
        rank = jnp.zeros(s[j].shape, jnp.int32)
        for k in range(GROUP_SIZE):
            if k < j:
                rank = rank + jnp.where(s[k] >= s[j], 1, 0)
            elif k > j:
                rank = rank + jnp.where(s[k] > s[j], 1, 0)
        chosen.append(rank < 2)
    w = [jnp.where(chosen[j], c[j], 0.0) for j in range(GROUP_SIZE)]
    denom = (w[0] + w[1]) + (w[2] + w[3])
    gate = [w[j] / denom for j in range(GROUP_SIZE)]
    lo = jnp.where(chosen[0], 0, jnp.where(chosen[1], 1, 2))
    hi = jnp.where(chosen[3], 3, jnp.where(chosen[2], 2, 1))
    g_lo = jnp.where(chosen[0], gate[0], jnp.where(chosen[1], gate[1], gate[2]))
    g_hi = jnp.where(chosen[3], gate[3], jnp.where(chosen[2], gate[2], gate[1]))
    pair = jnp.where(lo == 0, hi - 1, jnp.where(lo == 1, hi + 1, N_PAIRS - 1))
    return gidx * N_PAIRS + pair, g_lo, g_hi


def _ffn_prep_kernel(h_ref, gain_ref, sc_ref, sh_ref, rw_ref, rb_ref, v_ref, part_ref, gates_ref):
    v = _rms(h_ref[...], gain_ref[...]) * (1.0 + sc_ref[...]) + sh_ref[...]
    v_ref[...] = v
    logits = lax.dot_general(rw_ref[...], v, (((1,), (1,)), ((), ())),
                             precision=lax.Precision.HIGHEST, preferred_element_type=F32)
    part, g_lo, g_hi = _route(logits, rb_ref[...])
    part_ref[...] = jnp.zeros(part_ref.shape, jnp.int32)
    part_ref[0:1, :] = part
    gates_ref[...] = jnp.zeros(gates_ref.shape, F32)
    gates_ref[0:1, :] = g_lo
    gates_ref[1:2, :] = g_hi


def _ffn_prep(h, gain, mod, router_wt, router_bias):
    n = h.shape[0]
    return pl.pallas_call(
        _ffn_prep_kernel,
        grid=(n // TM,),
        in_specs=[
            pl.BlockSpec((TM, D_MODEL), lambda i: (i, 0)),
            pl.BlockSpec((1, D_MODEL), lambda i: (0, 0)),
            pl.BlockSpec((None, 1, D_MODEL), lambda i: (4 * MOD_ROWS + _mod_row(i), 0, 0)),
            pl.BlockSpec((None, 1, D_MODEL), lambda i: (3 * MOD_ROWS + _mod_row(i), 0, 0)),
            pl.BlockSpec((N_EXPERTS, D_MODEL), lambda i: (0, 0)),
            pl.BlockSpec((N_EXPERTS, 1), lambda i: (0, 0)),
        ],
        out_specs=[
            pl.BlockSpec((TM, D_MODEL), lambda i: (i, 0)),
            pl.BlockSpec((8, TM), lambda i: (0, i)),
            pl.BlockSpec((8, TM), lambda i: (0, i)),
        ],
        out_shape=[
            jax.ShapeDtypeStruct((n, D_MODEL), F32),
            jax.ShapeDtypeStruct((8, n), jnp.int32),
            jax.ShapeDtypeStruct((8, n), F32),
        ],
        compiler_params=_params(),
        name="ffn_prep",
    )(h, gain.reshape(1, D_MODEL), mod, mod, router_wt, router_bias.reshape(N_EXPERTS, 1))


def _dispatch_meta(part, n_slots):
    n = part.shape[0]
    onehot = (part[:, None] == jnp.arange(N_PARTS, dtype=jnp.int32)[None, :]).astype(jnp.int32)
    csum = jnp.cumsum(onehot, axis=0)
    rank = jnp.take_along_axis(csum, part[:, None], axis=1)[:, 0] - 1
    counts = csum[-1]
    padded = ((counts + TM - 1) // TM) * TM
    ends = jnp.cumsum(padded)
    starts = ends - padded
    pos = (starts[part] + rank).astype(jnp.int32)
    src = jnp.zeros((n_slots,), jnp.int32).at[pos].set(jnp.arange(n, dtype=jnp.int32))
    tile_start = jnp.arange(n_slots // TM, dtype=jnp.int32) * TM
    tile_part = jnp.searchsorted(ends, tile_start, side="right").astype(jnp.int32)
    valid = (tile_start < ends[-1]).astype(jnp.int32)
    last_part = jnp.searchsorted(ends, ends[-1] - 1, side="right").astype(jnp.int32)
    tile_part = jnp.minimum(tile_part, last_part)
    group, pair = tile_part // N_PAIRS, tile_part % N_PAIRS
    e_lo = group * GROUP_SIZE + jnp.asarray(PAIR_LO, jnp.int32)[pair]
    e_hi = group * GROUP_SIZE + jnp.asarray(PAIR_HI, jnp.int32)[pair]
    return pos, src, e_lo, e_hi, valid


def _gather_kernel(idx_ref, src_ref, *refs, resid):
    if resid:
        res_ref, gate_ref, o_ref, buf, sem = refs
    else:
        o_ref, buf, sem = refs
    base = pl.program_id(0) * TM

    def row_copy(r, row):
        return pltpu.make_async_copy(src_ref.at[pl.ds(row, 1), :], buf.at[pl.ds(r, 1), :], sem)

    def issue(r, carry):
        row_copy(r, idx_ref[base + r]).start()
        return carry

    def wait(r, carry):
        row_copy(r, 0).wait()
        return carry

    lax.fori_loop(0, TM, issue, 0)
    lax.fori_loop(0, TM, wait, 0)
    if resid:
        o_ref[...] = res_ref[...] + gate_ref[...] * buf[...]
    else:
        o_ref[...] = buf[...].astype(o_ref.dtype)


def _gather_rows(idx, src, *, out_dtype, res=None, mod=None, mod_gate=None):
    n_out = idx.shape[0]
    width = src.shape[1]
    args = [idx, src]
    specs = [pl.BlockSpec(memory_space=pl.ANY)]
    if res is not None:
        args += [res, mod]
        specs += [
            pl.BlockSpec((TM, width), lambda i, idx_ref: (i, 0)),
            pl.BlockSpec((None, 1, width), lambda i, idx_ref: (mod_gate * MOD_ROWS + _mod_row(i), 0, 0)),
        ]
    return pl.pallas_call(
        functools.partial(_gather_kernel, resid=res is not None),
        grid_spec=pltpu.PrefetchScalarGridSpec(
            num_scalar_prefetch=1,
            grid=(n_out // TM,),
            in_specs=specs,
            out_specs=pl.BlockSpec((TM, width), lambda i, idx_ref: (i, 0)),
            scratch_shapes=[pltpu.VMEM((TM, width), src.dtype), pltpu.SemaphoreType.DMA(())],
        ),
        out_shape=jax.ShapeDtypeStruct((n_out, width), out_dtype),
        compiler_params=_params(),
        name="gather_rows",
    )(*args)


def _expert_kernel(elo_ref, ehi_ref, valid_ref, x_ref, g_ref, w1a, w3a, w2a, w1b, w3b, w2b, y_ref):
    t = pl.program_id(0)

    @pl.when(valid_ref[t] == 0)
    def _():
        y_ref[...] = jnp.zeros(y_ref.shape, y_ref.dtype)

    @pl.when(valid_ref[t] != 0)
    def _():
        x = x_ref[...]

        def hidden(w1, w3, gate):
            a = jnp.dot(x, w1[...], preferred_element_type=F32)
            b = jnp.dot(x, w3[...], preferred_element_type=F32)
            return (_silu(a) * b * gate).astype(BF16)

        g = g_ref[...]
        y = jnp.dot(hidden(w1a, w3a, g[:, 0:1]), w2a[...], preferred_element_type=F32)
        y = y + jnp.dot(hidden(w1b, w3b, g[:, 1:2]), w2b[...], preferred_element_type=F32)
        y_ref[...] = y


def _experts(xs, gs, e_lo, e_hi, valid, w1, w3, w2):
    n_slots = xs.shape[0]
    up = lambda sel: pl.BlockSpec((None, D_MODEL, D_EXPERT), lambda t, lo, hi, ok: ((lo, hi)[sel][t], 0, 0))
    down = lambda sel: pl.BlockSpec((None, D_EXPERT, D_MODEL), lambda t, lo, hi, ok: ((lo, hi)[sel][t], 0, 0))
    return pl.pallas_call(
        _expert_kernel,
        grid_spec=pltpu.PrefetchScalarGridSpec(
            num_scalar_prefetch=3,
            grid=(n_slots // TM,),
            in_specs=[
                pl.BlockSpec((TM, D_MODEL), lambda t, lo, hi, ok: (t, 0)),
                pl.BlockSpec((TM, 2), lambda t, lo, hi, ok: (t, 0)),
                up(0), up(0), down(0), up(1), up(1), down(1),
            ],
            out_specs=pl.BlockSpec((TM, D_MODEL), lambda t, lo, hi, ok: (t, 0)),
        ),
        out_shape=jax.ShapeDtypeStruct((n_slots, D_MODEL), F32),
        compiler_params=_params(),
        name="experts",
    )(e_lo, e_hi, valid, xs, gs, w1, w3, w2, w1, w3, w2)


def _moe(h, gain, mod, router_wt, router_bias, w1, w3, w2):
    n = h.shape[0]
    n_slots = n + N_PARTS * TM
    v, part, gates = _ffn_prep(h, gain, mod, router_wt, router_bias)
    pos, src, e_lo, e_hi, valid = _dispatch_meta(part[0], n_slots)
    gs = jnp.stack([gates[0], gates[1]], axis=1)[src]
    xs = _gather_rows(src, v, out_dtype=BF16)
    ys = _experts(xs, gs, e_lo, e_hi, valid, w1, w3, w2)
    return _gather_rows(pos, ys, out_dtype=F32, res=h, mod=mod, mod_gate=5)


def _final_kernel(h_ref, gain_ref, o_ref):
    o_ref[...] = _rms(h_ref[...], gain_ref[...])


def _final_norm(h, gain, batch):
    lat_tiles = SEQ // TM
    return pl.pallas_call(
        _final_kernel,
        grid=(batch, lat_tiles),
        in_specs=[
            pl.BlockSpec((TM, D_MODEL), lambda b, t: (b * TILES_PER_BATCH + 1 + t, 0)),
            pl.BlockSpec((1, D_MODEL), lambda b, t: (0, 0)),
        ],
        out_specs=pl.BlockSpec((TM, D_MODEL), lambda b, t: (b * lat_tiles + t, 0)),
        out_shape=jax.ShapeDtypeStruct((batch * SEQ, D_MODEL), F32),
        compiler_params=_params(),
        name="final_norm",
    )(h, gain.reshape(1, D_MODEL))


def _rope_tables(rot_dim, layout):
    n_freq = rot_dim // 4
    inv_freq = 1.0 / (ROPE_THETA ** (jnp.arange(n_freq, dtype=F32) / n_freq))
    t = jnp.arange(SEQ, dtype=jnp.int32)
    row = (t // GRID_W).astype(F32)
    col = (t % GRID_W).astype(F32)
    ang = jnp.concatenate([row[:, None] * inv_freq, col[:, None] * inv_freq], axis=-1)
    cos, sin = jnp.cos(ang), jnp.sin(ang)
    if layout == "padded":
        pad1, pad0 = jnp.ones_like(cos), jnp.zeros_like(sin)
        c = jnp.concatenate([cos, pad1, cos, pad1], axis=-1)
        s = jnp.concatenate([-sin, pad0, sin, pad0], axis=-1)
    else:
        c = jnp.concatenate([cos, cos], axis=-1)
        s = jnp.concatenate([-sin, sin], axis=-1)
    c = jnp.concatenate([jnp.ones((CTX_LEN, LANES), F32), c], axis=0)
    s = jnp.concatenate([jnp.zeros((CTX_LEN, LANES), F32), s], axis=0)
    return c, s


def _pad_rope_cols(w):
    half = MLA_ROPE // 2
    z = jnp.zeros(w.shape[:-1] + (half,), w.dtype)
    return jnp.concatenate([w[..., :half], z, w[..., half:], z], axis=-1)


def _gqa_layer(h, mod, norm_gain, w_qkv, q_gain, k_gain, w_o, tabs, batch):
    kinds = [("normrope", 0, 0)] * GQA_HEADS + [("normrope", 1, 1)] * GQA_KV_HEADS + [("plain",)] * GQA_KV_HEADS
    qkv = _proj(h, w_qkv.astype(BF16), out_dtype=BF16, kinds=kinds, prologue="normmod", gain=norm_gain,
                mod=mod, mod_sc=1, mod_sh=0, tabs=tabs, hgain=jnp.stack([q_gain, k_gain]))
    o = _attention(qkv, qkv, qkv, batch=batch, kv_heads=GQA_KV_HEADS, heads_per_step=GQA_GROUP, dq=HEAD_DIM,
                   q_blk0=0, k_blk0=GQA_HEADS, v_blk0=GQA_HEADS + GQA_KV_HEADS)
    return _proj(o, w_o.astype(BF16), out_dtype=F32, kinds=[("resid",)] * (D_MODEL // LANES),
                 res=h, mod=mod, mod_gate=2)


def _mla_layer(h, mod, norm_gain, w_a, q_gain, kv_gain, w_uq, w_ukv, w_o, tabs, batch):
    n_lat = 2 * MLA_RANK // LANES
    w_a_p = jnp.concatenate([w_a[:, :2 * MLA_RANK], _pad_rope_cols(w_a[:, 2 * MLA_RANK:])], axis=1)
    a = _proj(h, w_a_p.astype(BF16), out_dtype=F32, kinds=[("plain",)] * n_lat + [("rope", 1)],
              prologue="normmod", gain=norm_gain, mod=mod, mod_sc=1, mod_sh=0, tabs=tabs, chunk=LANES * (n_lat + 1))
    scale = MLA_QK ** -0.5
    w_uq_h = w_uq.reshape(MLA_RANK, MLA_HEADS, MLA_QK)
    w_uq_p = jnp.concatenate([w_uq_h[..., :MLA_NOPE], _pad_rope_cols(w_uq_h[..., MLA_NOPE:])], axis=-1)
    q = _proj(a, w_uq_p.reshape(MLA_RANK, MLA_HEADS * 2 * LANES).astype(BF16), out_dtype=BF16,
              kinds=[("scale", scale), ("rope", 0)] * MLA_HEADS, prologue="rms", x_col_block=0, gain=q_gain, tabs=tabs)
    w_ukv_h = w_ukv.reshape(MLA_RANK, MLA_HEADS, 2 * HEAD_DIM)
    w_ukv_p = jnp.concatenate([w_ukv_h[..., :MLA_NOPE].reshape(MLA_RANK, -1),
                               w_ukv_h[..., MLA_NOPE:].reshape(MLA_RANK, -1)], axis=1)
    kv = _proj(a, w_ukv_p.astype(BF16), out_dtype=BF16, kinds=[("plain",)] * (2 * MLA_HEADS),
               prologue="rms", x_col_block=1, gain=kv_gain)
    o = _attention(q, kv, kv, batch=batch, kv_heads=MLA_HEADS, heads_per_step=1, dq=2 * LANES,
                   q_blk0=0, k_blk0=0, v_blk0=MLA_HEADS, kpe_arr=a, kpe_blk=n_lat)
    return _proj(o, w_o.astype(BF16), out_dtype=F32, kinds=[("resid",)] * (D_MODEL // LANES),
                 res=h, mod=mod, mod_gate=2)


def _na_layer(h, mod, norm_gain, w_qkv, rpb, w_o, batch):
    scale = HEAD_DIM ** -0.5
    kinds = [("scale", scale)] * NA_HEADS + [("plain",)] * (2 * NA_HEADS)
    qkv = _proj(h, w_qkv.astype(BF16), out_dtype=BF16, kinds=kinds, prologue="normmod", gain=norm_gain,
                mod=mod, mod_sc=1, mod_sh=0)
    o = _na_attention(qkv, rpb, batch)
    return _proj(o, w_o.astype(BF16), out_dtype=F32, kinds=[("resid",)] * (D_MODEL // LANES),
                 res=h, mod=mod, mod_gate=2)


def kernel(x, c, ctx, c_ctx, mod_w, mod_b, norm_mix, norm_ffn, gqa_wqkv, gqa_q_gain, gqa_k_gain, gqa_wo,
           mla_wa, mla_q_gain, mla_kv_gain, mla_wuq, mla_wukv, mla_wo, na_wqkv, na_rpb, na_wo,
           router_w, router_bias, moe_w1, moe_w3, moe_w2, final_gain):
    batch = x.shape[0]
    assert batch <= CTX_MOD_ROW and x.shape[1:] == (SEQ, D_MODEL) and ctx.shape[1:] == (CTX_LEN, D_MODEL)
    h = jnp.concatenate([ctx, x], axis=1).reshape(batch * TOK, D_MODEL)
    cond = jnp.zeros((MOD_ROWS, D_MODEL), F32).at[:batch].set(c).at[CTX_MOD_ROW].set(c_ctx)
    mods = _modulation(cond, mod_w, mod_b)
    mods = mods.reshape(DEPTH, MOD_ROWS, 6, D_MODEL).transpose(0, 2, 1, 3).reshape(DEPTH, 6 * MOD_ROWS, 1, D_MODEL)

    c_full, s_full = _rope_tables(HEAD_DIM, "full")
    gqa_scale = HEAD_DIM ** -0.5
    gqa_tabs = (jnp.concatenate([c_full * gqa_scale, c_full], axis=1),
                jnp.concatenate([s_full * gqa_scale, s_full], axis=1))
    c_pad, s_pad = _rope_tables(MLA_ROPE, "padded")
    mla_scale = MLA_QK ** -0.5
    mla_tabs = (jnp.concatenate([c_pad * mla_scale, c_pad], axis=1),
                jnp.concatenate([s_pad * mla_scale, s_pad], axis=1))
    router_wt = router_w.T

    for i in range(DEPTH):
        mod = mods[i]
        kind, j = i % N_MIXERS, i // N_MIXERS
        if kind == 0:
            h = _gqa_layer(h, mod, norm_mix[i], gqa_wqkv[j], gqa_q_gain[j], gqa_k_gain[j], gqa_wo[j],
                           gqa_tabs, batch)
        elif kind == 1:
            h = _mla_layer(h, mod, norm_mix[i], mla_wa[j], mla_q_gain[j], mla_kv_gain[j], mla_wuq[j],
                           mla_wukv[j], mla_wo[j], mla_tabs, batch)
        else:
            h = _na_layer(h, mod, norm_mix[i], na_wqkv[j], na_rpb[j], na_wo[j], batch)
        h = _moe(h, norm_ffn[i], mod, router_wt, router_bias,
                 moe_w1[i].astype(BF16), moe_w3[i].astype(BF16), moe_w2[i].astype(BF16))
    return _final_norm(h, final_gain, batch).reshape(batch, SEQ, D_MODEL)
```

```python
import functools

import numpy as np
import jax
import jax.numpy as jnp
from jax import lax
from jax.experimental import pallas as pl
from jax.experimental.pallas import tpu as pltpu

F32 = jnp.float32
BF16 = jnp.bfloat16

D_MODEL = 2048
SEQ = 2048
GRID_W = 64
GRID_ROWS = SEQ // GRID_W
CTX_LEN = 256
TOK = CTX_LEN + SEQ
DEPTH = 4
N_MIXERS = 3
EPS = 1e-6
ROPE_THETA = 10000.0

GQA_HEADS = 16
GQA_KV_HEADS = 4
HEAD_DIM = 128
GQA_GROUP = GQA_HEADS // GQA_KV_HEADS

MLA_HEADS = 16
MLA_RANK = 512
MLA_NOPE = 128
MLA_ROPE = 64
MLA_QK = MLA_NOPE + MLA_ROPE

NA_HEADS = 16
NA_WIN_ROWS = 8
NA_WIN_COLS = 16
NA_Q_ROWS = 4
NA_K_ROWS = 12
NA_KEYS = NA_K_ROWS * GRID_W
NA_HEADS_PER_STEP = 4
MLA_HEADS_PER_STEP = 4
KEY_CHUNK = 256

N_EXPERTS = 16
N_GROUPS = 4
GROUP_SIZE = N_EXPERTS // N_GROUPS
N_PAIRS = 6
N_PARTS = N_GROUPS * N_PAIRS
D_EXPERT = 512
PAIR_LO = (0, 0, 0, 1, 1, 2)
PAIR_HI = (1, 2, 3, 2, 3, 3)

LANES = 128
TM = 256
TILES_PER_BATCH = TOK // TM
GATHER_UNROLL = 16
MOD_ROWS = 16
CTX_MOD_ROW = 8
NEG_BIG = -1e30
LOG2E = 1.4426950408889634
VMEM_LIMIT = 56 * 1024 * 1024


def _params():
    return pltpu.CompilerParams(vmem_limit_bytes=VMEM_LIMIT)


def _mod_row(i):
    return jnp.where(i % TILES_PER_BATCH == 0, CTX_MOD_ROW, i // TILES_PER_BATCH)


def _silu(x):
    return x / (1.0 + jnp.exp(-x))


def _mod_kernel(cond_ref, w_ref, b_ref, o_ref):
    s = _silu(cond_ref[...]).astype(BF16)
    o_ref[...] = jnp.dot(s, w_ref[...].astype(BF16), preferred_element_type=F32) + b_ref[...]


def _modulation(cond, mod_w, mod_b):
    tn = 1024
    n_out = mod_w.shape[-1]
    return pl.pallas_call(
        _mod_kernel,
        grid=(DEPTH, n_out // tn),
        in_specs=[
            pl.BlockSpec((MOD_ROWS, D_MODEL), lambda l, j: (0, 0)),
            pl.BlockSpec((None, D_MODEL, tn), lambda l, j: (l, 0, j)),
            pl.BlockSpec((None, 1, tn), lambda l, j: (l, 0, j)),
        ],
        out_specs=pl.BlockSpec((None, MOD_ROWS, tn), lambda l, j: (l, 0, j)),
        out_shape=jax.ShapeDtypeStruct((DEPTH, MOD_ROWS, n_out), F32),
        compiler_params=_params(),
        name="modulation",
    )(cond, mod_w, mod_b.reshape(DEPTH, 1, n_out))


def _rms(x, gain):
    return x * lax.rsqrt(jnp.mean(x * x, axis=-1, keepdims=True) + EPS) * gain


def _rope(x, c, s):
    return x * c + pltpu.roll(x, LANES // 2, 1) * s


def _runs(kinds):
    out, start = [], 0
    for k in range(1, len(kinds) + 1):
        if k == len(kinds) or kinds[k] != kinds[start] or kinds[start][0] in ("rope", "normrope", "vt"):
            out.append((start, k, kinds[start]))
            start = k
    return out


def _proj_kernel(*refs, prologue, kinds, chunk, has_tab, has_hgain, resid, has_vt):
    it = iter(refs)
    x_ref = next(it)
    gain_ref = sc_ref = sh_ref = None
    if prologue == "normmod":
        gain_ref, sc_ref, sh_ref = next(it), next(it), next(it)
    elif prologue == "rms":
        gain_ref = next(it)
    w_ref = next(it)
    ctab_ref = stab_ref = hgain_ref = res_ref = gate_ref = None
    if has_tab:
        ctab_ref, stab_ref = next(it), next(it)
    if has_hgain:
        hgain_ref = next(it)
    if resid:
        res_ref, gate_ref = next(it), next(it)
    o_ref = next(it)
    vt_ref = next(it) if has_vt else None

    if prologue == "none":
        u = x_ref[...]
    else:
        y = _rms(x_ref[...], gain_ref[...])
        if prologue == "normmod":
            y = y * (1.0 + sc_ref[...]) + sh_ref[...]
        u = y.astype(BF16)

    for c0 in range(0, len(kinds) * LANES, chunk):
        acc = jnp.dot(u, w_ref[:, c0:c0 + chunk], preferred_element_type=F32)
        b0 = c0 // LANES
        for lo, hi, kind in _runs(kinds[b0:b0 + chunk // LANES]):
            a = acc[:, lo * LANES:hi * LANES]
            cols = slice(c0 + lo * LANES, c0 + hi * LANES)
            if kind[0] == "scale":
                a = a * kind[1]
            elif kind[0] == "rope":
                t = kind[1]
                a = _rope(a, ctab_ref[:, t * LANES:(t + 1) * LANES], stab_ref[:, t * LANES:(t + 1) * LANES])
            elif kind[0] == "normrope":
                t, g = kind[1], kind[2]
                a = _rms(a, hgain_ref[g:g + 1, :])
                a = _rope(a, ctab_ref[:, t * LANES:(t + 1) * LANES], stab_ref[:, t * LANES:(t + 1) * LANES])
            elif kind[0] == "resid":
                a = res_ref[:, cols] + gate_ref[:, cols] * a
            elif kind[0] == "vt":
                vt_ref[kind[1] * LANES:(kind[1] + 1) * LANES, :] = a.T.astype(vt_ref.dtype)
                continue
            o_ref[:, cols] = a.astype(o_ref.dtype)


def _proj(x, w, *, out_dtype, kinds, prologue="none", x_col_block=0, gain=None, mod=None,
          mod_sc=None, mod_sh=None, tabs=None, hgain=None, res=None, mod_gate=None, chunk=512):
    n = x.shape[0]
    k, n_all = w.shape
    n_vt = sum(kind[0] == "vt" for kind in kinds)
    n_out = n_all - n_vt * LANES
    assert n % TM == 0 and n_all % chunk == 0 and len(kinds) == n_all // LANES
    assert all(kind[0] == "vt" for kind in kinds[n_out // LANES:])
    args = [x]
    specs = [pl.BlockSpec((TM, k), lambda i: (i, x_col_block))]
    if prologue in ("normmod", "rms"):
        args.append(gain.reshape(1, k))
        specs.append(pl.BlockSpec((1, k), lambda i: (0, 0)))
    if prologue == "normmod":
        for comp in (mod_sc, mod_sh):
            args.append(mod)
            specs.append(pl.BlockSpec((None, 1, k), lambda i, comp=comp: (comp * MOD_ROWS + _mod_row(i), 0, 0)))
    args.append(w)
    specs.append(pl.BlockSpec((k, n_all), lambda i: (0, 0), pipeline_mode=pl.Buffered(1)))
    if tabs is not None:
        for t in tabs:
            args.append(t)
            specs.append(pl.BlockSpec((TM, t.shape[1]), lambda i: (i % TILES_PER_BATCH, 0)))
    if hgain is not None:
        args.append(hgain)
        specs.append(pl.BlockSpec(hgain.shape, lambda i: (0, 0)))
    if res is not None:
        args.append(res)
        specs.append(pl.BlockSpec((TM, n_out), lambda i: (i, 0)))
        args.append(mod)
        specs.append(pl.BlockSpec((None, 1, n_out), lambda i: (mod_gate * MOD_ROWS + _mod_row(i), 0, 0)))
    kern = functools.partial(_proj_kernel, prologue=prologue, kinds=tuple(kinds), chunk=chunk,
                             has_tab=tabs is not None, has_hgain=hgain is not None, resid=res is not None,
                             has_vt=n_vt > 0)
    out_specs = [pl.BlockSpec((TM, n_out), lambda i: (i, 0))]
    out_shape = [jax.ShapeDtypeStruct((n, n_out), out_dtype)]
    if n_vt:
        out_specs.append(pl.BlockSpec((n_vt * LANES, TM), lambda i: (0, i)))
        out_shape.append(jax.ShapeDtypeStruct((n_vt * LANES, n), out_dtype))
    out = pl.pallas_call(
        kern,
        grid=(n // TM,),
        in_specs=specs,
        out_specs=out_specs,
        out_shape=out_shape,
        compiler_params=_params(),
        name="proj",
    )(*args)
    return out if n_vt else out[0]


def _kq(k, q):
    return lax.dot_general(k, q, (((1,), (1,)), ((), ())), preferred_element_type=F32)


def _pipelined_heads(n_heads, n_chunks, q_of, k_of, vt_of, bias_of, s_ref, store):
    def scores(h, c, slot):
        s = _kq(k_of(h, c), q_of(h))
        b = bias_of(h, c)
        s_ref[slot, c] = s if b is None else s + b

    for c in range(n_chunks):
        scores(0, c, 0)
    for h in range(n_heads):
        slot = h % 2
        m = functools.reduce(jnp.maximum,
                             [jnp.max(s_ref[slot, c], axis=0, keepdims=True) for c in range(n_chunks)])
        num, den = None, None
        for c in range(n_chunks):
            if h + 1 < n_heads:
                scores(h + 1, c, 1 - slot)
            p = jnp.exp2(s_ref[slot, c] - m)
            l = jnp.sum(p, axis=0, keepdims=True)
            o = jnp.dot(vt_of(h, c), p.astype(BF16), preferred_element_type=F32)
            num = o if num is None else num + o
            den = l if den is None else den + l
        store(h, (num / den).T)


def _attn_kernel(*refs, heads, dq, has_kpe):
    if has_kpe:
        q_ref, k_ref, kpe_ref, vt_ref, o_ref, s_ref, keys_ref = refs
    else:
        q_ref, k_ref, vt_ref, o_ref, s_ref = refs
        keys_ref = None
    qt = pl.program_id(2)

    if has_kpe:
        @pl.when(qt == 0)
        def _():
            kpe = kpe_ref[...].astype(BF16)
            for h in range(heads):
                keys_ref[h, :, 0:HEAD_DIM] = k_ref[:, h * HEAD_DIM:(h + 1) * HEAD_DIM]
                keys_ref[h, :, HEAD_DIM:] = kpe

    def q_of(h):
        return q_ref[:, h * dq:(h + 1) * dq]

    def k_of(h, c):
        rows = slice(c * KEY_CHUNK, (c + 1) * KEY_CHUNK)
        return keys_ref[h, rows, :] if has_kpe else k_ref[rows, :]

    def vt_of(h, c):
        r0 = h * HEAD_DIM if has_kpe else 0
        return vt_ref[r0:r0 + HEAD_DIM, c * KEY_CHUNK:(c + 1) * KEY_CHUNK]

    def store(h, o):
        o_ref[:, h * HEAD_DIM:(h + 1) * HEAD_DIM] = o.astype(o_ref.dtype)

    @pl.when(qt == 0)
    def _():
        _pipelined_heads(heads, CTX_LEN // KEY_CHUNK, q_of, k_of, vt_of, lambda h, c: None, s_ref, store)

    @pl.when(qt > 0)
    def _():
        _pipelined_heads(heads, TOK // KEY_CHUNK, q_of, k_of, vt_of, lambda h, c: None, s_ref, store)


def _attention(q_arr, k_arr, vt_arr, *, batch, groups, heads_per_step, dq, k_blk0, kpe_arr=None, kpe_blk=0):
    shared_kv = kpe_arr is None
    kw = HEAD_DIM if shared_kv else heads_per_step * HEAD_DIM
    ow = heads_per_step * HEAD_DIM
    args = [q_arr, k_arr]
    specs = [
        pl.BlockSpec((TM, heads_per_step * dq), lambda b, g, t: (b * TILES_PER_BATCH + t, g)),
        pl.BlockSpec((TOK, kw), lambda b, g, t: (b, k_blk0 + g)),
    ]
    scratch = [pltpu.VMEM((2, TOK // KEY_CHUNK, KEY_CHUNK, TM), F32)]
    if not shared_kv:
        args.append(kpe_arr)
        specs.append(pl.BlockSpec((TOK, LANES), lambda b, g, t: (b, kpe_blk)))
        scratch.append(pltpu.VMEM((heads_per_step, TOK, HEAD_DIM + LANES), BF16))
    args.append(vt_arr)
    specs.append(pl.BlockSpec((kw, TOK), lambda b, g, t: (g, b)))
    kern = functools.partial(_attn_kernel, heads=heads_per_step, dq=dq, has_kpe=not shared_kv)
    return pl.pallas_call(
        kern,
        grid=(batch, groups, TILES_PER_BATCH),
        in_specs=specs,
        out_specs=pl.BlockSpec((TM, ow), lambda b, g, t: (b * TILES_PER_BATCH + t, g)),
        out_shape=jax.ShapeDtypeStruct((batch * TOK, groups * ow), BF16),
        scratch_shapes=scratch,
        compiler_params=_params(),
        name="attention",
    )(*args)


def _na_kernel(q_ref, k_ref, vt_ref, tbl_ref, o_ref, s_ref):
    t = pl.program_id(2)
    n_lat_tiles = GRID_ROWS // NA_Q_ROWS
    k_row0 = jnp.clip(NA_Q_ROWS * t - NA_WIN_ROWS // 2, 0, GRID_ROWS - NA_K_ROWS)
    start = pl.multiple_of(CTX_LEN + k_row0 * GRID_W, KEY_CHUNK)
    which = jnp.where(t == 0, 0, jnp.where(t == n_lat_tiles - 1, 2, 1))

    def key_rows(c):
        return pl.ds(0, KEY_CHUNK) if c == 0 else pl.ds(start + (c - 1) * KEY_CHUNK, KEY_CHUNK)

    def q_of(h):
        return q_ref[:, h * HEAD_DIM:(h + 1) * HEAD_DIM]

    def k_of(h, c):
        return k_ref[key_rows(c), h * HEAD_DIM:(h + 1) * HEAD_DIM]

    def vt_of(h, c):
        return vt_ref[h * HEAD_DIM:(h + 1) * HEAD_DIM, key_rows(c)]

    def bias_of(h, c):
        return None if c == 0 else tbl_ref[h, which, (c - 1) * KEY_CHUNK:c * KEY_CHUNK, :]

    def store(h, o):
        o_ref[:, h * HEAD_DIM:(h + 1) * HEAD_DIM] = o.astype(o_ref.dtype)

    @pl.when(t == n_lat_tiles)
    def _():
        _pipelined_heads(NA_HEADS_PER_STEP, 1, q_of, k_of, vt_of, bias_of, s_ref, store)

    @pl.when(t < n_lat_tiles)
    def _():
        _pipelined_heads(NA_HEADS_PER_STEP, 1 + NA_KEYS // KEY_CHUNK, q_of, k_of, vt_of, bias_of, s_ref, store)


def _na_bias_table(rpb):
    cols = np.arange(GRID_W)
    c0 = np.clip(cols - NA_WIN_COLS // 2, 0, GRID_W - NA_WIN_COLS)
    col_ok = (cols[:, None] >= c0[None, :]) & (cols[:, None] < c0[None, :] + NA_WIN_COLS)
    cb = np.clip(cols[:, None] - cols[None, :] + NA_WIN_COLS - 1, 0, 2 * NA_WIN_COLS - 2)
    onehot = (cb[None] == np.arange(2 * NA_WIN_COLS - 1)[:, None, None]).astype(np.float32)
    planes = jnp.einsum("hdc,ckq->hdkq", rpb.astype(F32), onehot, precision=lax.Precision.HIGHEST)
    planes = jnp.where(col_ok[None, None], planes * LOG2E, NEG_BIG)
    n_rel = 2 * NA_WIN_ROWS - 1
    planes = jnp.concatenate([planes, jnp.full((NA_HEADS, 1, GRID_W, GRID_W), NEG_BIG, F32)], axis=1)
    n_lat_tiles = GRID_ROWS // NA_Q_ROWS
    which = np.full((3, NA_K_ROWS, NA_Q_ROWS), n_rel, np.int32)
    for kind, t in enumerate((0, 1, n_lat_tiles - 1)):
        k_row0 = int(np.clip(NA_Q_ROWS * t - NA_WIN_ROWS // 2, 0, GRID_ROWS - NA_K_ROWS))
        for kk in range(NA_K_ROWS):
            for i in range(NA_Q_ROWS):
                qr, kr = NA_Q_ROWS * t + i, k_row0 + kk
                r0 = int(np.clip(qr - NA_WIN_ROWS // 2, 0, GRID_ROWS - NA_WIN_ROWS))
                if r0 <= kr < r0 + NA_WIN_ROWS:
                    which[kind, kk, i] = kr - qr + NA_WIN_ROWS - 1
    tbl = planes[:, which]
    return tbl.transpose(0, 1, 2, 4, 3, 5).reshape(NA_HEADS, 3, NA_KEYS, TM)


def _na_attention(qk, vt, rpb, batch):
    tbl = _na_bias_table(rpb)
    n_lat_tiles = GRID_ROWS // NA_Q_ROWS

    def q_tile(b, t):
        return b * TILES_PER_BATCH + jnp.where(t == n_lat_tiles, 0, t + 1)

    gw = NA_HEADS_PER_STEP * HEAD_DIM
    n_groups = NA_HEADS // NA_HEADS_PER_STEP
    return pl.pallas_call(
        _na_kernel,
        grid=(n_groups, batch, n_lat_tiles + 1),
        in_specs=[
            pl.BlockSpec((TM, gw), lambda g, b, t: (q_tile(b, t), g)),
            pl.BlockSpec((TOK, gw), lambda g, b, t: (b, n_groups + g)),
            pl.BlockSpec((gw, TOK), lambda g, b, t: (g, b)),
            pl.BlockSpec((NA_HEADS_PER_STEP, 3, NA_KEYS, TM), lambda g, b, t: (g, 0, 0, 0)),
        ],
        out_specs=pl.BlockSpec((TM, gw), lambda g, b, t: (q_tile(b, t), g)),
        out_shape=jax.ShapeDtypeStruct((batch * TOK, NA_HEADS * HEAD_DIM), BF16),
        scratch_shapes=[pltpu.VMEM((2, 1 + NA_KEYS // KEY_CHUNK, KEY_CHUNK, TM), F32)],
        compiler_params=_params(),
        name="na_attention",
    )(qk, qk, vt, tbl)


def _route(logits, bias):
    scores = 1.0 / (1.0 + jnp.exp(-logits))
    sel = scores + bias
    srow = [sel[e:e + 1, :] for e in range(N_EXPERTS)]
    crow = [scores[e:e + 1, :] for e in range(N_EXPERTS)]
    gscore = []
    for g in range(N_GROUPS):
        a, b, c, d = srow[GROUP_SIZE * g:GROUP_SIZE * (g + 1)]
        gscore.append(jnp.maximum(jnp.maximum(jnp.maximum(a + b, a + c), jnp.maximum(a + d, b + c)),
                                  jnp.maximum(b + d, c + d)))
    best, gidx = gscore[0], jnp.zeros(gscore[0].shape, jnp.int32)
    for g in range(1, N_GROUPS):
        better = gscore[g] > best
        gidx = jnp.where(better, g, gidx)
        best = jnp.where(better, gscore[g], best)
    s, c = [], []
    for j in range(GROUP_SIZE):
        sj, cj = srow[j], crow[j]
        for g in range(1, N_GROUPS):
            sj = jnp.where(gidx == g, srow[GROUP_SIZE * g + j], sj)
            cj = jnp.where(gidx == g, crow[GROUP_SIZE * g + j], cj)
        s.append(sj)
        c.append(cj)
    chosen = []
    for j in range(GROUP_SIZE):
        rank = jnp.zeros(s[j].shape, jnp.int32)
        for k in range(GROUP_SIZE):
            if k < j:
                rank = rank + jnp.where(s[k] >= s[j], 1, 0)
            elif k > j:
                rank = rank + jnp.where(s[k] > s[j], 1, 0)
        chosen.append(rank < 2)
    w = [jnp.where(chosen[j], c[j], 0.0) for j in range(GROUP_SIZE)]
    denom = (w[0] + w[1]) + (w[2] + w[3])
    gate = [w[j] / denom for j in range(GROUP_SIZE)]
    lo = jnp.where(chosen[0], 0, jnp.where(chosen[1], 1, 2))
    hi = jnp.where(chosen[3], 3, jnp.where(chosen[2], 2, 1))
    g_lo = jnp.where(chosen[0], gate[0], jnp.where(chosen[1], gate[1], gate[2]))
    g_hi = jnp.where(chosen[3], gate[3], jnp.where(chosen[2], gate[2], gate[1]))
    pair = jnp.where(lo == 0, hi - 1, jnp.where(lo == 1, hi + 1, N_PAIRS - 1))
    return gidx * N_PAIRS + pair, g_lo, g_hi


def _ffn_prep_kernel(h_ref, gain_ref, sc_ref, sh_ref, rw_ref, rb_ref, v_ref, part_ref, gates_ref):
    v = _rms(h_ref[...], gain_ref[...]) * (1.0 + sc_ref[...]) + sh_ref[...]
    v_ref[...] = v
    logits = lax.dot_general(rw_ref[...], v, (((1,), (1,)), ((), ())),
                             precision=lax.Precision.HIGHEST, preferred_element_type=F32)
    part, g_lo, g_hi = _route(logits, rb_ref[...])
    part_ref[...] = jnp.zeros(part_ref.shape, jnp.int32)
    part_ref[0:1, :] = part
    gates_ref[...] = jnp.zeros(gates_ref.shape, F32)
    gates_ref[0:1, :] = g_lo
    gates_ref[1:2, :] = g_hi


def _ffn_prep(h, gain, mod, router_wt, router_bias):
    n = h.shape[0]
    return pl.pallas_call(
        _ffn_prep_kernel,
        grid=(n // TM,),
        in_specs=[
            pl.BlockSpec((TM, D_MODEL), lambda i: (i, 0)),
            pl.BlockSpec((1, D_MODEL), lambda i: (0, 0)),
            pl.BlockSpec((None, 1, D_MODEL), lambda i: (4 * MOD_ROWS + _mod_row(i), 0, 0)),
            pl.BlockSpec((None, 1, D_MODEL), lambda i: (3 * MOD_ROWS + _mod_row(i), 0, 0)),
            pl.BlockSpec((N_EXPERTS, D_MODEL), lambda i: (0, 0)),
            pl.BlockSpec((N_EXPERTS, 1), lambda i: (0, 0)),
        ],
        out_specs=[
            pl.BlockSpec((TM, D_MODEL), lambda i: (i, 0)),
            pl.BlockSpec((8, TM), lambda i: (0, i)),
            pl.BlockSpec((8, TM), lambda i: (0, i)),
        ],
        out_shape=[
            jax.ShapeDtypeStruct((n, D_MODEL), F32),
            jax.ShapeDtypeStruct((8, n), jnp.int32),
            jax.ShapeDtypeStruct((8, n), F32),
        ],
        compiler_params=_params(),
        name="ffn_prep",
    )(h, gain.reshape(1, D_MODEL), mod, mod, router_wt, router_bias.reshape(N_EXPERTS, 1))


def _dispatch_meta(part, n_slots):
    n = part.shape[0]
    onehot = (part[:, None] == jnp.arange(N_PARTS, dtype=jnp.int32)[None, :]).astype(jnp.int32)
    csum = jnp.cumsum(onehot, axis=0)
    rank = jnp.take_along_axis(csum, part[:, None], axis=1)[:, 0] - 1
    counts = csum[-1]
    padded = ((counts + TM - 1) // TM) * TM
    ends = jnp.cumsum(padded)
    starts = ends - padded
    pos = (starts[part] + rank).astype(jnp.int32)
    src = jnp.zeros((n_slots,), jnp.int32).at[pos].set(jnp.arange(n, dtype=jnp.int32))
    tile_start = jnp.arange(n_slots // TM, dtype=jnp.int32) * TM
    valid = (tile_start < ends[-1]).astype(jnp.int32)
    last_start = jnp.minimum(tile_start, ends[-1] - TM)
    tile_part = jnp.sum((ends[None, :] <= last_start[:, None]).astype(jnp.int32), axis=1)
    group, pair = tile_part // N_PAIRS, tile_part % N_PAIRS
    e_lo = group * GROUP_SIZE + jnp.asarray(PAIR_LO, jnp.int32)[pair]
    e_hi = group * GROUP_SIZE + jnp.asarray(PAIR_HI, jnp.int32)[pair]
    return pos, src, e_lo, e_hi, valid


def _gather_kernel(idx_ref, src_ref, *refs, resid):
    if resid:
        res_ref, gate_ref, o_ref, buf, sem = refs
    else:
        o_ref, buf, sem = refs
    base = pl.program_id(0) * TM

    def row_copy(r, row):
        return pltpu.make_async_copy(src_ref.at[pl.ds(row, 1), :], buf.at[pl.ds(r, 1), :], sem)

    def issue(c, carry):
        for u in range(GATHER_UNROLL):
            r = c * GATHER_UNROLL + u
            row_copy(r, idx_ref[base + r]).start(priority=u % 2)
        return carry

    def wait(c, carry):
        for u in range(GATHER_UNROLL):
            row_copy(c * GATHER_UNROLL + u, 0).wait()
        return carry

    lax.fori_loop(0, TM // GATHER_UNROLL, issue, 0)
    lax.fori_loop(0, TM // GATHER_UNROLL, wait, 0)
    if resid:
        o_ref[...] = res_ref[...] + gate_ref[...] * buf[...]
    else:
        o_ref[...] = buf[...].astype(o_ref.dtype)


def _gather_rows(idx, src, *, out_dtype, res=None, mod=None, mod_gate=None):
    n_out = idx.shape[0]
    width = src.shape[1]
    args = [idx, src]
    specs = [pl.BlockSpec(memory_space=pl.ANY)]
    if res is not None:
        args += [res, mod]
        specs += [
            pl.BlockSpec((TM, width), lambda i, idx_ref: (i, 0)),
            pl.BlockSpec((None, 1, width), lambda i, idx_ref: (mod_gate * MOD_ROWS + _mod_row(i), 0, 0)),
        ]
    return pl.pallas_call(
        functools.partial(_gather_kernel, resid=res is not None),
        grid_spec=pltpu.PrefetchScalarGridSpec(
            num_scalar_prefetch=1,
            grid=(n_out // TM,),
            in_specs=specs,
            out_specs=pl.BlockSpec((TM, width), lambda i, idx_ref: (i, 0)),
            scratch_shapes=[pltpu.VMEM((TM, width), src.dtype), pltpu.SemaphoreType.DMA(())],
        ),
        out_shape=jax.ShapeDtypeStruct((n_out, width), out_dtype),
        compiler_params=_params(),
        name="gather_rows",
    )(*args)


def _expert_kernel(elo_ref, ehi_ref, valid_ref, x_ref, g_ref, w1a, w3a, w2a, w1b, w3b, w2b, y_ref):
    t = pl.program_id(0)

    @pl.when(valid_ref[t] == 0)
    def _():
        y_ref[...] = jnp.zeros(y_ref.shape, y_ref.dtype)

    @pl.when(valid_ref[t] != 0)
    def _():
        x = x_ref[...]

        def hidden(w1, w3, gate):
            a = jnp.dot(x, w1[...], preferred_element_type=F32)
            b = jnp.dot(x, w3[...], preferred_element_type=F32)
            return (_silu(a) * b * gate).astype(BF16)

        g = g_ref[...]
        y = jnp.dot(hidden(w1a, w3a, g[:, 0:1]), w2a[...], preferred_element_type=F32)
        y = y + jnp.dot(hidden(w1b, w3b, g[:, 1:2]), w2b[...], preferred_element_type=F32)
        y_ref[...] = y


def _experts(xs, gs, e_lo, e_hi, valid, w1, w3, w2):
    n_slots = xs.shape[0]
    up = lambda sel: pl.BlockSpec((None, D_MODEL, D_EXPERT), lambda t, lo, hi, ok: ((lo, hi)[sel][t], 0, 0))
    down = lambda sel: pl.BlockSpec((None, D_EXPERT, D_MODEL), lambda t, lo, hi, ok: ((lo, hi)[sel][t], 0, 0))
    return pl.pallas_call(
        _expert_kernel,
        grid_spec=pltpu.PrefetchScalarGridSpec(
            num_scalar_prefetch=3,
            grid=(n_slots // TM,),
            in_specs=[
                pl.BlockSpec((TM, D_MODEL), lambda t, lo, hi, ok: (t, 0)),
                pl.BlockSpec((TM, 2), lambda t, lo, hi, ok: (t, 0)),
                up(0), up(0), down(0), up(1), up(1), down(1),
            ],
            out_specs=pl.BlockSpec((TM, D_MODEL), lambda t, lo, hi, ok: (t, 0)),
        ),
        out_shape=jax.ShapeDtypeStruct((n_slots, D_MODEL), F32),
        compiler_params=_params(),
        name="experts",
    )(e_lo, e_hi, valid, xs, gs, w1, w3, w2, w1, w3, w2)


def _moe(h, gain, mod, router_wt, router_bias, w1, w3, w2):
    n = h.shape[0]
    n_slots = n + N_PARTS * TM
    v, part, gates = _ffn_prep(h, gain, mod, router_wt, router_bias)
    pos, src, e_lo, e_hi, valid = _dispatch_meta(part[0], n_slots)
    gs = jnp.stack([gates[0], gates[1]], axis=1)[src]
    xs = _gather_rows(src, v, out_dtype=BF16)
    ys = _experts(xs, gs, e_lo, e_hi, valid, w1, w3, w2)
    return _gather_rows(pos, ys, out_dtype=F32, res=h, mod=mod, mod_gate=5)


def _final_kernel(h_ref, gain_ref, o_ref):
    o_ref[...] = _rms(h_ref[...], gain_ref[...])


def _final_norm(h, gain, batch):
    lat_tiles = SEQ // TM
    return pl.pallas_call(
        _final_kernel,
        grid=(batch, lat_tiles),
        in_specs=[
            pl.BlockSpec((TM, D_MODEL), lambda b, t: (b * TILES_PER_BATCH + 1 + t, 0)),
            pl.BlockSpec((1, D_MODEL), lambda b, t: (0, 0)),
        ],
        out_specs=pl.BlockSpec((TM, D_MODEL), lambda b, t: (b * lat_tiles + t, 0)),
        out_shape=jax.ShapeDtypeStruct((batch * SEQ, D_MODEL), F32),
        compiler_params=_params(),
        name="final_norm",
    )(h, gain.reshape(1, D_MODEL))


def _rope_tables(rot_dim, layout):
    n_freq = rot_dim // 4
    inv_freq = 1.0 / (ROPE_THETA ** (jnp.arange(n_freq, dtype=F32) / n_freq))
    t = jnp.arange(SEQ, dtype=jnp.int32)
    row = (t // GRID_W).astype(F32)
    col = (t % GRID_W).astype(F32)
    ang = jnp.concatenate([row[:, None] * inv_freq, col[:, None] * inv_freq], axis=-1)
    cos, sin = jnp.cos(ang), jnp.sin(ang)
    if layout == "padded":
        pad1, pad0 = jnp.ones_like(cos), jnp.zeros_like(sin)
        c = jnp.concatenate([cos, pad1, cos, pad1], axis=-1)
        s = jnp.concatenate([-sin, pad0, sin, pad0], axis=-1)
    else:
        c = jnp.concatenate([cos, cos], axis=-1)
        s = jnp.concatenate([-sin, sin], axis=-1)
    c = jnp.concatenate([jnp.ones((CTX_LEN, LANES), F32), c], axis=0)
    s = jnp.concatenate([jnp.zeros((CTX_LEN, LANES), F32), s], axis=0)
    return c, s


def _pad_rope_cols(w):
    half = MLA_ROPE // 2
    z = jnp.zeros(w.shape[:-1] + (half,), w.dtype)
    return jnp.concatenate([w[..., :half], z, w[..., half:], z], axis=-1)


def _gqa_layer(h, mod, norm_gain, w_qkv, q_gain, k_gain, w_o, tabs, batch):
    kinds = ([("normrope", 0, 0)] * GQA_HEADS + [("normrope", 1, 1)] * GQA_KV_HEADS
             + [("vt", j) for j in range(GQA_KV_HEADS)])
    qk, vt = _proj(h, w_qkv.astype(BF16), out_dtype=BF16, kinds=kinds, prologue="normmod", gain=norm_gain,
                   mod=mod, mod_sc=1, mod_sh=0, tabs=tabs, hgain=jnp.stack([q_gain, k_gain]))
    o = _attention(qk, qk, vt, batch=batch, groups=GQA_KV_HEADS, heads_per_step=GQA_GROUP, dq=HEAD_DIM,
                   k_blk0=GQA_HEADS)
    return _proj(o, w_o.astype(BF16), out_dtype=F32, kinds=[("resid",)] * (D_MODEL // LANES),
                 res=h, mod=mod, mod_gate=2)


def _mla_layer(h, mod, norm_gain, w_a, q_gain, kv_gain, w_uq, w_ukv, w_o, tabs, batch):
    n_lat = 2 * MLA_RANK // LANES
    w_a_p = jnp.concatenate([w_a[:, :2 * MLA_RANK], _pad_rope_cols(w_a[:, 2 * MLA_RANK:])], axis=1)
    a = _proj(h, w_a_p.astype(BF16), out_dtype=F32, kinds=[("plain",)] * n_lat + [("rope", 1)],
              prologue="normmod", gain=norm_gain, mod=mod, mod_sc=1, mod_sh=0, tabs=tabs, chunk=LANES * (n_lat + 1))
    scale = MLA_QK ** -0.5 * LOG2E
    w_uq_h = w_uq.reshape(MLA_RANK, MLA_HEADS, MLA_QK)
    w_uq_p = jnp.concatenate([w_uq_h[..., :MLA_NOPE], _pad_rope_cols(w_uq_h[..., MLA_NOPE:])], axis=-1)
    q = _proj(a, w_uq_p.reshape(MLA_RANK, MLA_HEADS * 2 * LANES).astype(BF16), out_dtype=BF16,
              kinds=[("scale", scale), ("rope", 0)] * MLA_HEADS, prologue="rms", x_col_block=0, gain=q_gain, tabs=tabs)
    w_ukv_h = w_ukv.reshape(MLA_RANK, MLA_HEADS, 2 * HEAD_DIM)
    w_ukv_p = jnp.concatenate([w_ukv_h[..., :MLA_NOPE].reshape(MLA_RANK, -1),
                               w_ukv_h[..., MLA_NOPE:].reshape(MLA_RANK, -1)], axis=1)
    k_nope, vt = _proj(a, w_ukv_p.astype(BF16), out_dtype=BF16,
                       kinds=[("plain",)] * MLA_HEADS + [("vt", j) for j in range(MLA_HEADS)],
                       prologue="rms", x_col_block=1, gain=kv_gain)
    o = _attention(q, k_nope, vt, batch=batch, groups=MLA_HEADS // MLA_HEADS_PER_STEP,
                   heads_per_step=MLA_HEADS_PER_STEP, dq=2 * LANES, k_blk0=0, kpe_arr=a, kpe_blk=n_lat)
    return _proj(o, w_o.astype(BF16), out_dtype=F32, kinds=[("resid",)] * (D_MODEL // LANES),
                 res=h, mod=mod, mod_gate=2)


def _na_layer(h, mod, norm_gain, w_qkv, rpb, w_o, batch):
    scale = HEAD_DIM ** -0.5 * LOG2E
    kinds = [("scale", scale)] * NA_HEADS + [("plain",)] * NA_HEADS + [("vt", j) for j in range(NA_HEADS)]
    qk, vt = _proj(h, w_qkv.astype(BF16), out_dtype=BF16, kinds=kinds, prologue="normmod", gain=norm_gain,
                   mod=mod, mod_sc=1, mod_sh=0)
    o = _na_attention(qk, vt, rpb, batch)
    return _proj(o, w_o.astype(BF16), out_dtype=F32, kinds=[("resid",)] * (D_MODEL // LANES),
                 res=h, mod=mod, mod_gate=2)


def kernel(x, c, ctx, c_ctx, mod_w, mod_b, norm_mix, norm_ffn, gqa_wqkv, gqa_q_gain, gqa_k_gain, gqa_wo,
           mla_wa, mla_q_gain, mla_kv_gain, mla_wuq, mla_wukv, mla_wo, na_wqkv, na_rpb, na_wo,
           router_w, router_bias, moe_w1, moe_w3, moe_w2, final_gain):
    batch = x.shape[0]
    assert batch <= CTX_MOD_ROW and x.shape[1:] == (SEQ, D_MODEL) and ctx.shape[1:] == (CTX_LEN, D_MODEL)
    h = jnp.concatenate([ctx, x], axis=1).reshape(batch * TOK, D_MODEL)
    cond = jnp.zeros((MOD_ROWS, D_MODEL), F32).at[:batch].set(c).at[CTX_MOD_ROW].set(c_ctx)
    mods = _modulation(cond, mod_w, mod_b)
    mods = mods.reshape(DEPTH, MOD_ROWS, 6, D_MODEL).transpose(0, 2, 1, 3).reshape(DEPTH, 6 * MOD_ROWS, 1, D_MODEL)

    c_full, s_full = _rope_tables(HEAD_DIM, "full")
    gqa_scale = HEAD_DIM ** -0.5 * LOG2E
    gqa_tabs = (jnp.concatenate([c_full * gqa_scale, c_full], axis=1),
                jnp.concatenate([s_full * gqa_scale, s_full], axis=1))
    c_pad, s_pad = _rope_tables(MLA_ROPE, "padded")
    mla_scale = MLA_QK ** -0.5 * LOG2E
    mla_tabs = (jnp.concatenate([c_pad * mla_scale, c_pad], axis=1),
                jnp.concatenate([s_pad * mla_scale, s_pad], axis=1))
    router_wt = router_w.T

    for i in range(DEPTH):
        mod = mods[i]
        kind, j = i % N_MIXERS, i // N_MIXERS
        if kind == 0:
            h = _gqa_layer(h, mod, norm_mix[i], gqa_wqkv[j], gqa_q_gain[j], gqa_k_gain[j], gqa_wo[j],
                           gqa_tabs, batch)
        elif kind == 1:
            h = _mla_layer(h, mod, norm_mix[i], mla_wa[j], mla_q_gain[j], mla_kv_gain[j], mla_wuq[j],
                           mla_wukv[j], mla_wo[j], mla_tabs, batch)
        else:
            h = _na_layer(h, mod, norm_mix[i], na_wqkv[j], na_rpb[j], na_wo[j], batch)
        h = _moe(h, norm_ffn[i], mod, router_wt, router_bias,
                 moe_w1[i].astype(BF16), moe_w3[i].astype(BF16), moe_w2[i].astype(BF16))
    return _final_norm(h, final_gain, batch).reshape(batch, SEQ, D_MODEL)
```

```python
import functools

import numpy as np
import jax
import jax.numpy as jnp
from jax import lax
from jax.experimental import pallas as pl
from jax.experimental.pallas import tpu as pltpu

F32 = jnp.float32
BF16 = jnp.bfloat16

D_MODEL = 2048
SEQ = 2048
GRID_W = 64
GRID_ROWS = SEQ // GRID_W
CTX_LEN = 256
TOK = CTX_LEN + SEQ
DEPTH = 4
N_MIXERS = 3
EPS = 1e-6
ROPE_THETA = 10000.0

GQA_HEADS = 16
GQA_KV_HEADS = 4
HEAD_DIM = 128
GQA_GROUP = GQA_HEADS // GQA_KV_HEADS

MLA_HEADS = 16
MLA_RANK = 512
MLA_NOPE = 128
MLA_ROPE = 64
MLA_QK = MLA_NOPE + MLA_ROPE

NA_HEADS = 16
NA_WIN_ROWS = 8
NA_WIN_COLS = 16
NA_Q_ROWS = 4
NA_K_ROWS = 12
NA_KEYS = NA_K_ROWS * GRID_W
NA_HEADS_PER_STEP = 4
MLA_HEADS_PER_STEP = 4
KEY_CHUNK = 256

N_EXPERTS = 16
N_GROUPS = 4
GROUP_SIZE = N_EXPERTS // N_GROUPS
N_PAIRS = 6
N_PARTS = N_GROUPS * N_PAIRS
D_EXPERT = 512
PAIR_LO = (0, 0, 0, 1, 1, 2)
PAIR_HI = (1, 2, 3, 2, 3, 3)

LANES = 128
TM = 256
TILES_PER_BATCH = TOK // TM
SLAB_ROWS = D_MODEL // LANES
SLAB_PITCH = SLAB_ROWS + 1
ROW_GROUP = 8
MOD_ROWS = 16
CTX_MOD_ROW = 8
NEG_BIG = -1e30
LOG2E = 1.4426950408889634
VMEM_LIMIT = 56 * 1024 * 1024


def _params():
    return pltpu.CompilerParams(vmem_limit_bytes=VMEM_LIMIT)


def _mod_row(i):
    return jnp.where(i % TILES_PER_BATCH == 0, CTX_MOD_ROW, i // TILES_PER_BATCH)


def _silu(x):
    return x / (1.0 + jnp.exp(-x))


def _mod_kernel(cond_ref, w_ref, b_ref, o_ref):
    s = _silu(cond_ref[...]).astype(BF16)
    o_ref[...] = jnp.dot(s, w_ref[...].astype(BF16), preferred_element_type=F32) + b_ref[...]


def _modulation(cond, mod_w, mod_b):
    tn = 1024
    n_out = mod_w.shape[-1]
    return pl.pallas_call(
        _mod_kernel,
        grid=(DEPTH, n_out // tn),
        in_specs=[
            pl.BlockSpec((MOD_ROWS, D_MODEL), lambda l, j: (0, 0)),
            pl.BlockSpec((None, D_MODEL, tn), lambda l, j: (l, 0, j)),
            pl.BlockSpec((None, 1, tn), lambda l, j: (l, 0, j)),
        ],
        out_specs=pl.BlockSpec((None, MOD_ROWS, tn), lambda l, j: (l, 0, j)),
        out_shape=jax.ShapeDtypeStruct((DEPTH, MOD_ROWS, n_out), F32),
        compiler_params=_params(),
        name="modulation",
    )(cond, mod_w, mod_b.reshape(DEPTH, 1, n_out))


def _rms(x, gain):
    return x * lax.rsqrt(jnp.mean(x * x, axis=-1, keepdims=True) + EPS) * gain


def _rope(x, c, s):
    return x * c + pltpu.roll(x, LANES // 2, 1) * s


def _runs(kinds):
    out, start = [], 0
    for k in range(1, len(kinds) + 1):
        if k == len(kinds) or kinds[k] != kinds[start] or kinds[start][0] in ("rope", "normrope", "vt"):
            out.append((start, k, kinds[start]))
            start = k
    return out


def _proj_kernel(*refs, prologue, kinds, chunk, has_tab, has_hgain, resid, has_vt):
    it = iter(refs)
    x_ref = next(it)
    gain_ref = sc_ref = sh_ref = None
    if prologue == "normmod":
        gain_ref, sc_ref, sh_ref = next(it), next(it), next(it)
    elif prologue == "rms":
        gain_ref = next(it)
    w_ref = next(it)
    ctab_ref = stab_ref = hgain_ref = res_ref = gate_ref = None
    if has_tab:
        ctab_ref, stab_ref = next(it), next(it)
    if has_hgain:
        hgain_ref = next(it)
    if resid:
        res_ref, gate_ref = next(it), next(it)
    o_ref = next(it)
    vt_ref = next(it) if has_vt else None

    if prologue == "none":
        u = x_ref[...]
    else:
        y = _rms(x_ref[...], gain_ref[...])
        if prologue == "normmod":
            y = y * (1.0 + sc_ref[...]) + sh_ref[...]
        u = y.astype(BF16)

    for c0 in range(0, len(kinds) * LANES, chunk):
        acc = jnp.dot(u, w_ref[:, c0:c0 + chunk], preferred_element_type=F32)
        b0 = c0 // LANES
        for lo, hi, kind in _runs(kinds[b0:b0 + chunk // LANES]):
            a = acc[:, lo * LANES:hi * LANES]
            cols = slice(c0 + lo * LANES, c0 + hi * LANES)
            if kind[0] == "scale":
                a = a * kind[1]
            elif kind[0] == "rope":
                t = kind[1]
                a = _rope(a, ctab_ref[:, t * LANES:(t + 1) * LANES], stab_ref[:, t * LANES:(t + 1) * LANES])
            elif kind[0] == "normrope":
                t, g = kind[1], kind[2]
                a = _rms(a, hgain_ref[g:g + 1, :])
                a = _rope(a, ctab_ref[:, t * LANES:(t + 1) * LANES], stab_ref[:, t * LANES:(t + 1) * LANES])
            elif kind[0] == "resid":
                a = res_ref[:, cols] + gate_ref[:, cols] * a
            elif kind[0] == "vt":
                vt_ref[kind[1] * LANES:(kind[1] + 1) * LANES, :] = a.T.astype(vt_ref.dtype)
                continue
            o_ref[:, cols] = a.astype(o_ref.dtype)


def _proj(x, w, *, out_dtype, kinds, prologue="none", x_col_block=0, gain=None, mod=None,
          mod_sc=None, mod_sh=None, tabs=None, hgain=None, res=None, mod_gate=None, chunk=512):
    n = x.shape[0]
    k, n_all = w.shape
    n_vt = sum(kind[0] == "vt" for kind in kinds)
    n_out = n_all - n_vt * LANES
    assert n % TM == 0 and n_all % chunk == 0 and len(kinds) == n_all // LANES
    assert all(kind[0] == "vt" for kind in kinds[n_out // LANES:])
    args = [x]
    specs = [pl.BlockSpec((TM, k), lambda i: (i, x_col_block))]
    if prologue in ("normmod", "rms"):
        args.append(gain.reshape(1, k))
        specs.append(pl.BlockSpec((1, k), lambda i: (0, 0)))
    if prologue == "normmod":
        for comp in (mod_sc, mod_sh):
            args.append(mod)
            specs.append(pl.BlockSpec((None, 1, k), lambda i, comp=comp: (comp * MOD_ROWS + _mod_row(i), 0, 0)))
    args.append(w)
    specs.append(pl.BlockSpec((k, n_all), lambda i: (0, 0), pipeline_mode=pl.Buffered(1)))
    if tabs is not None:
        for t in tabs:
            args.append(t)
            specs.append(pl.BlockSpec((TM, t.shape[1]), lambda i: (i % TILES_PER_BATCH, 0)))
    if hgain is not None:
        args.append(hgain)
        specs.append(pl.BlockSpec(hgain.shape, lambda i: (0, 0)))
    if res is not None:
        args.append(res)
        specs.append(pl.BlockSpec((TM, n_out), lambda i: (i, 0)))
        args.append(mod)
        specs.append(pl.BlockSpec((None, 1, n_out), lambda i: (mod_gate * MOD_ROWS + _mod_row(i), 0, 0)))
    kern = functools.partial(_proj_kernel, prologue=prologue, kinds=tuple(kinds), chunk=chunk,
                             has_tab=tabs is not None, has_hgain=hgain is not None, resid=res is not None,
                             has_vt=n_vt > 0)
    out_specs = [pl.BlockSpec((TM, n_out), lambda i: (i, 0))]
    out_shape = [jax.ShapeDtypeStruct((n, n_out), out_dtype)]
    if n_vt:
        out_specs.append(pl.BlockSpec((n_vt * LANES, TM), lambda i: (0, i)))
        out_shape.append(jax.ShapeDtypeStruct((n_vt * LANES, n), out_dtype))
    out = pl.pallas_call(
        kern,
        grid=(n // TM,),
        in_specs=specs,
        out_specs=out_specs,
        out_shape=out_shape,
        compiler_params=_params(),
        name="proj",
    )(*args)
    return out if n_vt else out[0]


def _kq(k, q):
    return lax.dot_general(k, q, (((1,), (1,)), ((), ())), preferred_element_type=F32)


def _pipelined_heads(n_heads, n_chunks, q_of, k_of, vt_of, bias_of, s_ref, store):
    def scores(h, c, slot):
        s = _kq(k_of(h, c), q_of(h))
        b = bias_of(h, c)
        s_ref[slot, c] = s if b is None else s + b

    for c in range(n_chunks):
        scores(0, c, 0)
    for h in range(n_heads):
        slot = h % 2
        m = functools.reduce(jnp.maximum,
                             [jnp.max(s_ref[slot, c], axis=0, keepdims=True) for c in range(n_chunks)])
        num, den = None, None
        for c in range(n_chunks):
            if h + 1 < n_heads:
                scores(h + 1, c, 1 - slot)
            p = jnp.exp2(s_ref[slot, c] - m)
            l = jnp.sum(p, axis=0, keepdims=True)
            o = jnp.dot(vt_of(h, c), p.astype(BF16), preferred_element_type=F32)
            num = o if num is None else num + o
            den = l if den is None else den + l
        store(h, (num / den).T)


def _attn_kernel(*refs, heads, dq, has_kpe):
    if has_kpe:
        q_ref, k_ref, kpe_ref, vt_ref, o_ref, s_ref, keys_ref = refs
    else:
        q_ref, k_ref, vt_ref, o_ref, s_ref = refs
        keys_ref = None
    qt = pl.program_id(2)

    if has_kpe:
        @pl.when(qt == 0)
        def _():
            kpe = kpe_ref[...].astype(BF16)
            for h in range(heads):
                keys_ref[h, :, 0:HEAD_DIM] = k_ref[:, h * HEAD_DIM:(h + 1) * HEAD_DIM]
                keys_ref[h, :, HEAD_DIM:] = kpe

    def q_of(h):
        return q_ref[:, h * dq:(h + 1) * dq]

    def k_of(h, c):
        rows = slice(c * KEY_CHUNK, (c + 1) * KEY_CHUNK)
        return keys_ref[h, rows, :] if has_kpe else k_ref[rows, :]

    def vt_of(h, c):
        r0 = h * HEAD_DIM if has_kpe else 0
        return vt_ref[r0:r0 + HEAD_DIM, c * KEY_CHUNK:(c + 1) * KEY_CHUNK]

    def store(h, o):
        o_ref[:, h * HEAD_DIM:(h + 1) * HEAD_DIM] = o.astype(o_ref.dtype)

    @pl.when(qt == 0)
    def _():
        _pipelined_heads(heads, CTX_LEN // KEY_CHUNK, q_of, k_of, vt_of, lambda h, c: None, s_ref, store)

    @pl.when(qt > 0)
    def _():
        _pipelined_heads(heads, TOK // KEY_CHUNK, q_of, k_of, vt_of, lambda h, c: None, s_ref, store)


def _attention(q_arr, k_arr, vt_arr, *, batch, groups, heads_per_step, dq, k_blk0, kpe_arr=None, kpe_blk=0):
    shared_kv = kpe_arr is None
    kw = HEAD_DIM if shared_kv else heads_per_step * HEAD_DIM
    ow = heads_per_step * HEAD_DIM
    args = [q_arr, k_arr]
    specs = [
        pl.BlockSpec((TM, heads_per_step * dq), lambda b, g, t: (b * TILES_PER_BATCH + t, g)),
        pl.BlockSpec((TOK, kw), lambda b, g, t: (b, k_blk0 + g)),
    ]
    scratch = [pltpu.VMEM((2, TOK // KEY_CHUNK, KEY_CHUNK, TM), F32)]
    if not shared_kv:
        args.append(kpe_arr)
        specs.append(pl.BlockSpec((TOK, LANES), lambda b, g, t: (b, kpe_blk)))
        scratch.append(pltpu.VMEM((heads_per_step, TOK, HEAD_DIM + LANES), BF16))
    args.append(vt_arr)
    specs.append(pl.BlockSpec((kw, TOK), lambda b, g, t: (g, b)))
    kern = functools.partial(_attn_kernel, heads=heads_per_step, dq=dq, has_kpe=not shared_kv)
    return pl.pallas_call(
        kern,
        grid=(batch, groups, TILES_PER_BATCH),
        in_specs=specs,
        out_specs=pl.BlockSpec((TM, ow), lambda b, g, t: (b * TILES_PER_BATCH + t, g)),
        out_shape=jax.ShapeDtypeStruct((batch * TOK, groups * ow), BF16),
        scratch_shapes=scratch,
        compiler_params=_params(),
        name="attention",
    )(*args)


def _na_kernel(q_ref, k_ref, vt_ref, tbl_ref, o_ref, s_ref):
    t = pl.program_id(2)
    n_lat_tiles = GRID_ROWS // NA_Q_ROWS
    k_row0 = jnp.clip(NA_Q_ROWS * t - NA_WIN_ROWS // 2, 0, GRID_ROWS - NA_K_ROWS)
    start = pl.multiple_of(CTX_LEN + k_row0 * GRID_W, KEY_CHUNK)
    which = jnp.where(t == 0, 0, jnp.where(t == n_lat_tiles - 1, 2, 1))

    def key_rows(c):
        return pl.ds(0, KEY_CHUNK) if c == 0 else pl.ds(start + (c - 1) * KEY_CHUNK, KEY_CHUNK)

    def q_of(h):
        return q_ref[:, h * HEAD_DIM:(h + 1) * HEAD_DIM]

    def k_of(h, c):
        return k_ref[key_rows(c), h * HEAD_DIM:(h + 1) * HEAD_DIM]

    def vt_of(h, c):
        return vt_ref[h * HEAD_DIM:(h + 1) * HEAD_DIM, key_rows(c)]

    def bias_of(h, c):
        return None if c == 0 else tbl_ref[h, which, (c - 1) * KEY_CHUNK:c * KEY_CHUNK, :]

    def store(h, o):
        o_ref[:, h * HEAD_DIM:(h + 1) * HEAD_DIM] = o.astype(o_ref.dtype)

    @pl.when(t == n_lat_tiles)
    def _():
        _pipelined_heads(NA_HEADS_PER_STEP, 1, q_of, k_of, vt_of, bias_of, s_ref, store)

    @pl.when(t < n_lat_tiles)
    def _():
        _pipelined_heads(NA_HEADS_PER_STEP, 1 + NA_KEYS // KEY_CHUNK, q_of, k_of, vt_of, bias_of, s_ref, store)


def _na_bias_table(rpb):
    cols = np.arange(GRID_W)
    c0 = np.clip(cols - NA_WIN_COLS // 2, 0, GRID_W - NA_WIN_COLS)
    col_ok = (cols[:, None] >= c0[None, :]) & (cols[:, None] < c0[None, :] + NA_WIN_COLS)
    cb = np.clip(cols[:, None] - cols[None, :] + NA_WIN_COLS - 1, 0, 2 * NA_WIN_COLS - 2)
    onehot = (cb[None] == np.arange(2 * NA_WIN_COLS - 1)[:, None, None]).astype(np.float32)
    planes = jnp.einsum("hdc,ckq->hdkq", rpb.astype(F32), onehot, precision=lax.Precision.HIGHEST)
    planes = jnp.where(col_ok[None, None], planes * LOG2E, NEG_BIG)
    n_rel = 2 * NA_WIN_ROWS - 1
    planes = jnp.concatenate([planes, jnp.full((NA_HEADS, 1, GRID_W, GRID_W), NEG_BIG, F32)], axis=1)
    n_lat_tiles = GRID_ROWS // NA_Q_ROWS
    which = np.full((3, NA_K_ROWS, NA_Q_ROWS), n_rel, np.int32)
    for kind, t in enumerate((0, 1, n_lat_tiles - 1)):
        k_row0 = int(np.clip(NA_Q_ROWS * t - NA_WIN_ROWS // 2, 0, GRID_ROWS - NA_K_ROWS))
        for kk in range(NA_K_ROWS):
            for i in range(NA_Q_ROWS):
                qr, kr = NA_Q_ROWS * t + i, k_row0 + kk
                r0 = int(np.clip(qr - NA_WIN_ROWS // 2, 0, GRID_ROWS - NA_WIN_ROWS))
                if r0 <= kr < r0 + NA_WIN_ROWS:
                    which[kind, kk, i] = kr - qr + NA_WIN_ROWS - 1
    tbl = jnp.concatenate([planes[:, which[:, :, i]] for i in range(NA_Q_ROWS)], axis=-1)
    return tbl.reshape(NA_HEADS, 3, NA_KEYS, TM)


def _na_attention(qk, vt, rpb, batch):
    tbl = _na_bias_table(rpb)
    n_lat_tiles = GRID_ROWS // NA_Q_ROWS

    def q_tile(b, t):
        return b * TILES_PER_BATCH + jnp.where(t == n_lat_tiles, 0, t + 1)

    gw = NA_HEADS_PER_STEP * HEAD_DIM
    n_groups = NA_HEADS // NA_HEADS_PER_STEP
    return pl.pallas_call(
        _na_kernel,
        grid=(n_groups, batch, n_lat_tiles + 1),
        in_specs=[
            pl.BlockSpec((TM, gw), lambda g, b, t: (q_tile(b, t), g)),
            pl.BlockSpec((TOK, gw), lambda g, b, t: (b, n_groups + g)),
            pl.BlockSpec((gw, TOK), lambda g, b, t: (g, b)),
            pl.BlockSpec((NA_HEADS_PER_STEP, 3, NA_KEYS, TM), lambda g, b, t: (g, 0, 0, 0)),
        ],
        out_specs=pl.BlockSpec((TM, gw), lambda g, b, t: (q_tile(b, t), g)),
        out_shape=jax.ShapeDtypeStruct((batch * TOK, NA_HEADS * HEAD_DIM), BF16),
        scratch_shapes=[pltpu.VMEM((2, 1 + NA_KEYS // KEY_CHUNK, KEY_CHUNK, TM), F32)],
        compiler_params=_params(),
        name="na_attention",
    )(qk, qk, vt, tbl)


def _route(logits, bias):
    scores = 1.0 / (1.0 + jnp.exp(-logits))
    sel = scores + bias
    srow = [sel[e:e + 1, :] for e in range(N_EXPERTS)]
    crow = [scores[e:e + 1, :] for e in range(N_EXPERTS)]
    gscore = []
    for g in range(N_GROUPS):
        a, b, c, d = srow[GROUP_SIZE * g:GROUP_SIZE * (g + 1)]
        gscore.append(jnp.maximum(jnp.maximum(jnp.maximum(a + b, a + c), jnp.maximum(a + d, b + c)),
                                  jnp.maximum(b + d, c + d)))
    best, gidx = gscore[0], jnp.zeros(gscore[0].shape, jnp.int32)
    for g in range(1, N_GROUPS):
        better = gscore[g] > best
        gidx = jnp.where(better, g, gidx)
        best = jnp.where(better, gscore[g], best)
    s, c = [], []
    for j in range(GROUP_SIZE):
        sj, cj = srow[j], crow[j]
        for g in range(1, N_GROUPS):
            sj = jnp.where(gidx == g, srow[GROUP_SIZE * g + j], sj)
            cj = jnp.where(gidx == g, crow[GROUP_SIZE * g + j], cj)
        s.append(sj)
        c.append(cj)
    chosen = []
    for j in range(GROUP_SIZE):
        rank = jnp.zeros(s[j].shape, jnp.int32)
        for k in range(GROUP_SIZE):
            if k < j:
                rank = rank + jnp.where(s[k] >= s[j], 1, 0)
            elif k > j:
                rank = rank + jnp.where(s[k] > s[j], 1, 0)
        chosen.append(rank < 2)
    w = [jnp.where(chosen[j], c[j], 0.0) for j in range(GROUP_SIZE)]
    denom = (w[0] + w[1]) + (w[2] + w[3])
    gate = [w[j] / denom for j in range(GROUP_SIZE)]
    lo = jnp.where(chosen[0], 0, jnp.where(chosen[1], 1, 2))
    hi = jnp.where(chosen[3], 3, jnp.where(chosen[2], 2, 1))
    g_lo = jnp.where(chosen[0], gate[0], jnp.where(chosen[1], gate[1], gate[2]))
    g_hi = jnp.where(chosen[3], gate[3], jnp.where(chosen[2], gate[2], gate[1]))
    pair = jnp.where(lo == 0, hi - 1, jnp.where(lo == 1, hi + 1, N_PAIRS - 1))
    return gidx * N_PAIRS + pair, g_lo, g_hi


def _slab_rows(c):
    return pl.ds(c, TM, stride=SLAB_PITCH)


def _store_slabs(ref, val, extra):
    for c in range(SLAB_ROWS):
        ref[_slab_rows(c), :] = val[:, c * LANES:(c + 1) * LANES]
    ref[_slab_rows(SLAB_ROWS), :] = extra


def _ffn_prep_kernel(h_ref, gain_ref, sc_ref, sh_ref, rw_ref, rb_ref, v_ref, part_ref):
    v = _rms(h_ref[...], gain_ref[...]) * (1.0 + sc_ref[...]) + sh_ref[...]
    logits = lax.dot_general(rw_ref[...], v, (((1,), (1,)), ((), ())),
                             precision=lax.Precision.HIGHEST, preferred_element_type=F32)
    part, g_lo, g_hi = _route(logits, rb_ref[...])
    part_ref[...] = jnp.zeros(part_ref.shape, jnp.int32)
    part_ref[0:1, :] = part
    row = lax.broadcasted_iota(jnp.int32, (LANES, TM), 0)
    gates_t = jnp.where(row == 0, g_lo, jnp.where(row == 1, g_hi, 0.0))
    _store_slabs(v_ref, v, gates_t.T)


def _ffn_prep(h, gain, mod, router_wt, router_bias):
    n = h.shape[0]
    return pl.pallas_call(
        _ffn_prep_kernel,
        grid=(n // TM,),
        in_specs=[
            pl.BlockSpec((TM, D_MODEL), lambda i: (i, 0)),
            pl.BlockSpec((1, D_MODEL), lambda i: (0, 0)),
            pl.BlockSpec((None, 1, D_MODEL), lambda i: (4 * MOD_ROWS + _mod_row(i), 0, 0)),
            pl.BlockSpec((None, 1, D_MODEL), lambda i: (3 * MOD_ROWS + _mod_row(i), 0, 0)),
            pl.BlockSpec((N_EXPERTS, D_MODEL), lambda i: (0, 0)),
            pl.BlockSpec((N_EXPERTS, 1), lambda i: (0, 0)),
        ],
        out_specs=[
            pl.BlockSpec((TM * SLAB_PITCH, LANES), lambda i: (i, 0)),
            pl.BlockSpec((8, TM), lambda i: (0, i)),
        ],
        out_shape=[
            jax.ShapeDtypeStruct((n * SLAB_PITCH, LANES), F32),
            jax.ShapeDtypeStruct((8, n), jnp.int32),
        ],
        compiler_params=_params(),
        name="ffn_prep",
    )(h, gain.reshape(1, D_MODEL), mod, mod, router_wt, router_bias.reshape(N_EXPERTS, 1))


def _dispatch_meta(part, n_slots):
    n = part.shape[0]
    onehot = (part[:, None] == jnp.arange(N_PARTS, dtype=jnp.int32)[None, :]).astype(jnp.int32)
    csum = jnp.cumsum(onehot, axis=0)
    rank = jnp.take_along_axis(csum, part[:, None], axis=1)[:, 0] - 1
    counts = csum[-1]
    padded = ((counts + TM - 1) // TM) * TM
    ends = jnp.cumsum(padded)
    starts = ends - padded
    pos = (starts[part] + rank).astype(jnp.int32)
    src = jnp.zeros((n_slots,), jnp.int32).at[pos].set(jnp.arange(n, dtype=jnp.int32))
    tile_start = jnp.arange(n_slots // TM, dtype=jnp.int32) * TM
    last_start = jnp.minimum(tile_start, ends[-1] - TM)
    tile_part = jnp.sum((ends[None, :] <= last_start[:, None]).astype(jnp.int32), axis=1)
    n_rows = jnp.clip(counts[tile_part] - (tile_start - starts[tile_part]), 0, TM).astype(jnp.int32)
    group, pair = tile_part // N_PAIRS, tile_part % N_PAIRS
    e_lo = group * GROUP_SIZE + jnp.asarray(PAIR_LO, jnp.int32)[pair]
    e_hi = group * GROUP_SIZE + jnp.asarray(PAIR_HI, jnp.int32)[pair]
    return src, e_lo, e_hi, n_rows


def _expert_kernel(src_ref, elo_ref, ehi_ref, nrows_ref, v_hbm, w1a, w3a, w2a, w1b, w3b, w2b, y_hbm,
                   xbuf, ybuf, x2d, in_sem, out_sem, *, n_tok):
    t = pl.program_id(0)
    last = pl.num_programs(0) - 1
    slot = t % 2

    def n_groups(tile):
        return (nrows_ref[tile] + ROW_GROUP - 1) // ROW_GROUP

    def in_copy(slot_, r, tok):
        return pltpu.make_async_copy(v_hbm.at[pl.ds(tok * SLAB_PITCH, SLAB_PITCH), :],
                                     xbuf.at[slot_, pl.ds(r * SLAB_PITCH, SLAB_PITCH), :], in_sem.at[slot_])

    def out_copy(slot_, r, tok):
        return pltpu.make_async_copy(ybuf.at[slot_, pl.ds(r * SLAB_PITCH, SLAB_PITCH), :],
                                     y_hbm.at[pl.ds(tok * SLAB_PITCH, SLAB_PITCH), :], out_sem.at[slot_])

    def for_rows(tile, fn):
        def body(g, carry):
            for u in range(ROW_GROUP):
                fn(g * ROW_GROUP + u, u)
            return carry
        lax.fori_loop(0, n_groups(tile), body, 0)

    def start_in(tile, slot_):
        for_rows(tile, lambda r, u: in_copy(slot_, r, src_ref[tile * TM + r]).start(priority=u % 2))

    def start_out(tile, slot_):
        n = nrows_ref[tile]

        def one(r, u):
            tok = jnp.where(r < n, src_ref[tile * TM + r], n_tok + slot_ * ROW_GROUP + u)
            out_copy(slot_, r, tok).start(priority=u % 2)
        for_rows(tile, one)

    @pl.when(t == 0)
    def _():
        xbuf[...] = jnp.zeros(xbuf.shape, xbuf.dtype)
        spare = 2 * ROW_GROUP * SLAB_PITCH
        fill = pltpu.make_async_copy(xbuf.at[1, pl.ds(0, spare), :],
                                     y_hbm.at[pl.ds(n_tok * SLAB_PITCH, spare), :], out_sem.at[0])
        fill.start()
        fill.wait()
        start_in(0, 0)

    @pl.when(jnp.logical_and(t < last, nrows_ref[jnp.minimum(t + 1, last)] > 0))
    def _():
        start_in(t + 1, 1 - slot)

    @pl.when(nrows_ref[t] > 0)
    def _():
        for_rows(t, lambda r, u: in_copy(slot, r, 0).wait())
        xs = xbuf.at[slot]
        for c in range(SLAB_ROWS):
            x2d[:, c * LANES:(c + 1) * LANES] = xs[_slab_rows(c), :].astype(BF16)
        g = xs[_slab_rows(SLAB_ROWS), :]
        x = x2d[...]

        def hidden(w1, w3, gate):
            a = jnp.dot(x, w1[...], preferred_element_type=F32)
            b = jnp.dot(x, w3[...], preferred_element_type=F32)
            return (_silu(a) * b * gate).astype(BF16)

        y = jnp.dot(hidden(w1a, w3a, g[:, 0:1]), w2a[...], preferred_element_type=F32)
        y = y + jnp.dot(hidden(w1b, w3b, g[:, 1:2]), w2b[...], preferred_element_type=F32)
        _store_slabs(ybuf.at[slot], y, jnp.zeros((TM, LANES), F32))
        start_out(t, slot)

    @pl.when(jnp.logical_and(t > 0, nrows_ref[jnp.maximum(t - 1, 0)] > 0))
    def _():
        for_rows(t - 1, lambda r, u: out_copy(1 - slot, r, 0).wait())

    @pl.when(jnp.logical_and(t == last, nrows_ref[t] > 0))
    def _():
        for_rows(t, lambda r, u: out_copy(slot, r, 0).wait())


def _experts(v_slabs, src, e_lo, e_hi, n_rows, w1, w3, w2, layer, n_tok):
    n_tiles = n_rows.shape[0]
    up = lambda sel: pl.BlockSpec((None, None, D_MODEL, D_EXPERT),
                                  lambda t, s, lo, hi, nr: (layer, (lo, hi)[sel][t], 0, 0))
    down = lambda sel: pl.BlockSpec((None, None, D_EXPERT, D_MODEL),
                                    lambda t, s, lo, hi, nr: (layer, (lo, hi)[sel][t], 0, 0))
    buf = pltpu.VMEM((2, TM * SLAB_PITCH, LANES), F32)
    return pl.pallas_call(
        functools.partial(_expert_kernel, n_tok=n_tok),
        grid_spec=pltpu.PrefetchScalarGridSpec(
            num_scalar_prefetch=4,
            grid=(n_tiles,),
            in_specs=[pl.BlockSpec(memory_space=pl.ANY), up(0), up(0), down(0), up(1), up(1), down(1)],
            out_specs=pl.BlockSpec(memory_space=pl.ANY),
            scratch_shapes=[buf, buf, pltpu.VMEM((TM, D_MODEL), BF16),
                            pltpu.SemaphoreType.DMA((2,)), pltpu.SemaphoreType.DMA((2,))],
        ),
        out_shape=jax.ShapeDtypeStruct(((n_tok + 2 * ROW_GROUP) * SLAB_PITCH, LANES), F32),
        compiler_params=_params(),
        name="experts",
    )(src, e_lo, e_hi, n_rows, v_slabs, w1, w3, w2, w1, w3, w2)


def _combine_kernel(y_ref, res_ref, gate_ref, o_ref):
    for c in range(SLAB_ROWS):
        cols = slice(c * LANES, (c + 1) * LANES)
        o_ref[:, cols] = res_ref[:, cols] + gate_ref[:, cols] * y_ref[_slab_rows(c), :]


def _combine(y_slabs, h, mod, mod_gate):
    n = h.shape[0]
    return pl.pallas_call(
        _combine_kernel,
        grid=(n // TM,),
        in_specs=[
            pl.BlockSpec((TM * SLAB_PITCH, LANES), lambda i: (i, 0)),
            pl.BlockSpec((TM, D_MODEL), lambda i: (i, 0)),
            pl.BlockSpec((None, 1, D_MODEL), lambda i: (mod_gate * MOD_ROWS + _mod_row(i), 0, 0)),
        ],
        out_specs=pl.BlockSpec((TM, D_MODEL), lambda i: (i, 0)),
        out_shape=jax.ShapeDtypeStruct((n, D_MODEL), F32),
        compiler_params=_params(),
        name="combine",
    )(y_slabs, h, mod)


def _moe(h, gain, mod, router_wt, router_bias, w1, w3, w2, layer):
    n = h.shape[0]
    n_slots = n + N_PARTS * TM
    v_slabs, part = _ffn_prep(h, gain, mod, router_wt, router_bias)
    src, e_lo, e_hi, n_rows = _dispatch_meta(part[0], n_slots)
    y_slabs = _experts(v_slabs, src, e_lo, e_hi, n_rows, w1, w3, w2, layer, n)
    return _combine(y_slabs, h, mod, 5)


def _final_kernel(h_ref, gain_ref, o_ref):
    o_ref[...] = _rms(h_ref[...], gain_ref[...])


def _final_norm(h, gain, batch):
    lat_tiles = SEQ // TM
    return pl.pallas_call(
        _final_kernel,
        grid=(batch, lat_tiles),
        in_specs=[
            pl.BlockSpec((TM, D_MODEL), lambda b, t: (b * TILES_PER_BATCH + 1 + t, 0)),
            pl.BlockSpec((1, D_MODEL), lambda b, t: (0, 0)),
        ],
        out_specs=pl.BlockSpec((TM, D_MODEL), lambda b, t: (b * lat_tiles + t, 0)),
        out_shape=jax.ShapeDtypeStruct((batch * SEQ, D_MODEL), F32),
        compiler_params=_params(),
        name="final_norm",
    )(h, gain.reshape(1, D_MODEL))


def _rope_tables(rot_dim, layout):
    n_freq = rot_dim // 4
    inv_freq = 1.0 / (ROPE_THETA ** (jnp.arange(n_freq, dtype=F32) / n_freq))
    t = jnp.arange(SEQ, dtype=jnp.int32)
    row = (t // GRID_W).astype(F32)
    col = (t % GRID_W).astype(F32)
    ang = jnp.concatenate([row[:, None] * inv_freq, col[:, None] * inv_freq], axis=-1)
    cos, sin = jnp.cos(ang), jnp.sin(ang)
    if layout == "padded":
        pad1, pad0 = jnp.ones_like(cos), jnp.zeros_like(sin)
        c = jnp.concatenate([cos, pad1, cos, pad1], axis=-1)
        s = jnp.concatenate([-sin, pad0, sin, pad0], axis=-1)
    else:
        c = jnp.concatenate([cos, cos], axis=-1)
        s = jnp.concatenate([-sin, sin], axis=-1)
    c = jnp.concatenate([jnp.ones((CTX_LEN, LANES), F32), c], axis=0)
    s = jnp.concatenate([jnp.zeros((CTX_LEN, LANES), F32), s], axis=0)
    return c, s


def _pad_rope_cols(w):
    half = MLA_ROPE // 2
    z = jnp.zeros(w.shape[:-1] + (half,), w.dtype)
    return jnp.concatenate([w[..., :half], z, w[..., half:], z], axis=-1)


def _gqa_layer(h, mod, norm_gain, w_qkv, q_gain, k_gain, w_o, tabs, batch):
    kinds = ([("normrope", 0, 0)] * GQA_HEADS + [("normrope", 1, 1)] * GQA_KV_HEADS
             + [("vt", j) for j in range(GQA_KV_HEADS)])
    qk, vt = _proj(h, w_qkv.astype(BF16), out_dtype=BF16, kinds=kinds, prologue="normmod", gain=norm_gain,
                   mod=mod, mod_sc=1, mod_sh=0, tabs=tabs, hgain=jnp.stack([q_gain, k_gain]))
    o = _attention(qk, qk, vt, batch=batch, groups=GQA_KV_HEADS, heads_per_step=GQA_GROUP, dq=HEAD_DIM,
                   k_blk0=GQA_HEADS)
    return _proj(o, w_o.astype(BF16), out_dtype=F32, kinds=[("resid",)] * (D_MODEL // LANES),
                 res=h, mod=mod, mod_gate=2)


def _mla_layer(h, mod, norm_gain, w_a, q_gain, kv_gain, w_uq, w_ukv, w_o, tabs, batch):
    n_lat = 2 * MLA_RANK // LANES
    w_a_p = jnp.concatenate([w_a[:, :2 * MLA_RANK], _pad_rope_cols(w_a[:, 2 * MLA_RANK:])], axis=1)
    a = _proj(h, w_a_p.astype(BF16), out_dtype=F32, kinds=[("plain",)] * n_lat + [("rope", 1)],
              prologue="normmod", gain=norm_gain, mod=mod, mod_sc=1, mod_sh=0, tabs=tabs, chunk=LANES * (n_lat + 1))
    scale = MLA_QK ** -0.5 * LOG2E
    w_uq_h = w_uq.reshape(MLA_RANK, MLA_HEADS, MLA_QK)
    w_uq_p = jnp.concatenate([w_uq_h[..., :MLA_NOPE], _pad_rope_cols(w_uq_h[..., MLA_NOPE:])], axis=-1)
    q = _proj(a, w_uq_p.reshape(MLA_RANK, MLA_HEADS * 2 * LANES).astype(BF16), out_dtype=BF16,
              kinds=[("scale", scale), ("rope", 0)] * MLA_HEADS, prologue="rms", x_col_block=0, gain=q_gain, tabs=tabs)
    w_ukv_h = w_ukv.reshape(MLA_RANK, MLA_HEADS, 2 * HEAD_DIM)
    w_ukv_p = jnp.concatenate([w_ukv_h[..., :MLA_NOPE].reshape(MLA_RANK, -1),
                               w_ukv_h[..., MLA_NOPE:].reshape(MLA_RANK, -1)], axis=1)
    k_nope, vt = _proj(a, w_ukv_p.astype(BF16), out_dtype=BF16,
                       kinds=[("plain",)] * MLA_HEADS + [("vt", j) for j in range(MLA_HEADS)],
                       prologue="rms", x_col_block=1, gain=kv_gain)
    o = _attention(q, k_nope, vt, batch=batch, groups=MLA_HEADS // MLA_HEADS_PER_STEP,
                   heads_per_step=MLA_HEADS_PER_STEP, dq=2 * LANES, k_blk0=0, kpe_arr=a, kpe_blk=n_lat)
    return _proj(o, w_o.astype(BF16), out_dtype=F32, kinds=[("resid",)] * (D_MODEL // LANES),
                 res=h, mod=mod, mod_gate=2)


def _na_layer(h, mod, norm_gain, w_qkv, rpb, w_o, batch):
    scale = HEAD_DIM ** -0.5 * LOG2E
    kinds = [("scale", scale)] * NA_HEADS + [("plain",)] * NA_HEADS + [("vt", j) for j in range(NA_HEADS)]
    qk, vt = _proj(h, w_qkv.astype(BF16), out_dtype=BF16, kinds=kinds, prologue="normmod", gain=norm_gain,
                   mod=mod, mod_sc=1, mod_sh=0)
    o = _na_attention(qk, vt, rpb, batch)
    return _proj(o, w_o.astype(BF16), out_dtype=F32, kinds=[("resid",)] * (D_MODEL // LANES),
                 res=h, mod=mod, mod_gate=2)


def kernel(x, c, ctx, c_ctx, mod_w, mod_b, norm_mix, norm_ffn, gqa_wqkv, gqa_q_gain, gqa_k_gain, gqa_wo,
           mla_wa, mla_q_gain, mla_kv_gain, mla_wuq, mla_wukv, mla_wo, na_wqkv, na_rpb, na_wo,
           router_w, router_bias, moe_w1, moe_w3, moe_w2, final_gain):
    batch = x.shape[0]
    assert batch <= CTX_MOD_ROW and x.shape[1:] == (SEQ, D_MODEL) and ctx.shape[1:] == (CTX_LEN, D_MODEL)
    h = jnp.concatenate([ctx, x], axis=1).reshape(batch * TOK, D_MODEL)
    cond = jnp.zeros((MOD_ROWS, D_MODEL), F32).at[:batch].set(c).at[CTX_MOD_ROW].set(c_ctx)
    mods = _modulation(cond, mod_w, mod_b)
    mods = mods.reshape(DEPTH, MOD_ROWS, 6, D_MODEL).transpose(0, 2, 1, 3).reshape(DEPTH, 6 * MOD_ROWS, 1, D_MODEL)

    c_full, s_full = _rope_tables(HEAD_DIM, "full")
    gqa_scale = HEAD_DIM ** -0.5 * LOG2E
    gqa_tabs = (jnp.concatenate([c_full * gqa_scale, c_full], axis=1),
                jnp.concatenate([s_full * gqa_scale, s_full], axis=1))
    c_pad, s_pad = _rope_tables(MLA_ROPE, "padded")
    mla_scale = MLA_QK ** -0.5 * LOG2E
    mla_tabs = (jnp.concatenate([c_pad * mla_scale, c_pad], axis=1),
                jnp.concatenate([s_pad * mla_scale, s_pad], axis=1))
    router_wt = router_w.T
    w1_b, w3_b, w2_b = moe_w1.astype(BF16), moe_w3.astype(BF16), moe_w2.astype(BF16)

    for i in range(DEPTH):
        mod = mods[i]
        kind, j = i % N_MIXERS, i // N_MIXERS
        if kind == 0:
            h = _gqa_layer(h, mod, norm_mix[i], gqa_wqkv[j], gqa_q_gain[j], gqa_k_gain[j], gqa_wo[j],
                           gqa_tabs, batch)
        elif kind == 1:
            h = _mla_layer(h, mod, norm_mix[i], mla_wa[j], mla_q_gain[j], mla_kv_gain[j], mla_wuq[j],
                           mla_wukv[j], mla_wo[j], mla_tabs, batch)
        else:
            h = _na_layer(h, mod, norm_mix[i], na_wqkv[j], na_rpb[j], na_wo[j], batch)
        h = _moe(h, norm_ffn[i], mod, router_wt, router_bias, w1_b, w3_b, w2_b, i)
    return _final_norm(h, final_gain, batch).reshape(batch, SEQ, D_MODEL)
```

```python
import functools

import numpy as np
import jax
import jax.numpy as jnp
from jax import lax
from jax.experimental import pallas as pl
from jax.experimental.pallas import tpu as pltpu

F32 = jnp.float32
BF16 = jnp.bfloat16

D_MODEL = 2048
SEQ = 2048
GRID_W = 64
GRID_ROWS = SEQ // GRID_W
CTX_LEN = 256
TOK = CTX_LEN + SEQ
DEPTH = 4
N_MIXERS = 3
EPS = 1e-6
ROPE_THETA = 10000.0

GQA_HEADS = 16
GQA_KV_HEADS = 4
HEAD_DIM = 128
GQA_GROUP = GQA_HEADS // GQA_KV_HEADS

MLA_HEADS = 16
MLA_RANK = 512
MLA_NOPE = 128
MLA_ROPE = 64
MLA_QK = MLA_NOPE + MLA_ROPE

NA_HEADS = 16
NA_WIN_ROWS = 8
NA_WIN_COLS = 16
NA_Q_ROWS = 4
NA_K_ROWS = 12
NA_KEYS = NA_K_ROWS * GRID_W
NA_HEADS_PER_STEP = 4
MLA_HEADS_PER_STEP = 4
KEY_CHUNK = 256

N_EXPERTS = 16
N_GROUPS = 4
GROUP_SIZE = N_EXPERTS // N_GROUPS
N_PAIRS = 6
N_PARTS = N_GROUPS * N_PAIRS
D_EXPERT = 512
PAIR_LO = (0, 0, 0, 1, 1, 2)
PAIR_HI = (1, 2, 3, 2, 3, 3)

LANES = 128
TM = 256
TILES_PER_BATCH = TOK // TM
SLAB_ROWS = D_MODEL // LANES
SLAB_PITCH = SLAB_ROWS + 1
ROW_GROUP = 8
MOD_ROWS = 16
CTX_MOD_ROW = 8
NEG_BIG = -1e30
LOG2E = 1.4426950408889634
VMEM_LIMIT = 56 * 1024 * 1024


def _params():
    return pltpu.CompilerParams(vmem_limit_bytes=VMEM_LIMIT)


def _mod_row(i):
    return jnp.where(i % TILES_PER_BATCH == 0, CTX_MOD_ROW, i // TILES_PER_BATCH)


def _silu(x):
    return x / (1.0 + jnp.exp(-x))


def _mod_kernel(cond_ref, w_ref, b_ref, o_ref):
    s = _silu(cond_ref[...]).astype(BF16)
    o_ref[...] = jnp.dot(s, w_ref[...].astype(BF16), preferred_element_type=F32) + b_ref[...]


def _modulation(cond, mod_w, mod_b):
    tn = 1024
    n_out = mod_w.shape[-1]
    return pl.pallas_call(
        _mod_kernel,
        grid=(DEPTH, n_out // tn),
        in_specs=[
            pl.BlockSpec((MOD_ROWS, D_MODEL), lambda l, j: (0, 0)),
            pl.BlockSpec((None, D_MODEL, tn), lambda l, j: (l, 0, j)),
            pl.BlockSpec((None, 1, tn), lambda l, j: (l, 0, j)),
        ],
        out_specs=pl.BlockSpec((None, MOD_ROWS, tn), lambda l, j: (l, 0, j)),
        out_shape=jax.ShapeDtypeStruct((DEPTH, MOD_ROWS, n_out), F32),
        compiler_params=_params(),
        name="modulation",
    )(cond, mod_w, mod_b.reshape(DEPTH, 1, n_out))


def _rms(x, gain):
    return x * lax.rsqrt(jnp.mean(x * x, axis=-1, keepdims=True) + EPS) * gain


def _rope(x, c, s):
    return x * c + pltpu.roll(x, LANES // 2, 1) * s


def _runs(kinds):
    out, start = [], 0
    for k in range(1, len(kinds) + 1):
        if k == len(kinds) or kinds[k] != kinds[start] or kinds[start][0] in ("rope", "normrope", "vt"):
            out.append((start, k, kinds[start]))
            start = k
    return out


def _proj_kernel(*refs, prologue, kinds, chunk, has_tab, has_hgain, resid, has_vt, comb, ffn):
    it = iter(refs)
    if comb:
        y_ref, hprev_ref, cgate_ref = next(it), next(it), next(it)
    else:
        x_ref = next(it)
    gain_ref = sc_ref = sh_ref = None
    if prologue == "normmod":
        gain_ref, sc_ref, sh_ref = next(it), next(it), next(it)
    elif prologue == "rms":
        gain_ref = next(it)
    w_ref = next(it)
    ctab_ref = stab_ref = hgain_ref = res_ref = gate_ref = None
    if has_tab:
        ctab_ref, stab_ref = next(it), next(it)
    if has_hgain:
        hgain_ref = next(it)
    if resid:
        res_ref, gate_ref = next(it), next(it)
    if ffn:
        fgain_ref, fsc_ref, fsh_ref, rw_ref, rb_ref = (next(it) for _ in range(5))
    o_ref = next(it)
    vt_ref = next(it) if has_vt else None
    hnew_ref = next(it) if comb else None
    if ffn:
        v_ref, part_ref = next(it), next(it)

    if comb:
        for c in range(SLAB_ROWS):
            cols = slice(c * LANES, (c + 1) * LANES)
            hnew_ref[:, cols] = hprev_ref[:, cols] + cgate_ref[:, cols] * y_ref[_slab_rows(c), :]
        x_ref = hnew_ref

    if prologue == "none":
        u = x_ref[...]
    else:
        y = _rms(x_ref[...], gain_ref[...])
        if prologue == "normmod":
            y = y * (1.0 + sc_ref[...]) + sh_ref[...]
        u = y.astype(BF16)

    for c0 in range(0, len(kinds) * LANES, chunk):
        acc = jnp.dot(u, w_ref[:, c0:c0 + chunk], preferred_element_type=F32)
        b0 = c0 // LANES
        for lo, hi, kind in _runs(kinds[b0:b0 + chunk // LANES]):
            a = acc[:, lo * LANES:hi * LANES]
            cols = slice(c0 + lo * LANES, c0 + hi * LANES)
            if kind[0] == "scale":
                a = a * kind[1]
            elif kind[0] == "rope":
                t = kind[1]
                a = _rope(a, ctab_ref[:, t * LANES:(t + 1) * LANES], stab_ref[:, t * LANES:(t + 1) * LANES])
            elif kind[0] == "normrope":
                t, g = kind[1], kind[2]
                a = _rms(a, hgain_ref[g:g + 1, :])
                a = _rope(a, ctab_ref[:, t * LANES:(t + 1) * LANES], stab_ref[:, t * LANES:(t + 1) * LANES])
            elif kind[0] == "resid":
                a = res_ref[:, cols] + gate_ref[:, cols] * a
            elif kind[0] == "vt":
                vt_ref[kind[1] * LANES:(kind[1] + 1) * LANES, :] = a.T.astype(vt_ref.dtype)
                continue
            o_ref[:, cols] = a.astype(o_ref.dtype)

    if ffn:
        _ffn_prep_tile(o_ref[...], fgain_ref, fsc_ref, fsh_ref, rw_ref, rb_ref, v_ref, part_ref)


def _proj(x, w, *, out_dtype, kinds, prologue="none", x_col_block=0, gain=None, mod=None,
          mod_sc=None, mod_sh=None, tabs=None, hgain=None, res=None, mod_gate=None, chunk=512,
          comb=None, ffn=None):
    k, n_all = w.shape
    n = comb[1].shape[0] if comb else x.shape[0]
    n_vt = sum(kind[0] == "vt" for kind in kinds)
    n_out = n_all - n_vt * LANES
    assert n % TM == 0 and n_all % chunk == 0 and len(kinds) == n_all // LANES
    assert all(kind[0] == "vt" for kind in kinds[n_out // LANES:])
    if comb:
        y_slabs, h_prev, mod_prev = comb
        args = [y_slabs, h_prev, mod_prev]
        specs = [pl.BlockSpec((TM * SLAB_PITCH, LANES), lambda i: (i, 0)),
                 pl.BlockSpec((TM, k), lambda i: (i, 0)),
                 pl.BlockSpec((None, 1, k), lambda i: (5 * MOD_ROWS + _mod_row(i), 0, 0))]
    else:
        args = [x]
        specs = [pl.BlockSpec((TM, k), lambda i: (i, x_col_block))]
    if prologue in ("normmod", "rms"):
        args.append(gain.reshape(1, k))
        specs.append(pl.BlockSpec((1, k), lambda i: (0, 0)))
    if prologue == "normmod":
        for comp in (mod_sc, mod_sh):
            args.append(mod)
            specs.append(pl.BlockSpec((None, 1, k), lambda i, comp=comp: (comp * MOD_ROWS + _mod_row(i), 0, 0)))
    args.append(w)
    specs.append(pl.BlockSpec((k, n_all), lambda i: (0, 0), pipeline_mode=pl.Buffered(1)))
    if tabs is not None:
        for t in tabs:
            args.append(t)
            specs.append(pl.BlockSpec((TM, t.shape[1]), lambda i: (i % TILES_PER_BATCH, 0)))
    if hgain is not None:
        args.append(hgain)
        specs.append(pl.BlockSpec(hgain.shape, lambda i: (0, 0)))
    if res is not None:
        args.append(res)
        specs.append(pl.BlockSpec((TM, n_out), lambda i: (i, 0)))
        args.append(mod)
        specs.append(pl.BlockSpec((None, 1, n_out), lambda i: (mod_gate * MOD_ROWS + _mod_row(i), 0, 0)))
    if ffn:
        f_gain, router_wt, router_bias = ffn
        assert res is not None and n_out == D_MODEL
        args += [f_gain.reshape(1, D_MODEL), mod, mod, router_wt, router_bias.reshape(N_EXPERTS, 1)]
        specs += [
            pl.BlockSpec((1, D_MODEL), lambda i: (0, 0)),
            pl.BlockSpec((None, 1, D_MODEL), lambda i: (4 * MOD_ROWS + _mod_row(i), 0, 0)),
            pl.BlockSpec((None, 1, D_MODEL), lambda i: (3 * MOD_ROWS + _mod_row(i), 0, 0)),
            pl.BlockSpec((N_EXPERTS, D_MODEL), lambda i: (0, 0)),
            pl.BlockSpec((N_EXPERTS, 1), lambda i: (0, 0)),
        ]
    kern = functools.partial(_proj_kernel, prologue=prologue, kinds=tuple(kinds), chunk=chunk,
                             has_tab=tabs is not None, has_hgain=hgain is not None, resid=res is not None,
                             has_vt=n_vt > 0, comb=comb is not None, ffn=ffn is not None)
    out_specs = [pl.BlockSpec((TM, n_out), lambda i: (i, 0))]
    out_shape = [jax.ShapeDtypeStruct((n, n_out), out_dtype)]
    if n_vt:
        out_specs.append(pl.BlockSpec((n_vt * LANES, TM), lambda i: (0, i)))
        out_shape.append(jax.ShapeDtypeStruct((n_vt * LANES, n), out_dtype))
    if comb:
        out_specs.append(pl.BlockSpec((TM, k), lambda i: (i, 0)))
        out_shape.append(jax.ShapeDtypeStruct((n, k), F32))
    if ffn:
        out_specs += [pl.BlockSpec((TM * SLAB_PITCH, LANES), lambda i: (i, 0)), pl.BlockSpec((8, TM), lambda i: (0, i))]
        out_shape += [jax.ShapeDtypeStruct((n * SLAB_PITCH, LANES), F32), jax.ShapeDtypeStruct((8, n), jnp.int32)]
    out = pl.pallas_call(
        kern,
        grid=(n // TM,),
        in_specs=specs,
        out_specs=out_specs,
        out_shape=out_shape,
        compiler_params=_params(),
        name="proj",
    )(*args)
    return out if len(out) > 1 else out[0]


def _kq(k, q):
    return lax.dot_general(k, q, (((1,), (1,)), ((), ())), preferred_element_type=F32)


def _pipelined_heads(n_heads, n_chunks, q_of, k_of, vt_of, bias_of, s_ref, store):
    def scores(h, c, slot):
        s = _kq(k_of(h, c), q_of(h))
        b = bias_of(h, c)
        s_ref[slot, c] = s if b is None else s + b

    for c in range(n_chunks):
        scores(0, c, 0)
    for h in range(n_heads):
        slot = h % 2
        m = functools.reduce(jnp.maximum,
                             [jnp.max(s_ref[slot, c], axis=0, keepdims=True) for c in range(n_chunks)])
        num, den = None, None
        for c in range(n_chunks):
            if h + 1 < n_heads:
                scores(h + 1, c, 1 - slot)
            p = jnp.exp2(s_ref[slot, c] - m)
            l = jnp.sum(p, axis=0, keepdims=True)
            o = jnp.dot(vt_of(h, c), p.astype(BF16), preferred_element_type=F32)
            num = o if num is None else num + o
            den = l if den is None else den + l
        store(h, (num / den).T)


def _attn_kernel(*refs, heads, dq, has_kpe):
    if has_kpe:
        q_ref, k_ref, kpe_ref, vt_ref, o_ref, s_ref, keys_ref = refs
    else:
        q_ref, k_ref, vt_ref, o_ref, s_ref = refs
        keys_ref = None
    qt = pl.program_id(2)

    if has_kpe:
        @pl.when(qt == 0)
        def _():
            kpe = kpe_ref[...].astype(BF16)
            for h in range(heads):
                keys_ref[h, :, 0:HEAD_DIM] = k_ref[:, h * HEAD_DIM:(h + 1) * HEAD_DIM]
                keys_ref[h, :, HEAD_DIM:] = kpe

    def q_of(h):
        return q_ref[:, h * dq:(h + 1) * dq]

    def k_of(h, c):
        rows = slice(c * KEY_CHUNK, (c + 1) * KEY_CHUNK)
        return keys_ref[h, rows, :] if has_kpe else k_ref[rows, :]

    def vt_of(h, c):
        r0 = h * HEAD_DIM if has_kpe else 0
        return vt_ref[r0:r0 + HEAD_DIM, c * KEY_CHUNK:(c + 1) * KEY_CHUNK]

    def store(h, o):
        o_ref[:, h * HEAD_DIM:(h + 1) * HEAD_DIM] = o.astype(o_ref.dtype)

    @pl.when(qt == 0)
    def _():
        _pipelined_heads(heads, CTX_LEN // KEY_CHUNK, q_of, k_of, vt_of, lambda h, c: None, s_ref, store)

    @pl.when(qt > 0)
    def _():
        _pipelined_heads(heads, TOK // KEY_CHUNK, q_of, k_of, vt_of, lambda h, c: None, s_ref, store)


def _attention(q_arr, k_arr, vt_arr, *, batch, groups, heads_per_step, dq, k_blk0, kpe_arr=None, kpe_blk=0):
    shared_kv = kpe_arr is None
    kw = HEAD_DIM if shared_kv else heads_per_step * HEAD_DIM
    ow = heads_per_step * HEAD_DIM
    args = [q_arr, k_arr]
    specs = [
        pl.BlockSpec((TM, heads_per_step * dq), lambda b, g, t: (b * TILES_PER_BATCH + t, g)),
        pl.BlockSpec((TOK, kw), lambda b, g, t: (b, k_blk0 + g)),
    ]
    scratch = [pltpu.VMEM((2, TOK // KEY_CHUNK, KEY_CHUNK, TM), F32)]
    if not shared_kv:
        args.append(kpe_arr)
        specs.append(pl.BlockSpec((TOK, LANES), lambda b, g, t: (b, kpe_blk)))
        scratch.append(pltpu.VMEM((heads_per_step, TOK, HEAD_DIM + LANES), BF16))
    args.append(vt_arr)
    specs.append(pl.BlockSpec((kw, TOK), lambda b, g, t: (g, b)))
    kern = functools.partial(_attn_kernel, heads=heads_per_step, dq=dq, has_kpe=not shared_kv)
    return pl.pallas_call(
        kern,
        grid=(batch, groups, TILES_PER_BATCH),
        in_specs=specs,
        out_specs=pl.BlockSpec((TM, ow), lambda b, g, t: (b * TILES_PER_BATCH + t, g)),
        out_shape=jax.ShapeDtypeStruct((batch * TOK, groups * ow), BF16),
        scratch_shapes=scratch,
        compiler_params=_params(),
        name="attention",
    )(*args)


def _na_kernel(q_ref, k_ref, vt_ref, tbl_ref, o_ref, s_ref):
    t = pl.program_id(2)
    n_lat_tiles = GRID_ROWS // NA_Q_ROWS
    k_row0 = jnp.clip(NA_Q_ROWS * t - NA_WIN_ROWS // 2, 0, GRID_ROWS - NA_K_ROWS)
    start = pl.multiple_of(CTX_LEN + k_row0 * GRID_W, KEY_CHUNK)
    which = jnp.where(t == 0, 0, jnp.where(t == n_lat_tiles - 1, 2, 1))

    def key_rows(c):
        return pl.ds(0, KEY_CHUNK) if c == 0 else pl.ds(start + (c - 1) * KEY_CHUNK, KEY_CHUNK)

    def q_of(h):
        return q_ref[:, h * HEAD_DIM:(h + 1) * HEAD_DIM]

    def k_of(h, c):
        return k_ref[key_rows(c), h * HEAD_DIM:(h + 1) * HEAD_DIM]

    def vt_of(h, c):
        return vt_ref[h * HEAD_DIM:(h + 1) * HEAD_DIM, key_rows(c)]

    def bias_of(h, c):
        return None if c == 0 else tbl_ref[h, which, (c - 1) * KEY_CHUNK:c * KEY_CHUNK, :]

    def store(h, o):
        o_ref[:, h * HEAD_DIM:(h + 1) * HEAD_DIM] = o.astype(o_ref.dtype)

    @pl.when(t == n_lat_tiles)
    def _():
        _pipelined_heads(NA_HEADS_PER_STEP, 1, q_of, k_of, vt_of, bias_of, s_ref, store)

    @pl.when(t < n_lat_tiles)
    def _():
        _pipelined_heads(NA_HEADS_PER_STEP, 1 + NA_KEYS // KEY_CHUNK, q_of, k_of, vt_of, bias_of, s_ref, store)


def _na_bias_table(rpb):
    cols = np.arange(GRID_W)
    c0 = np.clip(cols - NA_WIN_COLS // 2, 0, GRID_W - NA_WIN_COLS)
    col_ok = (cols[:, None] >= c0[None, :]) & (cols[:, None] < c0[None, :] + NA_WIN_COLS)
    cb = np.clip(cols[:, None] - cols[None, :] + NA_WIN_COLS - 1, 0, 2 * NA_WIN_COLS - 2)
    onehot = (cb[None] == np.arange(2 * NA_WIN_COLS - 1)[:, None, None]).astype(np.float32)
    planes = jnp.einsum("hdc,ckq->hdkq", rpb.astype(F32), onehot, precision=lax.Precision.HIGHEST)
    planes = jnp.where(col_ok[None, None], planes * LOG2E, NEG_BIG)
    n_rel = 2 * NA_WIN_ROWS - 1
    planes = jnp.concatenate([planes, jnp.full((NA_HEADS, 1, GRID_W, GRID_W), NEG_BIG, F32)], axis=1)
    n_lat_tiles = GRID_ROWS // NA_Q_ROWS
    which = np.full((3, NA_K_ROWS, NA_Q_ROWS), n_rel, np.int32)
    for kind, t in enumerate((0, 1, n_lat_tiles - 1)):
        k_row0 = int(np.clip(NA_Q_ROWS * t - NA_WIN_ROWS // 2, 0, GRID_ROWS - NA_K_ROWS))
        for kk in range(NA_K_ROWS):
            for i in range(NA_Q_ROWS):
                qr, kr = NA_Q_ROWS * t + i, k_row0 + kk
                r0 = int(np.clip(qr - NA_WIN_ROWS // 2, 0, GRID_ROWS - NA_WIN_ROWS))
                if r0 <= kr < r0 + NA_WIN_ROWS:
                    which[kind, kk, i] = kr - qr + NA_WIN_ROWS - 1
    tbl = jnp.concatenate([planes[:, which[:, :, i]] for i in range(NA_Q_ROWS)], axis=-1)
    return tbl.reshape(NA_HEADS, 3, NA_KEYS, TM)


def _na_attention(qk, vt, rpb, batch):
    tbl = _na_bias_table(rpb)
    n_lat_tiles = GRID_ROWS // NA_Q_ROWS

    def q_tile(b, t):
        return b * TILES_PER_BATCH + jnp.where(t == n_lat_tiles, 0, t + 1)

    gw = NA_HEADS_PER_STEP * HEAD_DIM
    n_groups = NA_HEADS // NA_HEADS_PER_STEP
    return pl.pallas_call(
        _na_kernel,
        grid=(n_groups, batch, n_lat_tiles + 1),
        in_specs=[
            pl.BlockSpec((TM, gw), lambda g, b, t: (q_tile(b, t), g)),
            pl.BlockSpec((TOK, gw), lambda g, b, t: (b, n_groups + g)),
            pl.BlockSpec((gw, TOK), lambda g, b, t: (g, b)),
            pl.BlockSpec((NA_HEADS_PER_STEP, 3, NA_KEYS, TM), lambda g, b, t: (g, 0, 0, 0)),
        ],
        out_specs=pl.BlockSpec((TM, gw), lambda g, b, t: (q_tile(b, t), g)),
        out_shape=jax.ShapeDtypeStruct((batch * TOK, NA_HEADS * HEAD_DIM), BF16),
        scratch_shapes=[pltpu.VMEM((2, 1 + NA_KEYS // KEY_CHUNK, KEY_CHUNK, TM), F32)],
        compiler_params=_params(),
        name="na_attention",
    )(qk, qk, vt, tbl)


def _route(logits, bias):
    scores = 1.0 / (1.0 + jnp.exp(-logits))
    sel = scores + bias
    srow = [sel[e:e + 1, :] for e in range(N_EXPERTS)]
    crow = [scores[e:e + 1, :] for e in range(N_EXPERTS)]
    gscore = []
    for g in range(N_GROUPS):
        a, b, c, d = srow[GROUP_SIZE * g:GROUP_SIZE * (g + 1)]
        gscore.append(jnp.maximum(jnp.maximum(jnp.maximum(a + b, a + c), jnp.maximum(a + d, b + c)),
                                  jnp.maximum(b + d, c + d)))
    best, gidx = gscore[0], jnp.zeros(gscore[0].shape, jnp.int32)
    for g in range(1, N_GROUPS):
        better = gscore[g] > best
        gidx = jnp.where(better, g, gidx)
        best = jnp.where(better, gscore[g], best)
    s, c = [], []
    for j in range(GROUP_SIZE):
        sj, cj = srow[j], crow[j]
        for g in range(1, N_GROUPS):
            sj = jnp.where(gidx == g, srow[GROUP_SIZE * g + j], sj)
            cj = jnp.where(gidx == g, crow[GROUP_SIZE * g + j], cj)
        s.append(sj)
        c.append(cj)
    chosen = []
    for j in range(GROUP_SIZE):
        rank = jnp.zeros(s[j].shape, jnp.int32)
        for k in range(GROUP_SIZE):
            if k < j:
                rank = rank + jnp.where(s[k] >= s[j], 1, 0)
            elif k > j:
                rank = rank + jnp.where(s[k] > s[j], 1, 0)
        chosen.append(rank < 2)
    w = [jnp.where(chosen[j], c[j], 0.0) for j in range(GROUP_SIZE)]
    denom = (w[0] + w[1]) + (w[2] + w[3])
    gate = [w[j] / denom for j in range(GROUP_SIZE)]
    lo = jnp.where(chosen[0], 0, jnp.where(chosen[1], 1, 2))
    hi = jnp.where(chosen[3], 3, jnp.where(chosen[2], 2, 1))
    g_lo = jnp.where(chosen[0], gate[0], jnp.where(chosen[1], gate[1], gate[2]))
    g_hi = jnp.where(chosen[3], gate[3], jnp.where(chosen[2], gate[2], gate[1]))
    pair = jnp.where(lo == 0, hi - 1, jnp.where(lo == 1, hi + 1, N_PAIRS - 1))
    return gidx * N_PAIRS + pair, g_lo, g_hi


def _slab_rows(c):
    return pl.ds(c, TM, stride=SLAB_PITCH)


def _store_slabs(ref, val, extra):
    for c in range(SLAB_ROWS):
        ref[_slab_rows(c), :] = val[:, c * LANES:(c + 1) * LANES]
    ref[_slab_rows(SLAB_ROWS), :] = extra


def _ffn_prep_tile(h, gain_ref, sc_ref, sh_ref, rw_ref, rb_ref, v_ref, part_ref):
    v = _rms(h, gain_ref[...]) * (1.0 + sc_ref[...]) + sh_ref[...]
    logits = lax.dot_general(rw_ref[...], v, (((1,), (1,)), ((), ())),
                             precision=lax.Precision.HIGHEST, preferred_element_type=F32)
    part, g_lo, g_hi = _route(logits, rb_ref[...])
    part_ref[...] = jnp.zeros(part_ref.shape, jnp.int32)
    part_ref[0:1, :] = part
    row = lax.broadcasted_iota(jnp.int32, (LANES, TM), 0)
    gates_t = jnp.where(row == 0, g_lo, jnp.where(row == 1, g_hi, 0.0))
    _store_slabs(v_ref, v, gates_t.T)


def _dispatch_meta(part, n_slots):
    n = part.shape[0]
    onehot = (part[:, None] == jnp.arange(N_PARTS, dtype=jnp.int32)[None, :]).astype(jnp.int32)
    csum = jnp.cumsum(onehot, axis=0)
    rank = jnp.take_along_axis(csum, part[:, None], axis=1)[:, 0] - 1
    counts = csum[-1]
    padded = ((counts + TM - 1) // TM) * TM
    ends = jnp.cumsum(padded)
    starts = ends - padded
    pos = (starts[part] + rank).astype(jnp.int32)
    src = jnp.zeros((n_slots,), jnp.int32).at[pos].set(jnp.arange(n, dtype=jnp.int32))
    tile_start = jnp.arange(n_slots // TM, dtype=jnp.int32) * TM
    last_start = jnp.minimum(tile_start, ends[-1] - TM)
    tile_part = jnp.sum((ends[None, :] <= last_start[:, None]).astype(jnp.int32), axis=1)
    n_rows = jnp.clip(counts[tile_part] - (tile_start - starts[tile_part]), 0, TM).astype(jnp.int32)
    group, pair = tile_part // N_PAIRS, tile_part % N_PAIRS
    e_lo = group * GROUP_SIZE + jnp.asarray(PAIR_LO, jnp.int32)[pair]
    e_hi = group * GROUP_SIZE + jnp.asarray(PAIR_HI, jnp.int32)[pair]
    return src, e_lo, e_hi, n_rows


def _expert_kernel(src_ref, elo_ref, ehi_ref, nrows_ref, v_hbm, w1a, w3a, w2a, w1b, w3b, w2b, y_hbm,
                   xbuf, ybuf, x2d, in_sem, out_sem, *, n_tok):
    t = pl.program_id(0)
    last = pl.num_programs(0) - 1
    slot = t % 2

    def n_groups(tile):
        return (nrows_ref[tile] + ROW_GROUP - 1) // ROW_GROUP

    def in_copy(slot_, r, tok):
        return pltpu.make_async_copy(v_hbm.at[pl.ds(tok * SLAB_PITCH, SLAB_PITCH), :],
                                     xbuf.at[slot_, pl.ds(r * SLAB_PITCH, SLAB_PITCH), :], in_sem.at[slot_])

    def out_copy(slot_, r, tok):
        return pltpu.make_async_copy(ybuf.at[slot_, pl.ds(r * SLAB_PITCH, SLAB_PITCH), :],
                                     y_hbm.at[pl.ds(tok * SLAB_PITCH, SLAB_PITCH), :], out_sem.at[slot_])

    def for_rows(tile, fn):
        def body(g, carry):
            for u in range(ROW_GROUP):
                fn(g * ROW_GROUP + u, u)
            return carry
        lax.fori_loop(0, n_groups(tile), body, 0)

    def start_in(tile, slot_):
        for_rows(tile, lambda r, u: in_copy(slot_, r, src_ref[tile * TM + r]).start(priority=u % 2))

    def start_out(tile, slot_):
        n = nrows_ref[tile]

        def one(r, u):
            tok = jnp.where(r < n, src_ref[tile * TM + r], n_tok + slot_ * ROW_GROUP + u)
            out_copy(slot_, r, tok).start(priority=u % 2)
        for_rows(tile, one)

    @pl.when(t == 0)
    def _():
        xbuf[...] = jnp.zeros(xbuf.shape, xbuf.dtype)
        spare = 2 * ROW_GROUP * SLAB_PITCH
        fill = pltpu.make_async_copy(xbuf.at[1, pl.ds(0, spare), :],
                                     y_hbm.at[pl.ds(n_tok * SLAB_PITCH, spare), :], out_sem.at[0])
        fill.start()
        fill.wait()
        start_in(0, 0)

    @pl.when(jnp.logical_and(t < last, nrows_ref[jnp.minimum(t + 1, last)] > 0))
    def _():
        start_in(t + 1, 1 - slot)

    @pl.when(nrows_ref[t] > 0)
    def _():
        for_rows(t, lambda r, u: in_copy(slot, r, 0).wait())
        xs = xbuf.at[slot]
        for c in range(SLAB_ROWS):
            x2d[:, c * LANES:(c + 1) * LANES] = xs[_slab_rows(c), :].astype(BF16)
        g = xs[_slab_rows(SLAB_ROWS), :]
        x = x2d[...]

        def hidden(w1, w3, gate):
            a = jnp.dot(x, w1[...], preferred_element_type=F32)
            b = jnp.dot(x, w3[...], preferred_element_type=F32)
            return (_silu(a) * b * gate).astype(BF16)

        y = jnp.dot(hidden(w1a, w3a, g[:, 0:1]), w2a[...], preferred_element_type=F32)
        y = y + jnp.dot(hidden(w1b, w3b, g[:, 1:2]), w2b[...], preferred_element_type=F32)
        _store_slabs(ybuf.at[slot], y, jnp.zeros((TM, LANES), F32))
        start_out(t, slot)

    @pl.when(jnp.logical_and(t > 0, nrows_ref[jnp.maximum(t - 1, 0)] > 0))
    def _():
        for_rows(t - 1, lambda r, u: out_copy(1 - slot, r, 0).wait())

    @pl.when(jnp.logical_and(t == last, nrows_ref[t] > 0))
    def _():
        for_rows(t, lambda r, u: out_copy(slot, r, 0).wait())


def _experts(v_slabs, src, e_lo, e_hi, n_rows, w1, w3, w2, layer, n_tok):
    n_tiles = n_rows.shape[0]
    up = lambda sel: pl.BlockSpec((None, None, D_MODEL, D_EXPERT),
                                  lambda t, s, lo, hi, nr: (layer, (lo, hi)[sel][t], 0, 0))
    down = lambda sel: pl.BlockSpec((None, None, D_EXPERT, D_MODEL),
                                    lambda t, s, lo, hi, nr: (layer, (lo, hi)[sel][t], 0, 0))
    buf = pltpu.VMEM((2, TM * SLAB_PITCH, LANES), F32)
    return pl.pallas_call(
        functools.partial(_expert_kernel, n_tok=n_tok),
        grid_spec=pltpu.PrefetchScalarGridSpec(
            num_scalar_prefetch=4,
            grid=(n_tiles,),
            in_specs=[pl.BlockSpec(memory_space=pl.ANY), up(0), up(0), down(0), up(1), up(1), down(1)],
            out_specs=pl.BlockSpec(memory_space=pl.ANY),
            scratch_shapes=[buf, buf, pltpu.VMEM((TM, D_MODEL), BF16),
                            pltpu.SemaphoreType.DMA((2,)), pltpu.SemaphoreType.DMA((2,))],
        ),
        out_shape=jax.ShapeDtypeStruct(((n_tok + 2 * ROW_GROUP) * SLAB_PITCH, LANES), F32),
        compiler_params=_params(),
        name="experts",
    )(src, e_lo, e_hi, n_rows, v_slabs, w1, w3, w2, w1, w3, w2)


def _moe_experts(v_slabs, part, w1, w3, w2, layer):
    n = part.shape[1]
    src, e_lo, e_hi, n_rows = _dispatch_meta(part[0], n + N_PARTS * TM)
    return _experts(v_slabs, src, e_lo, e_hi, n_rows, w1, w3, w2, layer, n)


def _final_kernel(y_ref, h_ref, gate_ref, gain_ref, o_ref, hn_ref):
    for c in range(SLAB_ROWS):
        cols = slice(c * LANES, (c + 1) * LANES)
        hn_ref[:, cols] = h_ref[:, cols] + gate_ref[:, cols] * y_ref[_slab_rows(c), :]
    o_ref[...] = _rms(hn_ref[...], gain_ref[...])


def _final_norm(y_slabs, h, mod, gain, batch):
    lat_tiles = SEQ // TM
    tile = lambda b, t: b * TILES_PER_BATCH + 1 + t
    return pl.pallas_call(
        _final_kernel,
        grid=(batch, lat_tiles),
        in_specs=[
            pl.BlockSpec((TM * SLAB_PITCH, LANES), lambda b, t: (tile(b, t), 0)),
            pl.BlockSpec((TM, D_MODEL), lambda b, t: (tile(b, t), 0)),
            pl.BlockSpec((None, 1, D_MODEL), lambda b, t: (5 * MOD_ROWS + b, 0, 0)),
            pl.BlockSpec((1, D_MODEL), lambda b, t: (0, 0)),
        ],
        out_specs=pl.BlockSpec((TM, D_MODEL), lambda b, t: (b * lat_tiles + t, 0)),
        out_shape=jax.ShapeDtypeStruct((batch * SEQ, D_MODEL), F32),
        scratch_shapes=[pltpu.VMEM((TM, D_MODEL), F32)],
        compiler_params=_params(),
        name="final_norm",
    )(y_slabs, h, mod, gain.reshape(1, D_MODEL))


def _rope_tables(rot_dim, layout):
    n_freq = rot_dim // 4
    inv_freq = 1.0 / (ROPE_THETA ** (jnp.arange(n_freq, dtype=F32) / n_freq))
    t = jnp.arange(SEQ, dtype=jnp.int32)
    row = (t // GRID_W).astype(F32)
    col = (t % GRID_W).astype(F32)
    ang = jnp.concatenate([row[:, None] * inv_freq, col[:, None] * inv_freq], axis=-1)
    cos, sin = jnp.cos(ang), jnp.sin(ang)
    if layout == "padded":
        pad1, pad0 = jnp.ones_like(cos), jnp.zeros_like(sin)
        c = jnp.concatenate([cos, pad1, cos, pad1], axis=-1)
        s = jnp.concatenate([-sin, pad0, sin, pad0], axis=-1)
    else:
        c = jnp.concatenate([cos, cos], axis=-1)
        s = jnp.concatenate([-sin, sin], axis=-1)
    c = jnp.concatenate([jnp.ones((CTX_LEN, LANES), F32), c], axis=0)
    s = jnp.concatenate([jnp.zeros((CTX_LEN, LANES), F32), s], axis=0)
    return c, s


def _pad_rope_cols(w):
    half = MLA_ROPE // 2
    z = jnp.zeros(w.shape[:-1] + (half,), w.dtype)
    return jnp.concatenate([w[..., :half], z, w[..., half:], z], axis=-1)


def _first_proj(h, comb, w, **kw):
    if comb is None:
        return _proj(h, w, **kw), h
    out = _proj(None, w, comb=(comb[0], h, comb[1]), **kw)
    if kw["kinds"][-1][0] == "vt":
        return out[:2], out[2]
    return out[0], out[1]


def _out_proj(o, w_o, h, mod, ffn):
    return _proj(o, w_o.astype(BF16), out_dtype=F32, kinds=[("resid",)] * (D_MODEL // LANES),
                 res=h, mod=mod, mod_gate=2, ffn=ffn)


def _gqa_layer(h, comb, mod, norm_gain, w_qkv, q_gain, k_gain, w_o, tabs, ffn, batch):
    kinds = ([("normrope", 0, 0)] * GQA_HEADS + [("normrope", 1, 1)] * GQA_KV_HEADS
             + [("vt", j) for j in range(GQA_KV_HEADS)])
    (qk, vt), h = _first_proj(h, comb, w_qkv.astype(BF16), out_dtype=BF16, kinds=kinds, prologue="normmod",
                              gain=norm_gain, mod=mod, mod_sc=1, mod_sh=0, tabs=tabs,
                              hgain=jnp.stack([q_gain, k_gain]))
    o = _attention(qk, qk, vt, batch=batch, groups=GQA_KV_HEADS, heads_per_step=GQA_GROUP, dq=HEAD_DIM,
                   k_blk0=GQA_HEADS)
    return _out_proj(o, w_o, h, mod, ffn)


def _mla_layer(h, comb, mod, norm_gain, w_a, q_gain, kv_gain, w_uq, w_ukv, w_o, tabs, ffn, batch):
    n_lat = 2 * MLA_RANK // LANES
    w_a_p = jnp.concatenate([w_a[:, :2 * MLA_RANK], _pad_rope_cols(w_a[:, 2 * MLA_RANK:])], axis=1)
    a, h = _first_proj(h, comb, w_a_p.astype(BF16), out_dtype=F32, kinds=[("plain",)] * n_lat + [("rope", 1)],
                       prologue="normmod", gain=norm_gain, mod=mod, mod_sc=1, mod_sh=0, tabs=tabs,
                       chunk=LANES * (n_lat + 1))
    scale = MLA_QK ** -0.5 * LOG2E
    w_uq_h = w_uq.reshape(MLA_RANK, MLA_HEADS, MLA_QK)
    w_uq_p = jnp.concatenate([w_uq_h[..., :MLA_NOPE], _pad_rope_cols(w_uq_h[..., MLA_NOPE:])], axis=-1)
    q = _proj(a, w_uq_p.reshape(MLA_RANK, MLA_HEADS * 2 * LANES).astype(BF16), out_dtype=BF16,
              kinds=[("scale", scale), ("rope", 0)] * MLA_HEADS, prologue="rms", x_col_block=0, gain=q_gain, tabs=tabs)
    w_ukv_h = w_ukv.reshape(MLA_RANK, MLA_HEADS, 2 * HEAD_DIM)
    w_ukv_p = jnp.concatenate([w_ukv_h[..., :MLA_NOPE].reshape(MLA_RANK, -1),
                               w_ukv_h[..., MLA_NOPE:].reshape(MLA_RANK, -1)], axis=1)
    k_nope, vt = _proj(a, w_ukv_p.astype(BF16), out_dtype=BF16,
                       kinds=[("plain",)] * MLA_HEADS + [("vt", j) for j in range(MLA_HEADS)],
                       prologue="rms", x_col_block=1, gain=kv_gain)
    o = _attention(q, k_nope, vt, batch=batch, groups=MLA_HEADS // MLA_HEADS_PER_STEP,
                   heads_per_step=MLA_HEADS_PER_STEP, dq=2 * LANES, k_blk0=0, kpe_arr=a, kpe_blk=n_lat)
    return _out_proj(o, w_o, h, mod, ffn)


def _na_layer(h, comb, mod, norm_gain, w_qkv, rpb, w_o, ffn, batch):
    scale = HEAD_DIM ** -0.5 * LOG2E
    kinds = [("scale", scale)] * NA_HEADS + [("plain",)] * NA_HEADS + [("vt", j) for j in range(NA_HEADS)]
    (qk, vt), h = _first_proj(h, comb, w_qkv.astype(BF16), out_dtype=BF16, kinds=kinds, prologue="normmod",
                              gain=norm_gain, mod=mod, mod_sc=1, mod_sh=0)
    o = _na_attention(qk, vt, rpb, batch)
    return _out_proj(o, w_o, h, mod, ffn)


def kernel(x, c, ctx, c_ctx, mod_w, mod_b, norm_mix, norm_ffn, gqa_wqkv, gqa_q_gain, gqa_k_gain, gqa_wo,
           mla_wa, mla_q_gain, mla_kv_gain, mla_wuq, mla_wukv, mla_wo, na_wqkv, na_rpb, na_wo,
           router_w, router_bias, moe_w1, moe_w3, moe_w2, final_gain):
    batch = x.shape[0]
    assert batch <= CTX_MOD_ROW and x.shape[1:] == (SEQ, D_MODEL) and ctx.shape[1:] == (CTX_LEN, D_MODEL)
    h = jnp.concatenate([ctx, x], axis=1).reshape(batch * TOK, D_MODEL)
    cond = jnp.zeros((MOD_ROWS, D_MODEL), F32).at[:batch].set(c).at[CTX_MOD_ROW].set(c_ctx)
    mods = _modulation(cond, mod_w, mod_b)
    mods = mods.reshape(DEPTH, MOD_ROWS, 6, D_MODEL).transpose(0, 2, 1, 3).reshape(DEPTH, 6 * MOD_ROWS, 1, D_MODEL)

    c_full, s_full = _rope_tables(HEAD_DIM, "full")
    gqa_scale = HEAD_DIM ** -0.5 * LOG2E
    gqa_tabs = (jnp.concatenate([c_full * gqa_scale, c_full], axis=1),
                jnp.concatenate([s_full * gqa_scale, s_full], axis=1))
    c_pad, s_pad = _rope_tables(MLA_ROPE, "padded")
    mla_scale = MLA_QK ** -0.5 * LOG2E
    mla_tabs = (jnp.concatenate([c_pad * mla_scale, c_pad], axis=1),
                jnp.concatenate([s_pad * mla_scale, s_pad], axis=1))
    router_wt = router_w.T
    w1_b, w3_b, w2_b = moe_w1.astype(BF16), moe_w3.astype(BF16), moe_w2.astype(BF16)

    comb = None
    for i in range(DEPTH):
        mod = mods[i]
        ffn = (norm_ffn[i], router_wt, router_bias)
        kind, j = i % N_MIXERS, i // N_MIXERS
        if kind == 0:
            out = _gqa_layer(h, comb, mod, norm_mix[i], gqa_wqkv[j], gqa_q_gain[j], gqa_k_gain[j], gqa_wo[j],
                             gqa_tabs, ffn, batch)
        elif kind == 1:
            out = _mla_layer(h, comb, mod, norm_mix[i], mla_wa[j], mla_q_gain[j], mla_kv_gain[j], mla_wuq[j],
                             mla_wukv[j], mla_wo[j], mla_tabs, ffn, batch)
        else:
            out = _na_layer(h, comb, mod, norm_mix[i], na_wqkv[j], na_rpb[j], na_wo[j], ffn, batch)
        h, v_slabs, part = out
        comb = (_moe_experts(v_slabs, part, w1_b, w3_b, w2_b, i), mod)
    return _final_norm(comb[0], h, comb[1], final_gain, batch).reshape(batch, SEQ, D_MODEL)
```

```python
import functools

import numpy as np
import jax
import jax.numpy as jnp
from jax import lax
from jax.experimental import pallas as pl
from jax.experimental.pallas import tpu as pltpu

F32 = jnp.float32
BF16 = jnp.bfloat16

D_MODEL = 2048
SEQ = 2048
GRID_W = 64
GRID_ROWS = SEQ // GRID_W
CTX_LEN = 256
TOK = CTX_LEN + SEQ
DEPTH = 4
N_MIXERS = 3
EPS = 1e-6
ROPE_THETA = 10000.0

GQA_HEADS = 16
GQA_KV_HEADS = 4
HEAD_DIM = 128
GQA_GROUP = GQA_HEADS // GQA_KV_HEADS

MLA_HEADS = 16
MLA_RANK = 512
MLA_NOPE = 128
MLA_ROPE = 64
MLA_QK = MLA_NOPE + MLA_ROPE

NA_HEADS = 16
NA_WIN_ROWS = 8
NA_WIN_COLS = 16
NA_Q_ROWS = 4
NA_K_ROWS = 12
NA_KEYS = NA_K_ROWS * GRID_W
NA_HEADS_PER_STEP = 4
MLA_HEADS_PER_STEP = 8
GQA_KV_PER_STEP = 2
KEY_CHUNK = 256

N_EXPERTS = 16
N_GROUPS = 4
GROUP_SIZE = N_EXPERTS // N_GROUPS
N_PAIRS = 6
N_PARTS = N_GROUPS * N_PAIRS
D_EXPERT = 512
PAIR_LO = (0, 0, 0, 1, 1, 2)
PAIR_HI = (1, 2, 3, 2, 3, 3)

LANES = 128
TM = 256
TILES_PER_BATCH = TOK // TM
SLAB_ROWS = D_MODEL // LANES
SLAB_PITCH = SLAB_ROWS + 1
ROW_GROUP = 8
MOD_ROWS = 16
CTX_MOD_ROW = 8
NEG_BIG = -1e30
LOG2E = 1.4426950408889634
VMEM_LIMIT = 56 * 1024 * 1024


def _params():
    return pltpu.CompilerParams(vmem_limit_bytes=VMEM_LIMIT)


def _mod_row(i):
    return jnp.where(i % TILES_PER_BATCH == 0, CTX_MOD_ROW, i // TILES_PER_BATCH)


def _silu(x):
    return x / (1.0 + jnp.exp(-x))


def _mod_kernel(cond_ref, w_ref, b_ref, o_ref):
    s = _silu(cond_ref[...]).astype(BF16)
    o_ref[...] = jnp.dot(s, w_ref[...].astype(BF16), preferred_element_type=F32) + b_ref[...]


def _modulation(cond, mod_w, mod_b):
    tn = 1024
    n_out = mod_w.shape[-1]
    return pl.pallas_call(
        _mod_kernel,
        grid=(DEPTH, n_out // tn),
        in_specs=[
            pl.BlockSpec((MOD_ROWS, D_MODEL), lambda l, j: (0, 0)),
            pl.BlockSpec((None, D_MODEL, tn), lambda l, j: (l, 0, j)),
            pl.BlockSpec((None, 1, tn), lambda l, j: (l, 0, j)),
        ],
        out_specs=pl.BlockSpec((None, MOD_ROWS, tn), lambda l, j: (l, 0, j)),
        out_shape=jax.ShapeDtypeStruct((DEPTH, MOD_ROWS, n_out), F32),
        compiler_params=_params(),
        name="modulation",
    )(cond, mod_w, mod_b.reshape(DEPTH, 1, n_out))


def _rms(x, gain):
    return x * lax.rsqrt(jnp.mean(x * x, axis=-1, keepdims=True) + EPS) * gain


def _rope(x, c, s):
    return x * c + pltpu.roll(x, LANES // 2, 1) * s


def _runs(kinds):
    out, start = [], 0
    for k in range(1, len(kinds) + 1):
        if k == len(kinds) or kinds[k] != kinds[start] or kinds[start][0] in ("rope", "normrope", "vt"):
            out.append((start, k, kinds[start]))
            start = k
    return out


def _proj_kernel(*refs, prologue, kinds, chunk, has_tab, has_hgain, resid, has_vt, comb, ffn):
    it = iter(refs)
    if comb:
        y_ref, hprev_ref, cgate_ref = next(it), next(it), next(it)
    else:
        x_ref = next(it)
    gain_ref = sc_ref = sh_ref = None
    if prologue == "normmod":
        gain_ref, sc_ref, sh_ref = next(it), next(it), next(it)
    elif prologue == "rms":
        gain_ref = next(it)
    w_ref = next(it)
    ctab_ref = stab_ref = hgain_ref = res_ref = gate_ref = None
    if has_tab:
        ctab_ref, stab_ref = next(it), next(it)
    if has_hgain:
        hgain_ref = next(it)
    if resid:
        res_ref, gate_ref = next(it), next(it)
    if ffn:
        fgain_ref, fsc_ref, fsh_ref, rw_ref, rb_ref = (next(it) for _ in range(5))
    o_ref = next(it)
    vt_ref = next(it) if has_vt else None
    hnew_ref = next(it) if comb else None
    if ffn:
        v_ref, part_ref = next(it), next(it)

    if comb:
        for c in range(SLAB_ROWS):
            cols = slice(c * LANES, (c + 1) * LANES)
            hnew_ref[:, cols] = hprev_ref[:, cols] + cgate_ref[:, cols] * y_ref[_slab_rows(c), :]
        x_ref = hnew_ref

    if prologue == "none":
        u = x_ref[...]
    else:
        y = _rms(x_ref[...], gain_ref[...])
        if prologue == "normmod":
            y = y * (1.0 + sc_ref[...]) + sh_ref[...]
        u = y.astype(BF16)

    for c0 in range(0, len(kinds) * LANES, chunk):
        acc = jnp.dot(u, w_ref[:, c0:c0 + chunk], preferred_element_type=F32)
        b0 = c0 // LANES
        for lo, hi, kind in _runs(kinds[b0:b0 + chunk // LANES]):
            a = acc[:, lo * LANES:hi * LANES]
            cols = slice(c0 + lo * LANES, c0 + hi * LANES)
            if kind[0] == "scale":
                a = a * kind[1]
            elif kind[0] == "rope":
                t = kind[1]
                a = _rope(a, ctab_ref[:, t * LANES:(t + 1) * LANES], stab_ref[:, t * LANES:(t + 1) * LANES])
            elif kind[0] == "normrope":
                t, g = kind[1], kind[2]
                a = _rms(a, hgain_ref[g:g + 1, :])
                a = _rope(a, ctab_ref[:, t * LANES:(t + 1) * LANES], stab_ref[:, t * LANES:(t + 1) * LANES])
            elif kind[0] == "resid":
                a = res_ref[:, cols] + gate_ref[:, cols] * a
            elif kind[0] == "vt":
                vt_ref[kind[1] * LANES:(kind[1] + 1) * LANES, :] = a.T.astype(vt_ref.dtype)
                continue
            o_ref[:, cols] = a.astype(o_ref.dtype)

    if ffn:
        _ffn_prep_tile(o_ref[...], fgain_ref, fsc_ref, fsh_ref, rw_ref, rb_ref, v_ref, part_ref)


def _proj(x, w, *, out_dtype, kinds, prologue="none", x_col_block=0, gain=None, mod=None,
          mod_sc=None, mod_sh=None, tabs=None, hgain=None, res=None, mod_gate=None, chunk=512,
          comb=None, ffn=None):
    k, n_all = w.shape
    n = comb[1].shape[0] if comb else x.shape[0]
    n_vt = sum(kind[0] == "vt" for kind in kinds)
    n_out = n_all - n_vt * LANES
    assert n % TM == 0 and n_all % chunk == 0 and len(kinds) == n_all // LANES
    assert all(kind[0] == "vt" for kind in kinds[n_out // LANES:])
    if comb:
        y_slabs, h_prev, mod_prev = comb
        args = [y_slabs, h_prev, mod_prev]
        specs = [pl.BlockSpec((TM * SLAB_PITCH, LANES), lambda i: (i, 0)),
                 pl.BlockSpec((TM, k), lambda i: (i, 0)),
                 pl.BlockSpec((None, 1, k), lambda i: (5 * MOD_ROWS + _mod_row(i), 0, 0))]
    else:
        args = [x]
        specs = [pl.BlockSpec((TM, k), lambda i: (i, x_col_block))]
    if prologue in ("normmod", "rms"):
        args.append(gain.reshape(1, k))
        specs.append(pl.BlockSpec((1, k), lambda i: (0, 0)))
    if prologue == "normmod":
        for comp in (mod_sc, mod_sh):
            args.append(mod)
            specs.append(pl.BlockSpec((None, 1, k), lambda i, comp=comp: (comp * MOD_ROWS + _mod_row(i), 0, 0)))
    args.append(w)
    specs.append(pl.BlockSpec((k, n_all), lambda i: (0, 0), pipeline_mode=pl.Buffered(1)))
    if tabs is not None:
        for t in tabs:
            args.append(t)
            specs.append(pl.BlockSpec((TM, t.shape[1]), lambda i: (i % TILES_PER_BATCH, 0)))
    if hgain is not None:
        args.append(hgain)
        specs.append(pl.BlockSpec(hgain.shape, lambda i: (0, 0)))
    if res is not None:
        args.append(res)
        specs.append(pl.BlockSpec((TM, n_out), lambda i: (i, 0)))
        args.append(mod)
        specs.append(pl.BlockSpec((None, 1, n_out), lambda i: (mod_gate * MOD_ROWS + _mod_row(i), 0, 0)))
    if ffn:
        f_gain, router_wt, router_bias = ffn
        assert res is not None and n_out == D_MODEL
        args += [f_gain.reshape(1, D_MODEL), mod, mod, router_wt, router_bias.reshape(N_EXPERTS, 1)]
        specs += [
            pl.BlockSpec((1, D_MODEL), lambda i: (0, 0)),
            pl.BlockSpec((None, 1, D_MODEL), lambda i: (4 * MOD_ROWS + _mod_row(i), 0, 0)),
            pl.BlockSpec((None, 1, D_MODEL), lambda i: (3 * MOD_ROWS + _mod_row(i), 0, 0)),
            pl.BlockSpec((D_MODEL, 2 * LANES), lambda i: (0, 0)),
            pl.BlockSpec((N_EXPERTS, 1), lambda i: (0, 0)),
        ]
    kern = functools.partial(_proj_kernel, prologue=prologue, kinds=tuple(kinds), chunk=chunk,
                             has_tab=tabs is not None, has_hgain=hgain is not None, resid=res is not None,
                             has_vt=n_vt > 0, comb=comb is not None, ffn=ffn is not None)
    out_specs = [pl.BlockSpec((TM, n_out), lambda i: (i, 0))]
    out_shape = [jax.ShapeDtypeStruct((n, n_out), out_dtype)]
    if n_vt:
        out_specs.append(pl.BlockSpec((n_vt * LANES, TM), lambda i: (0, i)))
        out_shape.append(jax.ShapeDtypeStruct((n_vt * LANES, n), out_dtype))
    if comb:
        out_specs.append(pl.BlockSpec((TM, k), lambda i: (i, 0)))
        out_shape.append(jax.ShapeDtypeStruct((n, k), F32))
    if ffn:
        out_specs += [pl.BlockSpec((TM * SLAB_PITCH, LANES), lambda i: (i, 0)), pl.BlockSpec((8, TM), lambda i: (0, i))]
        out_shape += [jax.ShapeDtypeStruct((n * SLAB_PITCH, LANES), F32), jax.ShapeDtypeStruct((8, n), jnp.int32)]
    out = pl.pallas_call(
        kern,
        grid=(n // TM,),
        in_specs=specs,
        out_specs=out_specs,
        out_shape=out_shape,
        compiler_params=_params(),
        name="proj",
    )(*args)
    return out if len(out) > 1 else out[0]


def _kq(k, q):
    return lax.dot_general(k, q, (((1,), (1,)), ((), ())), preferred_element_type=F32)


def _pipelined_heads(n_heads, n_chunks, q_of, k_of, vt_of, bias_of, s_ref, store):
    def scores(h, c, slot):
        s = _kq(k_of(h, c), q_of(h))
        b = bias_of(h, c)
        s_ref[slot, c] = s if b is None else s + b

    for c in range(n_chunks):
        scores(0, c, 0)
    for h in range(n_heads):
        slot = h % 2
        m = functools.reduce(jnp.maximum,
                             [jnp.max(s_ref[slot, c], axis=0, keepdims=True) for c in range(n_chunks)])
        num, den = None, None
        for c in range(n_chunks):
            if h + 1 < n_heads:
                scores(h + 1, c, 1 - slot)
            p = jnp.exp2(s_ref[slot, c] - m)
            l = jnp.sum(p, axis=0, keepdims=True)
            o = jnp.dot(vt_of(h, c), p.astype(BF16), preferred_element_type=F32)
            num = o if num is None else num + o
            den = l if den is None else den + l
        store(h, (num / den).T)


def _attn_kernel(*refs, heads, kv_heads, dq, has_kpe):
    if has_kpe:
        q_ref, k_ref, kpe_ref, vt_ref, o_ref, s_ref, keys_ref = refs
    else:
        q_ref, k_ref, vt_ref, o_ref, s_ref = refs
        keys_ref = None
    qt = pl.program_id(2)

    if has_kpe:
        @pl.when(qt == 0)
        def _():
            kpe = kpe_ref[...].astype(BF16)
            for h in range(heads):
                keys_ref[h, :, 0:HEAD_DIM] = k_ref[:, h * HEAD_DIM:(h + 1) * HEAD_DIM]
                keys_ref[h, :, HEAD_DIM:] = kpe

    def kv_of(h):
        return h // (heads // kv_heads)

    def q_of(h):
        return q_ref[:, h * dq:(h + 1) * dq]

    def k_of(h, c):
        rows = slice(c * KEY_CHUNK, (c + 1) * KEY_CHUNK)
        if has_kpe:
            return keys_ref[h, rows, :]
        return k_ref[rows, kv_of(h) * HEAD_DIM:(kv_of(h) + 1) * HEAD_DIM]

    def vt_of(h, c):
        return vt_ref[kv_of(h) * HEAD_DIM:(kv_of(h) + 1) * HEAD_DIM, c * KEY_CHUNK:(c + 1) * KEY_CHUNK]

    def store(h, o):
        o_ref[:, h * HEAD_DIM:(h + 1) * HEAD_DIM] = o.astype(o_ref.dtype)

    @pl.when(qt == 0)
    def _():
        _pipelined_heads(heads, CTX_LEN // KEY_CHUNK, q_of, k_of, vt_of, lambda h, c: None, s_ref, store)

    @pl.when(qt > 0)
    def _():
        _pipelined_heads(heads, TOK // KEY_CHUNK, q_of, k_of, vt_of, lambda h, c: None, s_ref, store)


def _attention(q_arr, k_arr, vt_arr, *, batch, groups, heads_per_step, kv_per_step, dq, k_blk0,
               kpe_arr=None, kpe_blk=0):
    shared_kv = kpe_arr is None
    kw = kv_per_step * HEAD_DIM
    ow = heads_per_step * HEAD_DIM
    args = [q_arr, k_arr]
    specs = [
        pl.BlockSpec((TM, heads_per_step * dq), lambda b, g, t: (b * TILES_PER_BATCH + t, g)),
        pl.BlockSpec((TOK, kw), lambda b, g, t: (b, k_blk0 + g)),
    ]
    scratch = [pltpu.VMEM((2, TOK // KEY_CHUNK, KEY_CHUNK, TM), F32)]
    if not shared_kv:
        args.append(kpe_arr)
        specs.append(pl.BlockSpec((TOK, LANES), lambda b, g, t: (b, kpe_blk)))
        scratch.append(pltpu.VMEM((heads_per_step, TOK, HEAD_DIM + LANES), BF16))
    args.append(vt_arr)
    specs.append(pl.BlockSpec((kw, TOK), lambda b, g, t: (g, b)))
    kern = functools.partial(_attn_kernel, heads=heads_per_step, kv_heads=kv_per_step, dq=dq,
                             has_kpe=not shared_kv)
    return pl.pallas_call(
        kern,
        grid=(batch, groups, TILES_PER_BATCH),
        in_specs=specs,
        out_specs=pl.BlockSpec((TM, ow), lambda b, g, t: (b * TILES_PER_BATCH + t, g)),
        out_shape=jax.ShapeDtypeStruct((batch * TOK, groups * ow), BF16),
        scratch_shapes=scratch,
        compiler_params=_params(),
        name="attention",
    )(*args)


def _na_kernel(q_ref, k_ref, vt_ref, tbl_ref, o_ref, s_ref):
    t = pl.program_id(2)
    n_lat_tiles = GRID_ROWS // NA_Q_ROWS
    k_row0 = jnp.clip(NA_Q_ROWS * t - NA_WIN_ROWS // 2, 0, GRID_ROWS - NA_K_ROWS)
    start = pl.multiple_of(CTX_LEN + k_row0 * GRID_W, KEY_CHUNK)
    which = jnp.where(t == 0, 0, jnp.where(t == n_lat_tiles - 1, 2, 1))

    def key_rows(c):
        return pl.ds(0, KEY_CHUNK) if c == 0 else pl.ds(start + (c - 1) * KEY_CHUNK, KEY_CHUNK)

    def q_of(h):
        return q_ref[:, h * HEAD_DIM:(h + 1) * HEAD_DIM]

    def k_of(h, c):
        return k_ref[key_rows(c), h * HEAD_DIM:(h + 1) * HEAD_DIM]

    def vt_of(h, c):
        return vt_ref[h * HEAD_DIM:(h + 1) * HEAD_DIM, key_rows(c)]

    def bias_of(h, c):
        return None if c == 0 else tbl_ref[h, which, (c - 1) * KEY_CHUNK:c * KEY_CHUNK, :]

    def store(h, o):
        o_ref[:, h * HEAD_DIM:(h + 1) * HEAD_DIM] = o.astype(o_ref.dtype)

    @pl.when(t == n_lat_tiles)
    def _():
        _pipelined_heads(NA_HEADS_PER_STEP, 1, q_of, k_of, vt_of, bias_of, s_ref, store)

    @pl.when(t < n_lat_tiles)
    def _():
        _pipelined_heads(NA_HEADS_PER_STEP, 1 + NA_KEYS // KEY_CHUNK, q_of, k_of, vt_of, bias_of, s_ref, store)


def _na_bias_table(rpb):
    cols = np.arange(GRID_W)
    c0 = np.clip(cols - NA_WIN_COLS // 2, 0, GRID_W - NA_WIN_COLS)
    col_ok = (cols[:, None] >= c0[None, :]) & (cols[:, None] < c0[None, :] + NA_WIN_COLS)
    cb = np.clip(cols[:, None] - cols[None, :] + NA_WIN_COLS - 1, 0, 2 * NA_WIN_COLS - 2)
    onehot = (cb[None] == np.arange(2 * NA_WIN_COLS - 1)[:, None, None]).astype(np.float32)
    planes = jnp.einsum("hdc,ckq->hdkq", rpb.astype(F32), onehot, precision=lax.Precision.HIGHEST)
    planes = jnp.where(col_ok[None, None], planes * LOG2E, NEG_BIG)
    n_rel = 2 * NA_WIN_ROWS - 1
    planes = jnp.concatenate([planes, jnp.full((NA_HEADS, 1, GRID_W, GRID_W), NEG_BIG, F32)], axis=1)
    n_lat_tiles = GRID_ROWS // NA_Q_ROWS
    which = np.full((3, NA_K_ROWS, NA_Q_ROWS), n_rel, np.int32)
    for kind, t in enumerate((0, 1, n_lat_tiles - 1)):
        k_row0 = int(np.clip(NA_Q_ROWS * t - NA_WIN_ROWS // 2, 0, GRID_ROWS - NA_K_ROWS))
        for kk in range(NA_K_ROWS):
            for i in range(NA_Q_ROWS):
                qr, kr = NA_Q_ROWS * t + i, k_row0 + kk
                r0 = int(np.clip(qr - NA_WIN_ROWS // 2, 0, GRID_ROWS - NA_WIN_ROWS))
                if r0 <= kr < r0 + NA_WIN_ROWS:
                    which[kind, kk, i] = kr - qr + NA_WIN_ROWS - 1
    tbl = jnp.concatenate([planes[:, which[:, :, i]] for i in range(NA_Q_ROWS)], axis=-1)
    return tbl.reshape(NA_HEADS, 3, NA_KEYS, TM)


def _na_attention(qk, vt, rpb, batch):
    tbl = _na_bias_table(rpb)
    n_lat_tiles = GRID_ROWS // NA_Q_ROWS

    def q_tile(b, t):
        return b * TILES_PER_BATCH + jnp.where(t == n_lat_tiles, 0, t + 1)

    gw = NA_HEADS_PER_STEP * HEAD_DIM
    n_groups = NA_HEADS // NA_HEADS_PER_STEP
    return pl.pallas_call(
        _na_kernel,
        grid=(n_groups, batch, n_lat_tiles + 1),
        in_specs=[
            pl.BlockSpec((TM, gw), lambda g, b, t: (q_tile(b, t), g)),
            pl.BlockSpec((TOK, gw), lambda g, b, t: (b, n_groups + g)),
            pl.BlockSpec((gw, TOK), lambda g, b, t: (g, b)),
            pl.BlockSpec((NA_HEADS_PER_STEP, 3, NA_KEYS, TM), lambda g, b, t: (g, 0, 0, 0)),
        ],
        out_specs=pl.BlockSpec((TM, gw), lambda g, b, t: (q_tile(b, t), g)),
        out_shape=jax.ShapeDtypeStruct((batch * TOK, NA_HEADS * HEAD_DIM), BF16),
        scratch_shapes=[pltpu.VMEM((2, 1 + NA_KEYS // KEY_CHUNK, KEY_CHUNK, TM), F32)],
        compiler_params=_params(),
        name="na_attention",
    )(qk, qk, vt, tbl)


def _route(logits, bias):
    scores = 1.0 / (1.0 + jnp.exp(-logits))
    sel = scores + bias
    srow = [sel[e:e + 1, :] for e in range(N_EXPERTS)]
    crow = [scores[e:e + 1, :] for e in range(N_EXPERTS)]
    gscore = []
    for g in range(N_GROUPS):
        a, b, c, d = srow[GROUP_SIZE * g:GROUP_SIZE * (g + 1)]
        gscore.append(jnp.maximum(jnp.maximum(jnp.maximum(a + b, a + c), jnp.maximum(a + d, b + c)),
                                  jnp.maximum(b + d, c + d)))
    best, gidx = gscore[0], jnp.zeros(gscore[0].shape, jnp.int32)
    for g in range(1, N_GROUPS):
        better = gscore[g] > best
        gidx = jnp.where(better, g, gidx)
        best = jnp.where(better, gscore[g], best)
    s, c = [], []
    for j in range(GROUP_SIZE):
        sj, cj = srow[j], crow[j]
        for g in range(1, N_GROUPS):
            sj = jnp.where(gidx == g, srow[GROUP_SIZE * g + j], sj)
            cj = jnp.where(gidx == g, crow[GROUP_SIZE * g + j], cj)
        s.append(sj)
        c.append(cj)
    chosen = []
    for j in range(GROUP_SIZE):
        rank = jnp.zeros(s[j].shape, jnp.int32)
        for k in range(GROUP_SIZE):
            if k < j:
                rank = rank + jnp.where(s[k] >= s[j], 1, 0)
            elif k > j:
                rank = rank + jnp.where(s[k] > s[j], 1, 0)
        chosen.append(rank < 2)
    w = [jnp.where(chosen[j], c[j], 0.0) for j in range(GROUP_SIZE)]
    denom = (w[0] + w[1]) + (w[2] + w[3])
    gate = [w[j] / denom for j in range(GROUP_SIZE)]
    lo = jnp.where(chosen[0], 0, jnp.where(chosen[1], 1, 2))
    hi = jnp.where(chosen[3], 3, jnp.where(chosen[2], 2, 1))
    g_lo = jnp.where(chosen[0], gate[0], jnp.where(chosen[1], gate[1], gate[2]))
    g_hi = jnp.where(chosen[3], gate[3], jnp.where(chosen[2], gate[2], gate[1]))
    pair = jnp.where(lo == 0, hi - 1, jnp.where(lo == 1, hi + 1, N_PAIRS - 1))
    return gidx * N_PAIRS + pair, g_lo, g_hi


def _slab_rows(c):
    return pl.ds(c, TM, stride=SLAB_PITCH)


def _store_slabs(ref, val, extra):
    for c in range(SLAB_ROWS):
        ref[_slab_rows(c), :] = val[:, c * LANES:(c + 1) * LANES]
    ref[_slab_rows(SLAB_ROWS), :] = extra


def _ffn_prep_tile(h, gain_ref, sc_ref, sh_ref, rw_ref, rb_ref, v_ref, part_ref):
    v = _rms(h, gain_ref[...]) * (1.0 + sc_ref[...]) + sh_ref[...]
    v_hi = v.astype(BF16)
    v_lo = (v - v_hi.astype(F32)).astype(BF16)
    a = jnp.dot(v_hi, rw_ref[...], preferred_element_type=F32)
    b = jnp.dot(v_lo, rw_ref[:, 0:LANES], preferred_element_type=F32)
    logits = (a[:, 0:LANES] + a[:, LANES:] + b).T[0:N_EXPERTS, :]
    part, g_lo, g_hi = _route(logits, rb_ref[...])
    part_ref[...] = jnp.zeros(part_ref.shape, jnp.int32)
    part_ref[0:1, :] = part
    row = lax.broadcasted_iota(jnp.int32, (LANES, TM), 0)
    gates_t = jnp.where(row == 0, g_lo, jnp.where(row == 1, g_hi, 0.0))
    _store_slabs(v_ref, v, gates_t.T)


def _dispatch_meta(part, n_slots):
    n = part.shape[0]
    onehot = (part[:, None] == jnp.arange(N_PARTS, dtype=jnp.int32)[None, :]).astype(jnp.int32)
    csum = jnp.cumsum(onehot, axis=0)
    rank = jnp.take_along_axis(csum, part[:, None], axis=1)[:, 0] - 1
    counts = csum[-1]
    padded = ((counts + TM - 1) // TM) * TM
    ends = jnp.cumsum(padded)
    starts = ends - padded
    pos = (starts[part] + rank).astype(jnp.int32)
    src = jnp.zeros((n_slots,), jnp.int32).at[pos].set(jnp.arange(n, dtype=jnp.int32))
    tile_start = jnp.arange(n_slots // TM, dtype=jnp.int32) * TM
    last_start = jnp.minimum(tile_start, ends[-1] - TM)
    tile_part = jnp.sum((ends[None, :] <= last_start[:, None]).astype(jnp.int32), axis=1)
    n_rows = jnp.clip(counts[tile_part] - (tile_start - starts[tile_part]), 0, TM).astype(jnp.int32)
    group, pair = tile_part // N_PAIRS, tile_part % N_PAIRS
    e_lo = group * GROUP_SIZE + jnp.asarray(PAIR_LO, jnp.int32)[pair]
    e_hi = group * GROUP_SIZE + jnp.asarray(PAIR_HI, jnp.int32)[pair]
    return src, e_lo, e_hi, n_rows


def _expert_kernel(src_ref, elo_ref, ehi_ref, nrows_ref, v_hbm, w1a, w3a, w2a, w1b, w3b, w2b, y_hbm,
                   xbuf, ybuf, x2d, in_sem, out_sem, *, n_tok):
    t = pl.program_id(0)
    last = pl.num_programs(0) - 1
    slot = t % 2

    def n_groups(tile):
        return (nrows_ref[tile] + ROW_GROUP - 1) // ROW_GROUP

    def in_copy(slot_, r, tok):
        return pltpu.make_async_copy(v_hbm.at[pl.ds(tok * SLAB_PITCH, SLAB_PITCH), :],
                                     xbuf.at[slot_, pl.ds(r * SLAB_PITCH, SLAB_PITCH), :], in_sem.at[slot_])

    def out_copy(slot_, r, tok):
        return pltpu.make_async_copy(ybuf.at[slot_, pl.ds(r * SLAB_PITCH, SLAB_PITCH), :],
                                     y_hbm.at[pl.ds(tok * SLAB_PITCH, SLAB_PITCH), :], out_sem.at[slot_])

    def for_rows(tile, fn):
        def body(g, carry):
            for u in range(ROW_GROUP):
                fn(g * ROW_GROUP + u, u)
            return carry
        lax.fori_loop(0, n_groups(tile), body, 0)

    def start_in(tile, slot_):
        for_rows(tile, lambda r, u: in_copy(slot_, r, src_ref[tile * TM + r]).start(priority=u % 2))

    def start_out(tile, slot_):
        n = nrows_ref[tile]

        def one(r, u):
            tok = jnp.where(r < n, src_ref[tile * TM + r], n_tok + slot_ * ROW_GROUP + u)
            out_copy(slot_, r, tok).start(priority=u % 2)
        for_rows(tile, one)

    @pl.when(t == 0)
    def _():
        xbuf[...] = jnp.zeros(xbuf.shape, xbuf.dtype)
        spare = 2 * ROW_GROUP * SLAB_PITCH
        fill = pltpu.make_async_copy(xbuf.at[1, pl.ds(0, spare), :],
                                     y_hbm.at[pl.ds(n_tok * SLAB_PITCH, spare), :], out_sem.at[0])
        fill.start()
        fill.wait()
        start_in(0, 0)

    @pl.when(jnp.logical_and(t < last, nrows_ref[jnp.minimum(t + 1, last)] > 0))
    def _():
        start_in(t + 1, 1 - slot)

    @pl.when(nrows_ref[t] > 0)
    def _():
        for_rows(t, lambda r, u: in_copy(slot, r, 0).wait())
        xs = xbuf.at[slot]
        for c in range(SLAB_ROWS):
            x2d[:, c * LANES:(c + 1) * LANES] = xs[_slab_rows(c), :].astype(BF16)
        g = xs[_slab_rows(SLAB_ROWS), :]
        x = x2d[...]

        def hidden(w1, w3, gate):
            a = jnp.dot(x, w1[...], preferred_element_type=F32)
            b = jnp.dot(x, w3[...], preferred_element_type=F32)
            return (_silu(a) * b * gate).astype(BF16)

        y = jnp.dot(hidden(w1a, w3a, g[:, 0:1]), w2a[...], preferred_element_type=F32)
        y = y + jnp.dot(hidden(w1b, w3b, g[:, 1:2]), w2b[...], preferred_element_type=F32)
        _store_slabs(ybuf.at[slot], y, jnp.zeros((TM, LANES), F32))
        start_out(t, slot)

    @pl.when(jnp.logical_and(t > 0, nrows_ref[jnp.maximum(t - 1, 0)] > 0))
    def _():
        for_rows(t - 1, lambda r, u: out_copy(1 - slot, r, 0).wait())

    @pl.when(jnp.logical_and(t == last, nrows_ref[t] > 0))
    def _():
        for_rows(t, lambda r, u: out_copy(slot, r, 0).wait())


def _experts(v_slabs, src, e_lo, e_hi, n_rows, w1, w3, w2, layer, n_tok):
    n_tiles = n_rows.shape[0]
    up = lambda sel: pl.BlockSpec((None, None, D_MODEL, D_EXPERT),
                                  lambda t, s, lo, hi, nr: (layer, (lo, hi)[sel][t], 0, 0))
    down = lambda sel: pl.BlockSpec((None, None, D_EXPERT, D_MODEL),
                                    lambda t, s, lo, hi, nr: (layer, (lo, hi)[sel][t], 0, 0))
    buf = pltpu.VMEM((2, TM * SLAB_PITCH, LANES), F32)
    return pl.pallas_call(
        functools.partial(_expert_kernel, n_tok=n_tok),
        grid_spec=pltpu.PrefetchScalarGridSpec(
            num_scalar_prefetch=4,
            grid=(n_tiles,),
            in_specs=[pl.BlockSpec(memory_space=pl.ANY), up(0), up(0), down(0), up(1), up(1), down(1)],
            out_specs=pl.BlockSpec(memory_space=pl.ANY),
            scratch_shapes=[buf, buf, pltpu.VMEM((TM, D_MODEL), BF16),
                            pltpu.SemaphoreType.DMA((2,)), pltpu.SemaphoreType.DMA((2,))],
        ),
        out_shape=jax.ShapeDtypeStruct(((n_tok + 2 * ROW_GROUP) * SLAB_PITCH, LANES), F32),
        compiler_params=_params(),
        name="experts",
    )(src, e_lo, e_hi, n_rows, v_slabs, w1, w3, w2, w1, w3, w2)


def _moe_experts(v_slabs, part, w1, w3, w2, layer):
    n = part.shape[1]
    src, e_lo, e_hi, n_rows = _dispatch_meta(part[0], n + N_PARTS * TM)
    return _experts(v_slabs, src, e_lo, e_hi, n_rows, w1, w3, w2, layer, n)


def _final_kernel(y_ref, h_ref, gate_ref, gain_ref, o_ref, hn_ref):
    for c in range(SLAB_ROWS):
        cols = slice(c * LANES, (c + 1) * LANES)
        hn_ref[:, cols] = h_ref[:, cols] + gate_ref[:, cols] * y_ref[_slab_rows(c), :]
    o_ref[...] = _rms(hn_ref[...], gain_ref[...])


def _final_norm(y_slabs, h, mod, gain, batch):
    lat_tiles = SEQ // TM
    tile = lambda b, t: b * TILES_PER_BATCH + 1 + t
    return pl.pallas_call(
        _final_kernel,
        grid=(batch, lat_tiles),
        in_specs=[
            pl.BlockSpec((TM * SLAB_PITCH, LANES), lambda b, t: (tile(b, t), 0)),
            pl.BlockSpec((TM, D_MODEL), lambda b, t: (tile(b, t), 0)),
            pl.BlockSpec((None, 1, D_MODEL), lambda b, t: (5 * MOD_ROWS + b, 0, 0)),
            pl.BlockSpec((1, D_MODEL), lambda b, t: (0, 0)),
        ],
        out_specs=pl.BlockSpec((TM, D_MODEL), lambda b, t: (b * lat_tiles + t, 0)),
        out_shape=jax.ShapeDtypeStruct((batch * SEQ, D_MODEL), F32),
        scratch_shapes=[pltpu.VMEM((TM, D_MODEL), F32)],
        compiler_params=_params(),
        name="final_norm",
    )(y_slabs, h, mod, gain.reshape(1, D_MODEL))


def _rope_tables(rot_dim, layout):
    n_freq = rot_dim // 4
    inv_freq = 1.0 / (ROPE_THETA ** (jnp.arange(n_freq, dtype=F32) / n_freq))
    t = jnp.arange(SEQ, dtype=jnp.int32)
    row = (t // GRID_W).astype(F32)
    col = (t % GRID_W).astype(F32)
    ang = jnp.concatenate([row[:, None] * inv_freq, col[:, None] * inv_freq], axis=-1)
    cos, sin = jnp.cos(ang), jnp.sin(ang)
    if layout == "padded":
        pad1, pad0 = jnp.ones_like(cos), jnp.zeros_like(sin)
        c = jnp.concatenate([cos, pad1, cos, pad1], axis=-1)
        s = jnp.concatenate([-sin, pad0, sin, pad0], axis=-1)
    else:
        c = jnp.concatenate([cos, cos], axis=-1)
        s = jnp.concatenate([-sin, sin], axis=-1)
    c = jnp.concatenate([jnp.ones((CTX_LEN, LANES), F32), c], axis=0)
    s = jnp.concatenate([jnp.zeros((CTX_LEN, LANES), F32), s], axis=0)
    return c, s


def _pad_rope_cols(w):
    half = MLA_ROPE // 2
    z = jnp.zeros(w.shape[:-1] + (half,), w.dtype)
    return jnp.concatenate([w[..., :half], z, w[..., half:], z], axis=-1)


def _first_proj(h, comb, w, **kw):
    if comb is None:
        return _proj(h, w, **kw), h
    out = _proj(None, w, comb=(comb[0], h, comb[1]), **kw)
    if kw["kinds"][-1][0] == "vt":
        return out[:2], out[2]
    return out[0], out[1]


def _out_proj(o, w_o, h, mod, ffn):
    return _proj(o, w_o.astype(BF16), out_dtype=F32, kinds=[("resid",)] * (D_MODEL // LANES),
                 res=h, mod=mod, mod_gate=2, ffn=ffn)


def _gqa_layer(h, comb, mod, norm_gain, w_qkv, q_gain, k_gain, w_o, tabs, ffn, batch):
    kinds = ([("normrope", 0, 0)] * GQA_HEADS + [("normrope", 1, 1)] * GQA_KV_HEADS
             + [("vt", j) for j in range(GQA_KV_HEADS)])
    (qk, vt), h = _first_proj(h, comb, w_qkv.astype(BF16), out_dtype=BF16, kinds=kinds, prologue="normmod",
                              gain=norm_gain, mod=mod, mod_sc=1, mod_sh=0, tabs=tabs,
                              hgain=jnp.stack([q_gain, k_gain]))
    o = _attention(qk, qk, vt, batch=batch, groups=GQA_KV_HEADS // GQA_KV_PER_STEP,
                   heads_per_step=GQA_GROUP * GQA_KV_PER_STEP, kv_per_step=GQA_KV_PER_STEP, dq=HEAD_DIM,
                   k_blk0=GQA_HEADS // GQA_KV_PER_STEP)
    return _out_proj(o, w_o, h, mod, ffn)


def _mla_layer(h, comb, mod, norm_gain, w_a, q_gain, kv_gain, w_uq, w_ukv, w_o, tabs, ffn, batch):
    n_lat = 2 * MLA_RANK // LANES
    w_a_p = jnp.concatenate([w_a[:, :2 * MLA_RANK], _pad_rope_cols(w_a[:, 2 * MLA_RANK:])], axis=1)
    a, h = _first_proj(h, comb, w_a_p.astype(BF16), out_dtype=F32, kinds=[("plain",)] * n_lat + [("rope", 1)],
                       prologue="normmod", gain=norm_gain, mod=mod, mod_sc=1, mod_sh=0, tabs=tabs,
                       chunk=LANES * (n_lat + 1))
    scale = MLA_QK ** -0.5 * LOG2E
    w_uq_h = w_uq.reshape(MLA_RANK, MLA_HEADS, MLA_QK)
    w_uq_p = jnp.concatenate([w_uq_h[..., :MLA_NOPE], _pad_rope_cols(w_uq_h[..., MLA_NOPE:])], axis=-1)
    q = _proj(a, w_uq_p.reshape(MLA_RANK, MLA_HEADS * 2 * LANES).astype(BF16), out_dtype=BF16,
              kinds=[("scale", scale), ("rope", 0)] * MLA_HEADS, prologue="rms", x_col_block=0, gain=q_gain, tabs=tabs)
    w_ukv_h = w_ukv.reshape(MLA_RANK, MLA_HEADS, 2 * HEAD_DIM)
    w_ukv_p = jnp.concatenate([w_ukv_h[..., :MLA_NOPE].reshape(MLA_RANK, -1),
                               w_ukv_h[..., MLA_NOPE:].reshape(MLA_RANK, -1)], axis=1)
    k_nope, vt = _proj(a, w_ukv_p.astype(BF16), out_dtype=BF16,
                       kinds=[("plain",)] * MLA_HEADS + [("vt", j) for j in range(MLA_HEADS)],
                       prologue="rms", x_col_block=1, gain=kv_gain)
    o = _attention(q, k_nope, vt, batch=batch, groups=MLA_HEADS // MLA_HEADS_PER_STEP,
                   heads_per_step=MLA_HEADS_PER_STEP, kv_per_step=MLA_HEADS_PER_STEP, dq=2 * LANES, k_blk0=0,
                   kpe_arr=a, kpe_blk=n_lat)
    return _out_proj(o, w_o, h, mod, ffn)


def _na_layer(h, comb, mod, norm_gain, w_qkv, rpb, w_o, ffn, batch):
    scale = HEAD_DIM ** -0.5 * LOG2E
    kinds = [("scale", scale)] * NA_HEADS + [("plain",)] * NA_HEADS + [("vt", j) for j in range(NA_HEADS)]
    (qk, vt), h = _first_proj(h, comb, w_qkv.astype(BF16), out_dtype=BF16, kinds=kinds, prologue="normmod",
                              gain=norm_gain, mod=mod, mod_sc=1, mod_sh=0)
    o = _na_attention(qk, vt, rpb, batch)
    return _out_proj(o, w_o, h, mod, ffn)


def kernel(x, c, ctx, c_ctx, mod_w, mod_b, norm_mix, norm_ffn, gqa_wqkv, gqa_q_gain, gqa_k_gain, gqa_wo,
           mla_wa, mla_q_gain, mla_kv_gain, mla_wuq, mla_wukv, mla_wo, na_wqkv, na_rpb, na_wo,
           router_w, router_bias, moe_w1, moe_w3, moe_w2, final_gain):
    batch = x.shape[0]
    assert batch <= CTX_MOD_ROW and x.shape[1:] == (SEQ, D_MODEL) and ctx.shape[1:] == (CTX_LEN, D_MODEL)
    h = jnp.concatenate([ctx, x], axis=1).reshape(batch * TOK, D_MODEL)
    cond = jnp.zeros((MOD_ROWS, D_MODEL), F32).at[:batch].set(c).at[CTX_MOD_ROW].set(c_ctx)
    mods = _modulation(cond, mod_w, mod_b)
    mods = mods.reshape(DEPTH, MOD_ROWS, 6, D_MODEL).transpose(0, 2, 1, 3).reshape(DEPTH, 6 * MOD_ROWS, 1, D_MODEL)

    c_full, s_full = _rope_tables(HEAD_DIM, "full")
    gqa_scale = HEAD_DIM ** -0.5 * LOG2E
    gqa_tabs = (jnp.concatenate([c_full * gqa_scale, c_full], axis=1),
                jnp.concatenate([s_full * gqa_scale, s_full], axis=1))
    c_pad, s_pad = _rope_tables(MLA_ROPE, "padded")
    mla_scale = MLA_QK ** -0.5 * LOG2E
    mla_tabs = (jnp.concatenate([c_pad * mla_scale, c_pad], axis=1),
                jnp.concatenate([s_pad * mla_scale, s_pad], axis=1))
    rw_hi = router_w.astype(BF16)
    rw_lo = (router_w - rw_hi.astype(F32)).astype(BF16)
    lane_pad = ((0, 0), (0, LANES - N_EXPERTS))
    router_wt = jnp.concatenate([jnp.pad(rw_hi, lane_pad), jnp.pad(rw_lo, lane_pad)], axis=1)
    w1_b, w3_b, w2_b = moe_w1.astype(BF16), moe_w3.astype(BF16), moe_w2.astype(BF16)

    comb = None
    for i in range(DEPTH):
        mod = mods[i]
        ffn = (norm_ffn[i], router_wt, router_bias)
        kind, j = i % N_MIXERS, i // N_MIXERS
        if kind == 0:
            out = _gqa_layer(h, comb, mod, norm_mix[i], gqa_wqkv[j], gqa_q_gain[j], gqa_k_gain[j], gqa_wo[j],
                             gqa_tabs, ffn, batch)
        elif kind == 1:
            out = _mla_layer(h, comb, mod, norm_mix[i], mla_wa[j], mla_q_gain[j], mla_kv_gain[j], mla_wuq[j],
                             mla_wukv[j], mla_wo[j], mla_tabs, ffn, batch)
        else:
            out = _na_layer(h, comb, mod, norm_mix[i], na_wqkv[j], na_rpb[j], na_wo[j], ffn, batch)
        h, v_slabs, part = out
        comb = (_moe_experts(v_slabs, part, w1_b, w3_b, w2_b, i), mod)
    return _final_norm(comb[0], h, comb[1], final_gain, batch).reshape(batch, SEQ, D_MODEL)
```

```python
import functools

import numpy as np
import jax
import jax.numpy as jnp
from jax import lax
from jax.experimental import pallas as pl
from jax.experimental.pallas import tpu as pltpu

F32 = jnp.float32
BF16 = jnp.bfloat16

D_MODEL = 2048
SEQ = 2048
GRID_W = 64
GRID_ROWS = SEQ // GRID_W
CTX_LEN = 256
TOK = CTX_LEN + SEQ
DEPTH = 4
N_MIXERS = 3
EPS = 1e-6
ROPE_THETA = 10000.0

GQA_HEADS = 16
GQA_KV_HEADS = 4
HEAD_DIM = 128
GQA_GROUP = GQA_HEADS // GQA_KV_HEADS

MLA_HEADS = 16
MLA_RANK = 512
MLA_NOPE = 128
MLA_ROPE = 64
MLA_QK = MLA_NOPE + MLA_ROPE

NA_HEADS = 16
NA_WIN_ROWS = 8
NA_WIN_COLS = 16
NA_Q_ROWS = 4
NA_K_ROWS = 12
NA_KEYS = NA_K_ROWS * GRID_W
NA_HEADS_PER_STEP = 8
MLA_HEADS_PER_STEP = 8
GQA_KV_PER_STEP = 2
KEY_CHUNK = 256

N_EXPERTS = 16
N_GROUPS = 4
GROUP_SIZE = N_EXPERTS // N_GROUPS
N_PAIRS = 6
N_PARTS = N_GROUPS * N_PAIRS
D_EXPERT = 512
PAIR_LO = (0, 0, 0, 1, 1, 2)
PAIR_HI = (1, 2, 3, 2, 3, 3)

LANES = 128
TM = 256
TILES_PER_BATCH = TOK // TM
SLAB_ROWS = D_MODEL // LANES
SLAB_PITCH = SLAB_ROWS + 1
MOD_ROWS = 16
CTX_MOD_ROW = 8
NEG_BIG = -1e30
LOG2E = 1.4426950408889634
VMEM_LIMIT = 56 * 1024 * 1024


def _params():
    return pltpu.CompilerParams(vmem_limit_bytes=VMEM_LIMIT)


def _mod_row(i):
    return jnp.where(i % TILES_PER_BATCH == 0, CTX_MOD_ROW, i // TILES_PER_BATCH)


def _silu(x):
    return x / (1.0 + jnp.exp(-x))


def _mod_kernel(cond_ref, w_ref, b_ref, o_ref):
    s = _silu(cond_ref[...]).astype(BF16)
    o_ref[...] = jnp.dot(s, w_ref[...].astype(BF16), preferred_element_type=F32) + b_ref[...]


def _modulation(cond, mod_w, mod_b):
    tn = 1024
    n_out = mod_w.shape[-1]
    return pl.pallas_call(
        _mod_kernel,
        grid=(DEPTH, n_out // tn),
        in_specs=[
            pl.BlockSpec((MOD_ROWS, D_MODEL), lambda l, j: (0, 0)),
            pl.BlockSpec((None, D_MODEL, tn), lambda l, j: (l, 0, j)),
            pl.BlockSpec((None, 1, tn), lambda l, j: (l, 0, j)),
        ],
        out_specs=pl.BlockSpec((None, MOD_ROWS, tn), lambda l, j: (l, 0, j)),
        out_shape=jax.ShapeDtypeStruct((DEPTH, MOD_ROWS, n_out), F32),
        compiler_params=_params(),
        name="modulation",
    )(cond, mod_w, mod_b.reshape(DEPTH, 1, n_out))


def _rms(x, gain):
    return x * lax.rsqrt(jnp.mean(x * x, axis=-1, keepdims=True) + EPS) * gain


def _rope(x, c, s):
    return x * c + pltpu.roll(x, LANES // 2, 1) * s


def _runs(kinds):
    out, start = [], 0
    for k in range(1, len(kinds) + 1):
        if k == len(kinds) or kinds[k] != kinds[start] or kinds[start][0] in ("rope", "normrope", "vt"):
            out.append((start, k, kinds[start]))
            start = k
    return out


def _proj_kernel(*refs, prologue, kinds, chunk, has_tab, has_hgain, resid, has_vt, comb, first, ffn):
    it = iter(refs)
    if comb:
        y_ref, hprev_ref, cgate_ref = next(it), next(it), next(it)
    elif first:
        lat_ref, ctx_ref = next(it), next(it)
    else:
        x_ref = next(it)
    gain_ref = sc_ref = sh_ref = None
    if prologue == "normmod":
        gain_ref, sc_ref, sh_ref = next(it), next(it), next(it)
    elif prologue == "rms":
        gain_ref = next(it)
    w_ref = next(it)
    ctab_ref = stab_ref = hgain_ref = res_ref = gate_ref = None
    if has_tab:
        ctab_ref, stab_ref = next(it), next(it)
    if has_hgain:
        hgain_ref = next(it)
    if resid:
        res_ref, gate_ref = next(it), next(it)
    if ffn:
        fgain_ref, fsc_ref, fsh_ref, rw_ref, rb_ref = (next(it) for _ in range(5))
    o_ref = next(it)
    vt_ref = next(it) if has_vt else None
    hnew_ref = next(it) if (comb or first) else None
    if ffn:
        v_ref, part_ref = next(it), next(it)

    if first:
        is_ctx = pl.program_id(0) % TILES_PER_BATCH == 0
        hnew_ref[...] = jnp.where(is_ctx, ctx_ref[...], lat_ref[...])
        x_ref = hnew_ref
    if comb:
        for c in range(SLAB_ROWS):
            cols = slice(c * LANES, (c + 1) * LANES)
            hnew_ref[:, cols] = hprev_ref[:, cols] + cgate_ref[:, cols] * y_ref[_slab_rows(c), :]
        x_ref = hnew_ref

    if prologue == "none":
        u = x_ref[...]
    else:
        y = _rms(x_ref[...], gain_ref[...])
        if prologue == "normmod":
            y = y * (1.0 + sc_ref[...]) + sh_ref[...]
        u = y.astype(BF16)

    for c0 in range(0, len(kinds) * LANES, chunk):
        acc = jnp.dot(u, w_ref[:, c0:c0 + chunk], preferred_element_type=F32)
        b0 = c0 // LANES
        for lo, hi, kind in _runs(kinds[b0:b0 + chunk // LANES]):
            a = acc[:, lo * LANES:hi * LANES]
            cols = slice(c0 + lo * LANES, c0 + hi * LANES)
            if kind[0] == "scale":
                a = a * kind[1]
            elif kind[0] == "rope":
                t = kind[1]
                a = _rope(a, ctab_ref[:, t * LANES:(t + 1) * LANES], stab_ref[:, t * LANES:(t + 1) * LANES])
            elif kind[0] == "normrope":
                t, g = kind[1], kind[2]
                a = _rms(a, hgain_ref[g:g + 1, :])
                a = _rope(a, ctab_ref[:, t * LANES:(t + 1) * LANES], stab_ref[:, t * LANES:(t + 1) * LANES])
            elif kind[0] == "resid":
                a = res_ref[:, cols] + gate_ref[:, cols] * a
            elif kind[0] == "vt":
                vt_ref[kind[1] * LANES:(kind[1] + 1) * LANES, :] = a.T.astype(vt_ref.dtype)
                continue
            o_ref[:, cols] = a.astype(o_ref.dtype)

    if ffn:
        _ffn_prep_tile(o_ref[...], fgain_ref, fsc_ref, fsh_ref, rw_ref, rb_ref, v_ref, part_ref)


def _proj(x, w, *, out_dtype, kinds, prologue="none", x_col_block=0, gain=None, mod=None,
          mod_sc=None, mod_sh=None, tabs=None, hgain=None, res=None, mod_gate=None, chunk=512,
          comb=None, first=None, ffn=None):
    k, n_all = w.shape
    if first:
        n = first[0].shape[0] * TOK
    else:
        n = comb[1].shape[0] if comb else x.shape[0]
    n_vt = sum(kind[0] == "vt" for kind in kinds)
    n_out = n_all - n_vt * LANES
    assert n % TM == 0 and n_all % chunk == 0 and len(kinds) == n_all // LANES
    assert all(kind[0] == "vt" for kind in kinds[n_out // LANES:])
    if comb:
        y_slabs, h_prev, mod_prev = comb
        args = [y_slabs, h_prev, mod_prev]
        specs = [pl.BlockSpec((TM * SLAB_PITCH, LANES), lambda i: (i, 0)),
                 pl.BlockSpec((TM, k), lambda i: (i, 0)),
                 pl.BlockSpec((None, 1, k), lambda i: (5 * MOD_ROWS + _mod_row(i), 0, 0))]
    elif first:
        args = list(first)
        specs = [pl.BlockSpec((None, TM, k),
                              lambda i: (i // TILES_PER_BATCH, jnp.maximum(i % TILES_PER_BATCH - 1, 0), 0)),
                 pl.BlockSpec((None, TM, k), lambda i: (i // TILES_PER_BATCH, 0, 0))]
    else:
        args = [x]
        specs = [pl.BlockSpec((TM, k), lambda i: (i, x_col_block))]
    if prologue in ("normmod", "rms"):
        args.append(gain.reshape(1, k))
        specs.append(pl.BlockSpec((1, k), lambda i: (0, 0)))
    if prologue == "normmod":
        for comp in (mod_sc, mod_sh):
            args.append(mod)
            specs.append(pl.BlockSpec((None, 1, k), lambda i, comp=comp: (comp * MOD_ROWS + _mod_row(i), 0, 0)))
    args.append(w)
    specs.append(pl.BlockSpec((k, n_all), lambda i: (0, 0), pipeline_mode=pl.Buffered(1)))
    if tabs is not None:
        for t in tabs:
            args.append(t)
            specs.append(pl.BlockSpec((TM, t.shape[1]), lambda i: (i % TILES_PER_BATCH, 0)))
    if hgain is not None:
        args.append(hgain)
        specs.append(pl.BlockSpec(hgain.shape, lambda i: (0, 0)))
    if res is not None:
        args.append(res)
        specs.append(pl.BlockSpec((TM, n_out), lambda i: (i, 0)))
        args.append(mod)
        specs.append(pl.BlockSpec((None, 1, n_out), lambda i: (mod_gate * MOD_ROWS + _mod_row(i), 0, 0)))
    if ffn:
        f_gain, router_wt, router_bias = ffn
        assert res is not None and n_out == D_MODEL
        args += [f_gain.reshape(1, D_MODEL), mod, mod, router_wt, router_bias.reshape(N_EXPERTS, 1)]
        specs += [
            pl.BlockSpec((1, D_MODEL), lambda i: (0, 0)),
            pl.BlockSpec((None, 1, D_MODEL), lambda i: (4 * MOD_ROWS + _mod_row(i), 0, 0)),
            pl.BlockSpec((None, 1, D_MODEL), lambda i: (3 * MOD_ROWS + _mod_row(i), 0, 0)),
            pl.BlockSpec((D_MODEL, 2 * LANES), lambda i: (0, 0)),
            pl.BlockSpec((N_EXPERTS, 1), lambda i: (0, 0)),
        ]
    kern = functools.partial(_proj_kernel, prologue=prologue, kinds=tuple(kinds), chunk=chunk,
                             has_tab=tabs is not None, has_hgain=hgain is not None, resid=res is not None,
                             has_vt=n_vt > 0, comb=comb is not None, first=first is not None,
                             ffn=ffn is not None)
    out_specs = [pl.BlockSpec((TM, n_out), lambda i: (i, 0))]
    out_shape = [jax.ShapeDtypeStruct((n, n_out), out_dtype)]
    if n_vt:
        out_specs.append(pl.BlockSpec((n_vt * LANES, TM), lambda i: (0, i)))
        out_shape.append(jax.ShapeDtypeStruct((n_vt * LANES, n), out_dtype))
    if comb or first:
        out_specs.append(pl.BlockSpec((TM, k), lambda i: (i, 0)))
        out_shape.append(jax.ShapeDtypeStruct((n, k), F32))
    if ffn:
        out_specs += [pl.BlockSpec((TM * SLAB_PITCH, LANES), lambda i: (i, 0)), pl.BlockSpec((8, TM), lambda i: (0, i))]
        out_shape += [jax.ShapeDtypeStruct((n * SLAB_PITCH, LANES), F32), jax.ShapeDtypeStruct((8, n), jnp.int32)]
    out = pl.pallas_call(
        kern,
        grid=(n // TM,),
        in_specs=specs,
        out_specs=out_specs,
        out_shape=out_shape,
        compiler_params=_params(),
        name="proj",
    )(*args)
    return out if len(out) > 1 else out[0]


def _kq(k, q):
    return lax.dot_general(k, q, (((1,), (1,)), ((), ())), preferred_element_type=F32)


def _pipelined_heads(n_heads, n_chunks, q_of, k_of, vt_of, bias_of, s_ref, store):
    def scores(h, c, slot):
        s = _kq(k_of(h, c), q_of(h))
        b = bias_of(h, c)
        s_ref[slot, c] = s if b is None else s + b

    for c in range(n_chunks):
        scores(0, c, 0)
    for h in range(n_heads):
        slot = h % 2
        m = functools.reduce(jnp.maximum,
                             [jnp.max(s_ref[slot, c], axis=0, keepdims=True) for c in range(n_chunks)])
        num, den = None, None
        for c in range(n_chunks):
            if h + 1 < n_heads:
                scores(h + 1, c, 1 - slot)
            p = jnp.exp2(s_ref[slot, c] - m)
            l = jnp.sum(p, axis=0, keepdims=True)
            o = jnp.dot(vt_of(h, c), p.astype(BF16), preferred_element_type=F32)
            num = o if num is None else num + o
            den = l if den is None else den + l
        store(h, (num / den).T)


def _attn_kernel(*refs, heads, kv_heads, dq, has_kpe):
    if has_kpe:
        q_ref, k_ref, kpe_ref, vt_ref, o_ref, s_ref, keys_ref = refs
    else:
        q_ref, k_ref, vt_ref, o_ref, s_ref = refs
        keys_ref = None
    qt = pl.program_id(2)

    if has_kpe:
        @pl.when(qt == 0)
        def _():
            kpe = kpe_ref[...].astype(BF16)
            for h in range(heads):
                keys_ref[h, :, 0:HEAD_DIM] = k_ref[:, h * HEAD_DIM:(h + 1) * HEAD_DIM]
                keys_ref[h, :, HEAD_DIM:] = kpe

    def kv_of(h):
        return h // (heads // kv_heads)

    def q_of(h):
        return q_ref[:, h * dq:(h + 1) * dq]

    def k_of(h, c):
        rows = slice(c * KEY_CHUNK, (c + 1) * KEY_CHUNK)
        if has_kpe:
            return keys_ref[h, rows, :]
        return k_ref[rows, kv_of(h) * HEAD_DIM:(kv_of(h) + 1) * HEAD_DIM]

    def vt_of(h, c):
        return vt_ref[kv_of(h) * HEAD_DIM:(kv_of(h) + 1) * HEAD_DIM, c * KEY_CHUNK:(c + 1) * KEY_CHUNK]

    def store(h, o):
        o_ref[:, h * HEAD_DIM:(h + 1) * HEAD_DIM] = o.astype(o_ref.dtype)

    @pl.when(qt == 0)
    def _():
        _pipelined_heads(heads, CTX_LEN // KEY_CHUNK, q_of, k_of, vt_of, lambda h, c: None, s_ref, store)

    @pl.when(qt > 0)
    def _():
        _pipelined_heads(heads, TOK // KEY_CHUNK, q_of, k_of, vt_of, lambda h, c: None, s_ref, store)


def _attention(q_arr, k_arr, vt_arr, *, batch, groups, heads_per_step, kv_per_step, dq, k_blk0,
               kpe_arr=None, kpe_blk=0):
    shared_kv = kpe_arr is None
    kw = kv_per_step * HEAD_DIM
    ow = heads_per_step * HEAD_DIM
    args = [q_arr, k_arr]
    specs = [
        pl.BlockSpec((TM, heads_per_step * dq), lambda b, g, t: (b * TILES_PER_BATCH + t, g)),
        pl.BlockSpec((TOK, kw), lambda b, g, t: (b, k_blk0 + g)),
    ]
    scratch = [pltpu.VMEM((2, TOK // KEY_CHUNK, KEY_CHUNK, TM), F32)]
    if not shared_kv:
        args.append(kpe_arr)
        specs.append(pl.BlockSpec((TOK, LANES), lambda b, g, t: (b, kpe_blk)))
        scratch.append(pltpu.VMEM((heads_per_step, TOK, HEAD_DIM + LANES), BF16))
    args.append(vt_arr)
    specs.append(pl.BlockSpec((kw, TOK), lambda b, g, t: (g, b)))
    kern = functools.partial(_attn_kernel, heads=heads_per_step, kv_heads=kv_per_step, dq=dq,
                             has_kpe=not shared_kv)
    return pl.pallas_call(
        kern,
        grid=(batch, groups, TILES_PER_BATCH),
        in_specs=specs,
        out_specs=pl.BlockSpec((TM, ow), lambda b, g, t: (b * TILES_PER_BATCH + t, g)),
        out_shape=jax.ShapeDtypeStruct((batch * TOK, groups * ow), BF16),
        scratch_shapes=scratch,
        compiler_params=_params(),
        name="attention",
    )(*args)


def _na_kernel(q_ref, k_ref, vt_ref, tbl_ref, o_ref, s_ref):
    t = pl.program_id(2)
    n_lat_tiles = GRID_ROWS // NA_Q_ROWS
    k_row0 = jnp.clip(NA_Q_ROWS * t - NA_WIN_ROWS // 2, 0, GRID_ROWS - NA_K_ROWS)
    start = pl.multiple_of(CTX_LEN + k_row0 * GRID_W, KEY_CHUNK)
    which = jnp.where(t == 0, 0, jnp.where(t == n_lat_tiles - 1, 2, 1))

    def key_rows(c):
        return pl.ds(0, KEY_CHUNK) if c == 0 else pl.ds(start + (c - 1) * KEY_CHUNK, KEY_CHUNK)

    def q_of(h):
        return q_ref[:, h * HEAD_DIM:(h + 1) * HEAD_DIM]

    def k_of(h, c):
        return k_ref[key_rows(c), h * HEAD_DIM:(h + 1) * HEAD_DIM]

    def vt_of(h, c):
        return vt_ref[h * HEAD_DIM:(h + 1) * HEAD_DIM, key_rows(c)]

    def bias_of(h, c):
        return None if c == 0 else tbl_ref[h, which, (c - 1) * KEY_CHUNK:c * KEY_CHUNK, :]

    def store(h, o):
        o_ref[:, h * HEAD_DIM:(h + 1) * HEAD_DIM] = o.astype(o_ref.dtype)

    @pl.when(t == n_lat_tiles)
    def _():
        _pipelined_heads(NA_HEADS_PER_STEP, 1, q_of, k_of, vt_of, bias_of, s_ref, store)

    @pl.when(t < n_lat_tiles)
    def _():
        _pipelined_heads(NA_HEADS_PER_STEP, 1 + NA_KEYS // KEY_CHUNK, q_of, k_of, vt_of, bias_of, s_ref, store)


def _na_bias_table(rpb):
    cols = np.arange(GRID_W)
    c0 = np.clip(cols - NA_WIN_COLS // 2, 0, GRID_W - NA_WIN_COLS)
    col_ok = (cols[:, None] >= c0[None, :]) & (cols[:, None] < c0[None, :] + NA_WIN_COLS)
    cb = np.clip(cols[:, None] - cols[None, :] + NA_WIN_COLS - 1, 0, 2 * NA_WIN_COLS - 2)
    onehot = (cb[None] == np.arange(2 * NA_WIN_COLS - 1)[:, None, None]).astype(np.float32)
    planes = jnp.einsum("hdc,ckq->hdkq", rpb.astype(F32), onehot, precision=lax.Precision.HIGHEST)
    planes = jnp.where(col_ok[None, None], planes * LOG2E, NEG_BIG)
    n_rel = 2 * NA_WIN_ROWS - 1
    planes = jnp.concatenate([planes, jnp.full((NA_HEADS, 1, GRID_W, GRID_W), NEG_BIG, F32)], axis=1)
    n_lat_tiles = GRID_ROWS // NA_Q_ROWS
    which = np.full((3, NA_K_ROWS, NA_Q_ROWS), n_rel, np.int32)
    for kind, t in enumerate((0, 1, n_lat_tiles - 1)):
        k_row0 = int(np.clip(NA_Q_ROWS * t - NA_WIN_ROWS // 2, 0, GRID_ROWS - NA_K_ROWS))
        for kk in range(NA_K_ROWS):
            for i in range(NA_Q_ROWS):
                qr, kr = NA_Q_ROWS * t + i, k_row0 + kk
                r0 = int(np.clip(qr - NA_WIN_ROWS // 2, 0, GRID_ROWS - NA_WIN_ROWS))
                if r0 <= kr < r0 + NA_WIN_ROWS:
                    which[kind, kk, i] = kr - qr + NA_WIN_ROWS - 1
    tbl = jnp.concatenate([planes[:, which[:, :, i]] for i in range(NA_Q_ROWS)], axis=-1)
    return tbl.reshape(NA_HEADS, 3, NA_KEYS, TM)


def _na_attention(qk, vt, rpb, batch):
    tbl = _na_bias_table(rpb)
    n_lat_tiles = GRID_ROWS // NA_Q_ROWS

    def q_tile(b, t):
        return b * TILES_PER_BATCH + jnp.where(t == n_lat_tiles, 0, t + 1)

    gw = NA_HEADS_PER_STEP * HEAD_DIM
    n_groups = NA_HEADS // NA_HEADS_PER_STEP
    return pl.pallas_call(
        _na_kernel,
        grid=(n_groups, batch, n_lat_tiles + 1),
        in_specs=[
            pl.BlockSpec((TM, gw), lambda g, b, t: (q_tile(b, t), g)),
            pl.BlockSpec((TOK, gw), lambda g, b, t: (b, n_groups + g)),
            pl.BlockSpec((gw, TOK), lambda g, b, t: (g, b)),
            pl.BlockSpec((NA_HEADS_PER_STEP, 3, NA_KEYS, TM), lambda g, b, t: (g, 0, 0, 0),
                         pipeline_mode=pl.Buffered(1)),
        ],
        out_specs=pl.BlockSpec((TM, gw), lambda g, b, t: (q_tile(b, t), g)),
        out_shape=jax.ShapeDtypeStruct((batch * TOK, NA_HEADS * HEAD_DIM), BF16),
        scratch_shapes=[pltpu.VMEM((2, 1 + NA_KEYS // KEY_CHUNK, KEY_CHUNK, TM), F32)],
        compiler_params=_params(),
        name="na_attention",
    )(qk, qk, vt, tbl)


def _route(logits, bias):
    scores = 1.0 / (1.0 + jnp.exp(-logits))
    sel = scores + bias
    srow = [sel[e:e + 1, :] for e in range(N_EXPERTS)]
    crow = [scores[e:e + 1, :] for e in range(N_EXPERTS)]
    gscore = []
    for g in range(N_GROUPS):
        a, b, c, d = srow[GROUP_SIZE * g:GROUP_SIZE * (g + 1)]
        gscore.append(jnp.maximum(jnp.maximum(jnp.maximum(a + b, a + c), jnp.maximum(a + d, b + c)),
                                  jnp.maximum(b + d, c + d)))
    best, gidx = gscore[0], jnp.zeros(gscore[0].shape, jnp.int32)
    for g in range(1, N_GROUPS):
        better = gscore[g] > best
        gidx = jnp.where(better, g, gidx)
        best = jnp.where(better, gscore[g], best)
    s, c = [], []
    for j in range(GROUP_SIZE):
        sj, cj = srow[j], crow[j]
        for g in range(1, N_GROUPS):
            sj = jnp.where(gidx == g, srow[GROUP_SIZE * g + j], sj)
            cj = jnp.where(gidx == g, crow[GROUP_SIZE * g + j], cj)
        s.append(sj)
        c.append(cj)
    chosen = []
    for j in range(GROUP_SIZE):
        rank = jnp.zeros(s[j].shape, jnp.int32)
        for k in range(GROUP_SIZE):
            if k < j:
                rank = rank + jnp.where(s[k] >= s[j], 1, 0)
            elif k > j:
                rank = rank + jnp.where(s[k] > s[j], 1, 0)
        chosen.append(rank < 2)
    w = [jnp.where(chosen[j], c[j], 0.0) for j in range(GROUP_SIZE)]
    denom = (w[0] + w[1]) + (w[2] + w[3])
    gate = [w[j] / denom for j in range(GROUP_SIZE)]
    lo = jnp.where(chosen[0], 0, jnp.where(chosen[1], 1, 2))
    hi = jnp.where(chosen[3], 3, jnp.where(chosen[2], 2, 1))
    g_lo = jnp.where(chosen[0], gate[0], jnp.where(chosen[1], gate[1], gate[2]))
    g_hi = jnp.where(chosen[3], gate[3], jnp.where(chosen[2], gate[2], gate[1]))
    pair = jnp.where(lo == 0, hi - 1, jnp.where(lo == 1, hi + 1, N_PAIRS - 1))
    return gidx * N_PAIRS + pair, g_lo, g_hi


def _slab_rows(c):
    return pl.ds(c, TM, stride=SLAB_PITCH)


def _store_slabs(ref, val, extra):
    for c in range(SLAB_ROWS):
        ref[_slab_rows(c), :] = val[:, c * LANES:(c + 1) * LANES]
    ref[_slab_rows(SLAB_ROWS), :] = extra


def _ffn_prep_tile(h, gain_ref, sc_ref, sh_ref, rw_ref, rb_ref, v_ref, part_ref):
    v = _rms(h, gain_ref[...]) * (1.0 + sc_ref[...]) + sh_ref[...]
    v_hi = v.astype(BF16)
    v_lo = (v - v_hi.astype(F32)).astype(BF16)
    a = jnp.dot(v_hi, rw_ref[...], preferred_element_type=F32)
    b = jnp.dot(v_lo, rw_ref[:, 0:LANES], preferred_element_type=F32)
    logits = (a[:, 0:LANES] + a[:, LANES:] + b).T[0:N_EXPERTS, :]
    part, g_lo, g_hi = _route(logits, rb_ref[...])
    part_ref[...] = jnp.zeros(part_ref.shape, jnp.int32)
    part_ref[0:1, :] = part
    row = lax.broadcasted_iota(jnp.int32, (LANES, TM), 0)
    gates_t = jnp.where(row == 0, g_lo, jnp.where(row == 1, g_hi, 0.0))
    _store_slabs(v_ref, v, gates_t.T)


def _dispatch_meta(part, n_slots):
    n = part.shape[0]
    onehot = (part[:, None] == jnp.arange(N_PARTS, dtype=jnp.int32)[None, :]).astype(jnp.int32)
    csum = jnp.cumsum(onehot, axis=0)
    rank = jnp.take_along_axis(csum, part[:, None], axis=1)[:, 0] - 1
    counts = csum[-1]
    padded = ((counts + TM - 1) // TM) * TM
    ends = jnp.cumsum(padded)
    starts = ends - padded
    pos = (starts[part] + rank).astype(jnp.int32)
    src = jnp.zeros((n_slots,), jnp.int32).at[pos].set(jnp.arange(n, dtype=jnp.int32))
    tile_start = jnp.arange(n_slots // TM, dtype=jnp.int32) * TM
    last_start = jnp.minimum(tile_start, ends[-1] - TM)
    tile_part = jnp.sum((ends[None, :] <= last_start[:, None]).astype(jnp.int32), axis=1)
    n_rows = jnp.clip(counts[tile_part] - (tile_start - starts[tile_part]), 0, TM).astype(jnp.int32)
    group, pair = tile_part // N_PAIRS, tile_part % N_PAIRS
    e_lo = group * GROUP_SIZE + jnp.asarray(PAIR_LO, jnp.int32)[pair]
    e_hi = group * GROUP_SIZE + jnp.asarray(PAIR_HI, jnp.int32)[pair]
    return src, e_lo, e_hi, n_rows


def _expert_kernel(src_ref, elo_ref, ehi_ref, nrows_ref, v_hbm, w1a, w3a, w2a, w1b, w3b, w2b, y_hbm,
                   xbuf, ybuf, x2d, in_sem, out_sem, *, n_tok):
    t = pl.program_id(0)
    slot = t % 2

    def in_copy(slot_, r, tok):
        return pltpu.make_async_copy(v_hbm.at[pl.ds(tok * SLAB_PITCH, SLAB_PITCH), :],
                                     xbuf.at[slot_, pl.ds(r * SLAB_PITCH, SLAB_PITCH), :], in_sem.at[slot_])

    def out_copy(slot_, r, tok):
        return pltpu.make_async_copy(ybuf.at[slot_, pl.ds(r * SLAB_PITCH, SLAB_PITCH), :],
                                     y_hbm.at[pl.ds(tok * SLAB_PITCH, SLAB_PITCH), :], out_sem.at[slot_])

    def start_in(tile, slot_):
        for r in range(TM):
            in_copy(slot_, r, src_ref[tile * TM + r]).start(priority=r % 2)

    def start_out(tile, n, slot_):
        for r in range(TM):
            tok = jnp.where(r < n, src_ref[tile * TM + r], n_tok + slot_ * TM + r)
            out_copy(slot_, r, tok).start(priority=r % 2)

    def wait_out(slot_):
        for r in range(TM):
            out_copy(slot_, r, 0).wait()

    prev = jnp.maximum(t - 1, 0)
    n_prev = jnp.where(t > 0, nrows_ref[prev], 0)

    @pl.when(t == 0)
    def _():
        ybuf[...] = jnp.zeros(ybuf.shape, ybuf.dtype)
        fill = pltpu.make_async_copy(ybuf.at[0], y_hbm.at[pl.ds(n_tok * SLAB_PITCH, TM * SLAB_PITCH), :],
                                     out_sem.at[0])
        fill.start()
        fill.wait()
        start_in(0, 0)

    @pl.when(jnp.logical_or(t == 0, n_prev > 0))
    def _():
        for r in range(TM):
            in_copy(slot, r, 0).wait()

    @pl.when(nrows_ref[t] > 0)
    def _():
        start_in(t + 1, 1 - slot)
        start_out(prev, n_prev, 1 - slot)
        xs = xbuf.at[slot]
        for c in range(SLAB_ROWS):
            x2d[:, c * LANES:(c + 1) * LANES] = xs[_slab_rows(c), :].astype(BF16)
        g = xs[_slab_rows(SLAB_ROWS), :]
        x = x2d[...]

        def hidden(w1, w3, gate):
            a = jnp.dot(x, w1[...], preferred_element_type=F32)
            b = jnp.dot(x, w3[...], preferred_element_type=F32)
            return (_silu(a) * b * gate).astype(BF16)

        y = jnp.dot(hidden(w1a, w3a, g[:, 0:1]), w2a[...], preferred_element_type=F32)
        y = y + jnp.dot(hidden(w1b, w3b, g[:, 1:2]), w2b[...], preferred_element_type=F32)
        _store_slabs(ybuf.at[slot], y, jnp.zeros((TM, LANES), F32))
        wait_out(1 - slot)

    @pl.when(jnp.logical_and(nrows_ref[t] == 0, n_prev > 0))
    def _():
        start_out(prev, n_prev, 1 - slot)
        wait_out(1 - slot)


def _experts(v_slabs, src, e_lo, e_hi, n_rows, w1, w3, w2, layer, n_tok):
    n_tiles = n_rows.shape[0]
    up = lambda sel: pl.BlockSpec((None, None, D_MODEL, D_EXPERT),
                                  lambda t, s, lo, hi, nr: (layer, (lo, hi)[sel][t], 0, 0))
    down = lambda sel: pl.BlockSpec((None, None, D_EXPERT, D_MODEL),
                                    lambda t, s, lo, hi, nr: (layer, (lo, hi)[sel][t], 0, 0))
    buf = pltpu.VMEM((2, TM * SLAB_PITCH, LANES), F32)
    return pl.pallas_call(
        functools.partial(_expert_kernel, n_tok=n_tok),
        grid_spec=pltpu.PrefetchScalarGridSpec(
            num_scalar_prefetch=4,
            grid=(n_tiles,),
            in_specs=[pl.BlockSpec(memory_space=pl.ANY), up(0), up(0), down(0), up(1), up(1), down(1)],
            out_specs=pl.BlockSpec(memory_space=pl.ANY),
            scratch_shapes=[buf, buf, pltpu.VMEM((TM, D_MODEL), BF16),
                            pltpu.SemaphoreType.DMA((2,)), pltpu.SemaphoreType.DMA((2,))],
        ),
        out_shape=jax.ShapeDtypeStruct(((n_tok + 2 * TM) * SLAB_PITCH, LANES), F32),
        compiler_params=_params(),
        name="experts",
    )(src, e_lo, e_hi, n_rows, v_slabs, w1, w3, w2, w1, w3, w2)


def _moe_experts(v_slabs, part, w1, w3, w2, layer):
    n = part.shape[1]
    src, e_lo, e_hi, n_rows = _dispatch_meta(part[0], n + (N_PARTS + 1) * TM)
    return _experts(v_slabs, src, e_lo, e_hi, n_rows, w1, w3, w2, layer, n)


def _final_kernel(y_ref, h_ref, gate_ref, gain_ref, o_ref, hn_ref):
    for c in range(SLAB_ROWS):
        cols = slice(c * LANES, (c + 1) * LANES)
        hn_ref[:, cols] = h_ref[:, cols] + gate_ref[:, cols] * y_ref[_slab_rows(c), :]
    o_ref[...] = _rms(hn_ref[...], gain_ref[...])


def _final_norm(y_slabs, h, mod, gain, batch):
    lat_tiles = SEQ // TM
    tile = lambda b, t: b * TILES_PER_BATCH + 1 + t
    return pl.pallas_call(
        _final_kernel,
        grid=(batch, lat_tiles),
        in_specs=[
            pl.BlockSpec((TM * SLAB_PITCH, LANES), lambda b, t: (tile(b, t), 0)),
            pl.BlockSpec((TM, D_MODEL), lambda b, t: (tile(b, t), 0)),
            pl.BlockSpec((None, 1, D_MODEL), lambda b, t: (5 * MOD_ROWS + b, 0, 0)),
            pl.BlockSpec((1, D_MODEL), lambda b, t: (0, 0)),
        ],
        out_specs=pl.BlockSpec((TM, D_MODEL), lambda b, t: (b * lat_tiles + t, 0)),
        out_shape=jax.ShapeDtypeStruct((batch * SEQ, D_MODEL), F32),
        scratch_shapes=[pltpu.VMEM((TM, D_MODEL), F32)],
        compiler_params=_params(),
        name="final_norm",
    )(y_slabs, h, mod, gain.reshape(1, D_MODEL))


def _rope_tables(rot_dim, layout):
    n_freq = rot_dim // 4
    inv_freq = 1.0 / (ROPE_THETA ** (jnp.arange(n_freq, dtype=F32) / n_freq))
    t = jnp.arange(SEQ, dtype=jnp.int32)
    row = (t // GRID_W).astype(F32)
    col = (t % GRID_W).astype(F32)
    ang = jnp.concatenate([row[:, None] * inv_freq, col[:, None] * inv_freq], axis=-1)
    cos, sin = jnp.cos(ang), jnp.sin(ang)
    if layout == "padded":
        pad1, pad0 = jnp.ones_like(cos), jnp.zeros_like(sin)
        c = jnp.concatenate([cos, pad1, cos, pad1], axis=-1)
        s = jnp.concatenate([-sin, pad0, sin, pad0], axis=-1)
    else:
        c = jnp.concatenate([cos, cos], axis=-1)
        s = jnp.concatenate([-sin, sin], axis=-1)
    c = jnp.concatenate([jnp.ones((CTX_LEN, LANES), F32), c], axis=0)
    s = jnp.concatenate([jnp.zeros((CTX_LEN, LANES), F32), s], axis=0)
    return c, s


def _pad_rope_cols(w):
    half = MLA_ROPE // 2
    z = jnp.zeros(w.shape[:-1] + (half,), w.dtype)
    return jnp.concatenate([w[..., :half], z, w[..., half:], z], axis=-1)


def _first_proj(h, comb, w, **kw):
    if comb is None:
        out = _proj(None, w, first=h, **kw)
    else:
        out = _proj(None, w, comb=(comb[0], h, comb[1]), **kw)
    if kw["kinds"][-1][0] == "vt":
        return out[:2], out[2]
    return out[0], out[1]


def _out_proj(o, w_o, h, mod, ffn):
    return _proj(o, w_o.astype(BF16), out_dtype=F32, kinds=[("resid",)] * (D_MODEL // LANES),
                 res=h, mod=mod, mod_gate=2, ffn=ffn)


def _gqa_layer(h, comb, mod, norm_gain, w_qkv, q_gain, k_gain, w_o, tabs, ffn, batch):
    kinds = ([("normrope", 0, 0)] * GQA_HEADS + [("normrope", 1, 1)] * GQA_KV_HEADS
             + [("vt", j) for j in range(GQA_KV_HEADS)])
    (qk, vt), h = _first_proj(h, comb, w_qkv.astype(BF16), out_dtype=BF16, kinds=kinds, prologue="normmod",
                              gain=norm_gain, mod=mod, mod_sc=1, mod_sh=0, tabs=tabs,
                              hgain=jnp.stack([q_gain, k_gain]))
    o = _attention(qk, qk, vt, batch=batch, groups=GQA_KV_HEADS // GQA_KV_PER_STEP,
                   heads_per_step=GQA_GROUP * GQA_KV_PER_STEP, kv_per_step=GQA_KV_PER_STEP, dq=HEAD_DIM,
                   k_blk0=GQA_HEADS // GQA_KV_PER_STEP)
    return _out_proj(o, w_o, h, mod, ffn)


def _mla_layer(h, comb, mod, norm_gain, w_a, q_gain, kv_gain, w_uq, w_ukv, w_o, tabs, ffn, batch):
    n_lat = 2 * MLA_RANK // LANES
    w_a_p = jnp.concatenate([w_a[:, :2 * MLA_RANK], _pad_rope_cols(w_a[:, 2 * MLA_RANK:])], axis=1)
    a, h = _first_proj(h, comb, w_a_p.astype(BF16), out_dtype=F32, kinds=[("plain",)] * n_lat + [("rope", 1)],
                       prologue="normmod", gain=norm_gain, mod=mod, mod_sc=1, mod_sh=0, tabs=tabs,
                       chunk=LANES * (n_lat + 1))
    scale = MLA_QK ** -0.5 * LOG2E
    w_uq_h = w_uq.reshape(MLA_RANK, MLA_HEADS, MLA_QK)
    w_uq_p = jnp.concatenate([w_uq_h[..., :MLA_NOPE], _pad_rope_cols(w_uq_h[..., MLA_NOPE:])], axis=-1)
    q = _proj(a, w_uq_p.reshape(MLA_RANK, MLA_HEADS * 2 * LANES).astype(BF16), out_dtype=BF16,
              kinds=[("scale", scale), ("rope", 0)] * MLA_HEADS, prologue="rms", x_col_block=0, gain=q_gain, tabs=tabs)
    w_ukv_h = w_ukv.reshape(MLA_RANK, MLA_HEADS, 2 * HEAD_DIM)
    w_ukv_p = jnp.concatenate([w_ukv_h[..., :MLA_NOPE].reshape(MLA_RANK, -1),
                               w_ukv_h[..., MLA_NOPE:].reshape(MLA_RANK, -1)], axis=1)
    k_nope, vt = _proj(a, w_ukv_p.astype(BF16), out_dtype=BF16,
                       kinds=[("plain",)] * MLA_HEADS + [("vt", j) for j in range(MLA_HEADS)],
                       prologue="rms", x_col_block=1, gain=kv_gain)
    o = _attention(q, k_nope, vt, batch=batch, groups=MLA_HEADS // MLA_HEADS_PER_STEP,
                   heads_per_step=MLA_HEADS_PER_STEP, kv_per_step=MLA_HEADS_PER_STEP, dq=2 * LANES, k_blk0=0,
                   kpe_arr=a, kpe_blk=n_lat)
    return _out_proj(o, w_o, h, mod, ffn)


def _na_layer(h, comb, mod, norm_gain, w_qkv, rpb, w_o, ffn, batch):
    scale = HEAD_DIM ** -0.5 * LOG2E
    kinds = [("scale", scale)] * NA_HEADS + [("plain",)] * NA_HEADS + [("vt", j) for j in range(NA_HEADS)]
    (qk, vt), h = _first_proj(h, comb, w_qkv.astype(BF16), out_dtype=BF16, kinds=kinds, prologue="normmod",
                              gain=norm_gain, mod=mod, mod_sc=1, mod_sh=0)
    o = _na_attention(qk, vt, rpb, batch)
    return _out_proj(o, w_o, h, mod, ffn)


def kernel(x, c, ctx, c_ctx, mod_w, mod_b, norm_mix, norm_ffn, gqa_wqkv, gqa_q_gain, gqa_k_gain, gqa_wo,
           mla_wa, mla_q_gain, mla_kv_gain, mla_wuq, mla_wukv, mla_wo, na_wqkv, na_rpb, na_wo,
           router_w, router_bias, moe_w1, moe_w3, moe_w2, final_gain):
    batch = x.shape[0]
    assert batch <= CTX_MOD_ROW and x.shape[1:] == (SEQ, D_MODEL) and ctx.shape[1:] == (CTX_LEN, D_MODEL)
    h = (x, ctx)
    cond = jnp.zeros((MOD_ROWS, D_MODEL), F32).at[:batch].set(c).at[CTX_MOD_ROW].set(c_ctx)
    mods = _modulation(cond, mod_w, mod_b)
    mods = mods.reshape(DEPTH, MOD_ROWS, 6, D_MODEL).transpose(0, 2, 1, 3).reshape(DEPTH, 6 * MOD_ROWS, 1, D_MODEL)

    c_full, s_full = _rope_tables(HEAD_DIM, "full")
    gqa_scale = HEAD_DIM ** -0.5 * LOG2E
    gqa_tabs = (jnp.concatenate([c_full * gqa_scale, c_full], axis=1),
                jnp.concatenate([s_full * gqa_scale, s_full], axis=1))
    c_pad, s_pad = _rope_tables(MLA_ROPE, "padded")
    mla_scale = MLA_QK ** -0.5 * LOG2E
    mla_tabs = (jnp.concatenate([c_pad * mla_scale, c_pad], axis=1),
                jnp.concatenate([s_pad * mla_scale, s_pad], axis=1))
    rw_hi = router_w.astype(BF16)
    rw_lo = (router_w - rw_hi.astype(F32)).astype(BF16)
    lane_pad = ((0, 0), (0, LANES - N_EXPERTS))
    router_wt = jnp.concatenate([jnp.pad(rw_hi, lane_pad), jnp.pad(rw_lo, lane_pad)], axis=1)
    w1_b, w3_b, w2_b = moe_w1.astype(BF16), moe_w3.astype(BF16), moe_w2.astype(BF16)

    comb = None
    for i in range(DEPTH):
        mod = mods[i]
        ffn = (norm_ffn[i], router_wt, router_bias)
        kind, j = i % N_MIXERS, i // N_MIXERS
        if kind == 0:
            out = _gqa_layer(h, comb, mod, norm_mix[i], gqa_wqkv[j], gqa_q_gain[j], gqa_k_gain[j], gqa_wo[j],
                             gqa_tabs, ffn, batch)
        elif kind == 1:
            out = _mla_layer(h, comb, mod, norm_mix[i], mla_wa[j], mla_q_gain[j], mla_kv_gain[j], mla_wuq[j],
                             mla_wukv[j], mla_wo[j], mla_tabs, ffn, batch)
        else:
            out = _na_layer(h, comb, mod, norm_mix[i], na_wqkv[j], na_rpb[j], na_wo[j], ffn, batch)
        h, v_slabs, part = out
        comb = (_moe_experts(v_slabs, part, w1_b, w3_b, w2_b, i), mod)
    return _final_norm(comb[0], h, comb[1], final_gain, batch).reshape(batch, SEQ, D_MODEL)
```

```python
import functools

import numpy as np
import jax
import jax.numpy as jnp
from jax import lax
from jax.experimental import pallas as pl
from jax.experimental.pallas import tpu as pltpu

F32 = jnp.float32
BF16 = jnp.bfloat16

D_MODEL = 2048
SEQ = 2048
GRID_W = 64
GRID_ROWS = SEQ // GRID_W
CTX_LEN = 256
TOK = CTX_LEN + SEQ
DEPTH = 4
N_MIXERS = 3
EPS = 1e-6
ROPE_THETA = 10000.0

GQA_HEADS = 16
GQA_KV_HEADS = 4
HEAD_DIM = 128
GQA_GROUP = GQA_HEADS // GQA_KV_HEADS

MLA_HEADS = 16
MLA_RANK = 512
MLA_NOPE = 128
MLA_ROPE = 64
MLA_QK = MLA_NOPE + MLA_ROPE

NA_HEADS = 16
NA_WIN_ROWS = 8
NA_WIN_COLS = 16
NA_Q_ROWS = 4
NA_K_ROWS = 12
NA_KEYS = NA_K_ROWS * GRID_W
NA_HEADS_PER_STEP = 8
MLA_HEADS_PER_STEP = 8
GQA_KV_PER_STEP = 2
KEY_CHUNK = 256

N_EXPERTS = 16
N_GROUPS = 4
GROUP_SIZE = N_EXPERTS // N_GROUPS
N_PAIRS = 6
N_PARTS = N_GROUPS * N_PAIRS
D_EXPERT = 512
PAIR_LO = (0, 0, 0, 1, 1, 2)
PAIR_HI = (1, 2, 3, 2, 3, 3)

LANES = 128
TM = 256
TILES_PER_BATCH = TOK // TM
SLAB_ROWS = D_MODEL // LANES
SLAB_PITCH = SLAB_ROWS + 1
MOD_ROWS = 16
CTX_MOD_ROW = 8
NEG_BIG = -1e30
LOG2E = 1.4426950408889634
VMEM_LIMIT = 56 * 1024 * 1024


def _params():
    return pltpu.CompilerParams(vmem_limit_bytes=VMEM_LIMIT)


def _mod_row(i):
    return jnp.where(i % TILES_PER_BATCH == 0, CTX_MOD_ROW, i // TILES_PER_BATCH)


def _silu(x):
    return x / (1.0 + jnp.exp(-x))


def _mod_kernel(cond_ref, w_ref, b_ref, o_ref):
    s = _silu(cond_ref[...]).astype(BF16)
    o_ref[...] = jnp.dot(s, w_ref[...].astype(BF16), preferred_element_type=F32) + b_ref[...]


def _modulation(cond, mod_w, mod_b):
    tn = 1024
    n_out = mod_w.shape[-1]
    return pl.pallas_call(
        _mod_kernel,
        grid=(DEPTH, n_out // tn),
        in_specs=[
            pl.BlockSpec((MOD_ROWS, D_MODEL), lambda l, j: (0, 0)),
            pl.BlockSpec((None, D_MODEL, tn), lambda l, j: (l, 0, j)),
            pl.BlockSpec((None, 1, tn), lambda l, j: (l, 0, j)),
        ],
        out_specs=pl.BlockSpec((None, MOD_ROWS, tn), lambda l, j: (l, 0, j)),
        out_shape=jax.ShapeDtypeStruct((DEPTH, MOD_ROWS, n_out), F32),
        compiler_params=_params(),
        name="modulation",
    )(cond, mod_w, mod_b.reshape(DEPTH, 1, n_out))


def _rms(x, gain):
    return x * lax.rsqrt(jnp.mean(x * x, axis=-1, keepdims=True) + EPS) * gain


def _rope(x, c, s):
    return x * c + pltpu.roll(x, LANES // 2, 1) * s


def _runs(kinds):
    out, start = [], 0
    for k in range(1, len(kinds) + 1):
        if k == len(kinds) or kinds[k] != kinds[start] or kinds[start][0] in ("rope", "normrope", "vt"):
            out.append((start, k, kinds[start]))
            start = k
    return out


def _proj_kernel(*refs, prologue, kinds, chunk, has_tab, has_hgain, resid, has_vt, comb, first, ffn):
    it = iter(refs)
    if comb:
        pos_ref, y_hbm, hprev_ref, cgate_ref = next(it), next(it), next(it), next(it)
        cbuf, csem = refs[-2:]
    elif first:
        lat_ref, ctx_ref = next(it), next(it)
    else:
        x_ref = next(it)
    gain_ref = sc_ref = sh_ref = None
    if prologue == "normmod":
        gain_ref, sc_ref, sh_ref = next(it), next(it), next(it)
    elif prologue == "rms":
        gain_ref = next(it)
    w_ref = next(it)
    ctab_ref = stab_ref = hgain_ref = res_ref = gate_ref = None
    if has_tab:
        ctab_ref, stab_ref = next(it), next(it)
    if has_hgain:
        hgain_ref = next(it)
    if resid:
        res_ref, gate_ref = next(it), next(it)
    if ffn:
        fgain_ref, fsc_ref, fsh_ref, rw_ref, rb_ref = (next(it) for _ in range(5))
    o_ref = next(it)
    vt_ref = next(it) if has_vt else None
    hnew_ref = next(it) if (comb or first) else None
    if ffn:
        v_ref, part_ref = next(it), next(it)

    if first:
        is_ctx = pl.program_id(0) % TILES_PER_BATCH == 0
        hnew_ref[...] = jnp.where(is_ctx, ctx_ref[...], lat_ref[...])
        x_ref = hnew_ref
    if comb:
        ys = _slab_gather(pos_ref, y_hbm, cbuf, csem, pl.program_id(0), pl.num_programs(0), lambda s: s)
        for c in range(SLAB_ROWS):
            cols = slice(c * LANES, (c + 1) * LANES)
            hnew_ref[:, cols] = hprev_ref[:, cols] + cgate_ref[:, cols] * ys[_slab_rows(c), :]
        x_ref = hnew_ref

    if prologue == "none":
        u = x_ref[...]
    else:
        y = _rms(x_ref[...], gain_ref[...])
        if prologue == "normmod":
            y = y * (1.0 + sc_ref[...]) + sh_ref[...]
        u = y.astype(BF16)

    for c0 in range(0, len(kinds) * LANES, chunk):
        acc = jnp.dot(u, w_ref[:, c0:c0 + chunk], preferred_element_type=F32)
        b0 = c0 // LANES
        for lo, hi, kind in _runs(kinds[b0:b0 + chunk // LANES]):
            a = acc[:, lo * LANES:hi * LANES]
            cols = slice(c0 + lo * LANES, c0 + hi * LANES)
            if kind[0] == "scale":
                a = a * kind[1]
            elif kind[0] == "rope":
                t = kind[1]
                a = _rope(a, ctab_ref[:, t * LANES:(t + 1) * LANES], stab_ref[:, t * LANES:(t + 1) * LANES])
            elif kind[0] == "normrope":
                t, g = kind[1], kind[2]
                a = _rms(a, hgain_ref[g:g + 1, :])
                a = _rope(a, ctab_ref[:, t * LANES:(t + 1) * LANES], stab_ref[:, t * LANES:(t + 1) * LANES])
            elif kind[0] == "resid":
                a = res_ref[:, cols] + gate_ref[:, cols] * a
            elif kind[0] == "vt":
                vt_ref[kind[1] * LANES:(kind[1] + 1) * LANES, :] = a.T.astype(vt_ref.dtype)
                continue
            o_ref[:, cols] = a.astype(o_ref.dtype)

    if ffn:
        _ffn_prep_tile(o_ref[...], fgain_ref, fsc_ref, fsh_ref, rw_ref, rb_ref, v_ref, part_ref)


def _proj(x, w, *, out_dtype, kinds, prologue="none", x_col_block=0, gain=None, mod=None,
          mod_sc=None, mod_sh=None, tabs=None, hgain=None, res=None, mod_gate=None, chunk=512,
          comb=None, first=None, ffn=None):
    k, n_all = w.shape
    if first:
        n = first[0].shape[0] * TOK
    else:
        n = comb[1].shape[0] if comb else x.shape[0]
    n_vt = sum(kind[0] == "vt" for kind in kinds)
    n_out = n_all - n_vt * LANES
    assert n % TM == 0 and n_all % chunk == 0 and len(kinds) == n_all // LANES
    assert all(kind[0] == "vt" for kind in kinds[n_out // LANES:])
    prefetch, scratch = [], []
    if comb:
        y_slabs, h_prev, mod_prev, pos = comb
        prefetch = [pos]
        scratch = [pltpu.VMEM((2, TM * SLAB_PITCH, LANES), F32), pltpu.SemaphoreType.DMA((2,))]
        args = [y_slabs, h_prev, mod_prev]
        specs = [pl.BlockSpec(memory_space=pl.ANY),
                 pl.BlockSpec((TM, k), lambda i, *_: (i, 0)),
                 pl.BlockSpec((None, 1, k), lambda i, *_: (5 * MOD_ROWS + _mod_row(i), 0, 0))]
    elif first:
        args = list(first)
        specs = [pl.BlockSpec((None, TM, k),
                              lambda i, *_: (i // TILES_PER_BATCH, jnp.maximum(i % TILES_PER_BATCH - 1, 0), 0)),
                 pl.BlockSpec((None, TM, k), lambda i, *_: (i // TILES_PER_BATCH, 0, 0))]
    else:
        args = [x]
        specs = [pl.BlockSpec((TM, k), lambda i, *_: (i, x_col_block))]
    if prologue in ("normmod", "rms"):
        args.append(gain.reshape(1, k))
        specs.append(pl.BlockSpec((1, k), lambda i, *_: (0, 0)))
    if prologue == "normmod":
        for comp in (mod_sc, mod_sh):
            args.append(mod)
            specs.append(pl.BlockSpec((None, 1, k), lambda i, *_, comp=comp: (comp * MOD_ROWS + _mod_row(i), 0, 0)))
    args.append(w)
    specs.append(pl.BlockSpec((k, n_all), lambda i, *_: (0, 0), pipeline_mode=pl.Buffered(1)))
    if tabs is not None:
        for t in tabs:
            args.append(t)
            specs.append(pl.BlockSpec((TM, t.shape[1]), lambda i, *_: (i % TILES_PER_BATCH, 0)))
    if hgain is not None:
        args.append(hgain)
        specs.append(pl.BlockSpec(hgain.shape, lambda i, *_: (0, 0)))
    if res is not None:
        args.append(res)
        specs.append(pl.BlockSpec((TM, n_out), lambda i, *_: (i, 0)))
        args.append(mod)
        specs.append(pl.BlockSpec((None, 1, n_out), lambda i, *_: (mod_gate * MOD_ROWS + _mod_row(i), 0, 0)))
    if ffn:
        f_gain, router_wt, router_bias = ffn
        assert res is not None and n_out == D_MODEL
        args += [f_gain.reshape(1, D_MODEL), mod, mod, router_wt, router_bias.reshape(N_EXPERTS, 1)]
        specs += [
            pl.BlockSpec((1, D_MODEL), lambda i, *_: (0, 0)),
            pl.BlockSpec((None, 1, D_MODEL), lambda i, *_: (4 * MOD_ROWS + _mod_row(i), 0, 0)),
            pl.BlockSpec((None, 1, D_MODEL), lambda i, *_: (3 * MOD_ROWS + _mod_row(i), 0, 0)),
            pl.BlockSpec((D_MODEL, 2 * LANES), lambda i, *_: (0, 0)),
            pl.BlockSpec((N_EXPERTS, 1), lambda i, *_: (0, 0)),
        ]
    kern = functools.partial(_proj_kernel, prologue=prologue, kinds=tuple(kinds), chunk=chunk,
                             has_tab=tabs is not None, has_hgain=hgain is not None, resid=res is not None,
                             has_vt=n_vt > 0, comb=comb is not None, first=first is not None,
                             ffn=ffn is not None)
    out_specs = [pl.BlockSpec((TM, n_out), lambda i, *_: (i, 0))]
    out_shape = [jax.ShapeDtypeStruct((n, n_out), out_dtype)]
    if n_vt:
        out_specs.append(pl.BlockSpec((n_vt * LANES, TM), lambda i, *_: (0, i)))
        out_shape.append(jax.ShapeDtypeStruct((n_vt * LANES, n), out_dtype))
    if comb or first:
        out_specs.append(pl.BlockSpec((TM, k), lambda i, *_: (i, 0)))
        out_shape.append(jax.ShapeDtypeStruct((n, k), F32))
    if ffn:
        out_specs += [pl.BlockSpec((TM * SLAB_PITCH, LANES), lambda i, *_: (i, 0)), pl.BlockSpec((8, TM), lambda i, *_: (0, i))]
        out_shape += [jax.ShapeDtypeStruct((n * SLAB_PITCH, LANES), F32), jax.ShapeDtypeStruct((8, n), jnp.int32)]
    out = pl.pallas_call(
        kern,
        grid_spec=pltpu.PrefetchScalarGridSpec(
            num_scalar_prefetch=len(prefetch), grid=(n // TM,), in_specs=specs, out_specs=out_specs,
            scratch_shapes=scratch),
        out_shape=out_shape,
        compiler_params=_params(),
        name="proj",
    )(*prefetch, *args)
    return out if len(out) > 1 else out[0]


def _kq(k, q):
    return lax.dot_general(k, q, (((1,), (1,)), ((), ())), preferred_element_type=F32)


def _pipelined_heads(n_heads, n_chunks, q_of, k_of, vt_of, bias_of, s_ref, store):
    def scores(h, c, slot):
        s = _kq(k_of(h, c), q_of(h))
        b = bias_of(h, c)
        s_ref[slot, c] = s if b is None else s + b

    for c in range(n_chunks):
        scores(0, c, 0)
    for h in range(n_heads):
        slot = h % 2
        m = functools.reduce(jnp.maximum,
                             [jnp.max(s_ref[slot, c], axis=0, keepdims=True) for c in range(n_chunks)])
        num, den = None, None
        for c in range(n_chunks):
            if h + 1 < n_heads:
                scores(h + 1, c, 1 - slot)
            p = jnp.exp2(s_ref[slot, c] - m)
            l = jnp.sum(p, axis=0, keepdims=True)
            o = jnp.dot(vt_of(h, c), p.astype(BF16), preferred_element_type=F32)
            num = o if num is None else num + o
            den = l if den is None else den + l
        store(h, (num / den).T)


def _attn_kernel(*refs, heads, kv_heads, dq, has_kpe):
    if has_kpe:
        q_ref, k_ref, kpe_ref, vt_ref, o_ref, s_ref, keys_ref = refs
    else:
        q_ref, k_ref, vt_ref, o_ref, s_ref = refs
        keys_ref = None
    qt = pl.program_id(2)

    if has_kpe:
        @pl.when(qt == 0)
        def _():
            kpe = kpe_ref[...].astype(BF16)
            for h in range(heads):
                keys_ref[h, :, 0:HEAD_DIM] = k_ref[:, h * HEAD_DIM:(h + 1) * HEAD_DIM]
                keys_ref[h, :, HEAD_DIM:] = kpe

    def kv_of(h):
        return h // (heads // kv_heads)

    def q_of(h):
        return q_ref[:, h * dq:(h + 1) * dq]

    def k_of(h, c):
        rows = slice(c * KEY_CHUNK, (c + 1) * KEY_CHUNK)
        if has_kpe:
            return keys_ref[h, rows, :]
        return k_ref[rows, kv_of(h) * HEAD_DIM:(kv_of(h) + 1) * HEAD_DIM]

    def vt_of(h, c):
        return vt_ref[kv_of(h) * HEAD_DIM:(kv_of(h) + 1) * HEAD_DIM, c * KEY_CHUNK:(c + 1) * KEY_CHUNK]

    def store(h, o):
        o_ref[:, h * HEAD_DIM:(h + 1) * HEAD_DIM] = o.astype(o_ref.dtype)

    @pl.when(qt == 0)
    def _():
        _pipelined_heads(heads, CTX_LEN // KEY_CHUNK, q_of, k_of, vt_of, lambda h, c: None, s_ref, store)

    @pl.when(qt > 0)
    def _():
        _pipelined_heads(heads, TOK // KEY_CHUNK, q_of, k_of, vt_of, lambda h, c: None, s_ref, store)


def _attention(q_arr, k_arr, vt_arr, *, batch, groups, heads_per_step, kv_per_step, dq, k_blk0,
               kpe_arr=None, kpe_blk=0):
    shared_kv = kpe_arr is None
    kw = kv_per_step * HEAD_DIM
    ow = heads_per_step * HEAD_DIM
    args = [q_arr, k_arr]
    specs = [
        pl.BlockSpec((TM, heads_per_step * dq), lambda b, g, t: (b * TILES_PER_BATCH + t, g)),
        pl.BlockSpec((TOK, kw), lambda b, g, t: (b, k_blk0 + g)),
    ]
    scratch = [pltpu.VMEM((2, TOK // KEY_CHUNK, KEY_CHUNK, TM), F32)]
    if not shared_kv:
        args.append(kpe_arr)
        specs.append(pl.BlockSpec((TOK, LANES), lambda b, g, t: (b, kpe_blk)))
        scratch.append(pltpu.VMEM((heads_per_step, TOK, HEAD_DIM + LANES), BF16))
    args.append(vt_arr)
    specs.append(pl.BlockSpec((kw, TOK), lambda b, g, t: (g, b)))
    kern = functools.partial(_attn_kernel, heads=heads_per_step, kv_heads=kv_per_step, dq=dq,
                             has_kpe=not shared_kv)
    return pl.pallas_call(
        kern,
        grid=(batch, groups, TILES_PER_BATCH),
        in_specs=specs,
        out_specs=pl.BlockSpec((TM, ow), lambda b, g, t: (b * TILES_PER_BATCH + t, g)),
        out_shape=jax.ShapeDtypeStruct((batch * TOK, groups * ow), BF16),
        scratch_shapes=scratch,
        compiler_params=_params(),
        name="attention",
    )(*args)


def _na_kernel(q_ref, k_ref, vt_ref, tbl_ref, o_ref, s_ref):
    t = pl.program_id(2)
    n_lat_tiles = GRID_ROWS // NA_Q_ROWS
    k_row0 = jnp.clip(NA_Q_ROWS * t - NA_WIN_ROWS // 2, 0, GRID_ROWS - NA_K_ROWS)
    start = pl.multiple_of(CTX_LEN + k_row0 * GRID_W, KEY_CHUNK)
    which = jnp.where(t == 0, 0, jnp.where(t == n_lat_tiles - 1, 2, 1))

    def key_rows(c):
        return pl.ds(0, KEY_CHUNK) if c == 0 else pl.ds(start + (c - 1) * KEY_CHUNK, KEY_CHUNK)

    def q_of(h):
        return q_ref[:, h * HEAD_DIM:(h + 1) * HEAD_DIM]

    def k_of(h, c):
        return k_ref[key_rows(c), h * HEAD_DIM:(h + 1) * HEAD_DIM]

    def vt_of(h, c):
        return vt_ref[h * HEAD_DIM:(h + 1) * HEAD_DIM, key_rows(c)]

    def bias_of(h, c):
        return None if c == 0 else tbl_ref[h, which, (c - 1) * KEY_CHUNK:c * KEY_CHUNK, :]

    def store(h, o):
        o_ref[:, h * HEAD_DIM:(h + 1) * HEAD_DIM] = o.astype(o_ref.dtype)

    @pl.when(t == n_lat_tiles)
    def _():
        _pipelined_heads(NA_HEADS_PER_STEP, 1, q_of, k_of, vt_of, bias_of, s_ref, store)

    @pl.when(t < n_lat_tiles)
    def _():
        _pipelined_heads(NA_HEADS_PER_STEP, 1 + NA_KEYS // KEY_CHUNK, q_of, k_of, vt_of, bias_of, s_ref, store)


def _na_bias_table(rpb):
    cols = np.arange(GRID_W)
    c0 = np.clip(cols - NA_WIN_COLS // 2, 0, GRID_W - NA_WIN_COLS)
    col_ok = (cols[:, None] >= c0[None, :]) & (cols[:, None] < c0[None, :] + NA_WIN_COLS)
    cb = np.clip(cols[:, None] - cols[None, :] + NA_WIN_COLS - 1, 0, 2 * NA_WIN_COLS - 2)
    onehot = (cb[None] == np.arange(2 * NA_WIN_COLS - 1)[:, None, None]).astype(np.float32)
    planes = jnp.einsum("hdc,ckq->hdkq", rpb.astype(F32), onehot, precision=lax.Precision.HIGHEST)
    planes = jnp.where(col_ok[None, None], planes * LOG2E, NEG_BIG)
    n_rel = 2 * NA_WIN_ROWS - 1
    planes = jnp.concatenate([planes, jnp.full((NA_HEADS, 1, GRID_W, GRID_W), NEG_BIG, F32)], axis=1)
    n_lat_tiles = GRID_ROWS // NA_Q_ROWS
    which = np.full((3, NA_K_ROWS, NA_Q_ROWS), n_rel, np.int32)
    for kind, t in enumerate((0, 1, n_lat_tiles - 1)):
        k_row0 = int(np.clip(NA_Q_ROWS * t - NA_WIN_ROWS // 2, 0, GRID_ROWS - NA_K_ROWS))
        for kk in range(NA_K_ROWS):
            for i in range(NA_Q_ROWS):
                qr, kr = NA_Q_ROWS * t + i, k_row0 + kk
                r0 = int(np.clip(qr - NA_WIN_ROWS // 2, 0, GRID_ROWS - NA_WIN_ROWS))
                if r0 <= kr < r0 + NA_WIN_ROWS:
                    which[kind, kk, i] = kr - qr + NA_WIN_ROWS - 1
    tbl = jnp.concatenate([planes[:, which[:, :, i]] for i in range(NA_Q_ROWS)], axis=-1)
    return tbl.reshape(NA_HEADS, 3, NA_KEYS, TM)


def _na_attention(qk, vt, rpb, batch):
    tbl = _na_bias_table(rpb)
    n_lat_tiles = GRID_ROWS // NA_Q_ROWS

    def q_tile(b, t):
        return b * TILES_PER_BATCH + jnp.where(t == n_lat_tiles, 0, t + 1)

    gw = NA_HEADS_PER_STEP * HEAD_DIM
    n_groups = NA_HEADS // NA_HEADS_PER_STEP
    return pl.pallas_call(
        _na_kernel,
        grid=(n_groups, batch, n_lat_tiles + 1),
        in_specs=[
            pl.BlockSpec((TM, gw), lambda g, b, t: (q_tile(b, t), g)),
            pl.BlockSpec((TOK, gw), lambda g, b, t: (b, n_groups + g)),
            pl.BlockSpec((gw, TOK), lambda g, b, t: (g, b)),
            pl.BlockSpec((NA_HEADS_PER_STEP, 3, NA_KEYS, TM), lambda g, b, t: (g, 0, 0, 0),
                         pipeline_mode=pl.Buffered(1)),
        ],
        out_specs=pl.BlockSpec((TM, gw), lambda g, b, t: (q_tile(b, t), g)),
        out_shape=jax.ShapeDtypeStruct((batch * TOK, NA_HEADS * HEAD_DIM), BF16),
        scratch_shapes=[pltpu.VMEM((2, 1 + NA_KEYS // KEY_CHUNK, KEY_CHUNK, TM), F32)],
        compiler_params=_params(),
        name="na_attention",
    )(qk, qk, vt, tbl)


def _route(logits, bias):
    scores = 1.0 / (1.0 + jnp.exp(-logits))
    sel = scores + bias
    srow = [sel[e:e + 1, :] for e in range(N_EXPERTS)]
    crow = [scores[e:e + 1, :] for e in range(N_EXPERTS)]
    gscore = []
    for g in range(N_GROUPS):
        a, b, c, d = srow[GROUP_SIZE * g:GROUP_SIZE * (g + 1)]
        gscore.append(jnp.maximum(jnp.maximum(jnp.maximum(a + b, a + c), jnp.maximum(a + d, b + c)),
                                  jnp.maximum(b + d, c + d)))
    best, gidx = gscore[0], jnp.zeros(gscore[0].shape, jnp.int32)
    for g in range(1, N_GROUPS):
        better = gscore[g] > best
        gidx = jnp.where(better, g, gidx)
        best = jnp.where(better, gscore[g], best)
    s, c = [], []
    for j in range(GROUP_SIZE):
        sj, cj = srow[j], crow[j]
        for g in range(1, N_GROUPS):
            sj = jnp.where(gidx == g, srow[GROUP_SIZE * g + j], sj)
            cj = jnp.where(gidx == g, crow[GROUP_SIZE * g + j], cj)
        s.append(sj)
        c.append(cj)
    chosen = []
    for j in range(GROUP_SIZE):
        rank = jnp.zeros(s[j].shape, jnp.int32)
        for k in range(GROUP_SIZE):
            if k < j:
                rank = rank + jnp.where(s[k] >= s[j], 1, 0)
            elif k > j:
                rank = rank + jnp.where(s[k] > s[j], 1, 0)
        chosen.append(rank < 2)
    w = [jnp.where(chosen[j], c[j], 0.0) for j in range(GROUP_SIZE)]
    denom = (w[0] + w[1]) + (w[2] + w[3])
    gate = [w[j] / denom for j in range(GROUP_SIZE)]
    lo = jnp.where(chosen[0], 0, jnp.where(chosen[1], 1, 2))
    hi = jnp.where(chosen[3], 3, jnp.where(chosen[2], 2, 1))
    g_lo = jnp.where(chosen[0], gate[0], jnp.where(chosen[1], gate[1], gate[2]))
    g_hi = jnp.where(chosen[3], gate[3], jnp.where(chosen[2], gate[2], gate[1]))
    pair = jnp.where(lo == 0, hi - 1, jnp.where(lo == 1, hi + 1, N_PAIRS - 1))
    return gidx * N_PAIRS + pair, g_lo, g_hi


def _slab_rows(c):
    return pl.ds(c, TM, stride=SLAB_PITCH)


def _store_slabs(ref, val, extra):
    for c in range(SLAB_ROWS):
        ref[_slab_rows(c), :] = val[:, c * LANES:(c + 1) * LANES]
    ref[_slab_rows(SLAB_ROWS), :] = extra


def _ffn_prep_tile(h, gain_ref, sc_ref, sh_ref, rw_ref, rb_ref, v_ref, part_ref):
    v = _rms(h, gain_ref[...]) * (1.0 + sc_ref[...]) + sh_ref[...]
    v_hi = v.astype(BF16)
    v_lo = (v - v_hi.astype(F32)).astype(BF16)
    a = jnp.dot(v_hi, rw_ref[...], preferred_element_type=F32)
    b = jnp.dot(v_lo, rw_ref[:, 0:LANES], preferred_element_type=F32)
    logits = (a[:, 0:LANES] + a[:, LANES:] + b).T[0:N_EXPERTS, :]
    part, g_lo, g_hi = _route(logits, rb_ref[...])
    part_ref[...] = jnp.zeros(part_ref.shape, jnp.int32)
    part_ref[0:1, :] = part
    row = lax.broadcasted_iota(jnp.int32, (LANES, TM), 0)
    gates_t = jnp.where(row == 0, g_lo, jnp.where(row == 1, g_hi, 0.0))
    _store_slabs(v_ref, v, gates_t.T)


def _dispatch_meta(part, n_slots):
    n = part.shape[0]
    onehot = (part[:, None] == jnp.arange(N_PARTS, dtype=jnp.int32)[None, :]).astype(jnp.int32)
    csum = jnp.cumsum(onehot, axis=0)
    rank = jnp.take_along_axis(csum, part[:, None], axis=1)[:, 0] - 1
    counts = csum[-1]
    padded = ((counts + TM - 1) // TM) * TM
    ends = jnp.cumsum(padded)
    starts = ends - padded
    pos = (starts[part] + rank).astype(jnp.int32)
    src = jnp.zeros((n_slots,), jnp.int32).at[pos].set(jnp.arange(n, dtype=jnp.int32))
    tile_start = jnp.arange(n_slots // TM, dtype=jnp.int32) * TM
    last_start = jnp.minimum(tile_start, ends[-1] - TM)
    tile_part = jnp.sum((ends[None, :] <= last_start[:, None]).astype(jnp.int32), axis=1)
    n_rows = jnp.clip(counts[tile_part] - (tile_start - starts[tile_part]), 0, TM).astype(jnp.int32)
    group, pair = tile_part // N_PAIRS, tile_part % N_PAIRS
    e_lo = group * GROUP_SIZE + jnp.asarray(PAIR_LO, jnp.int32)[pair]
    e_hi = group * GROUP_SIZE + jnp.asarray(PAIR_HI, jnp.int32)[pair]
    return pos, src, e_lo, e_hi, n_rows


def _slab_gather(pos_ref, ys_hbm, cbuf, csem, step, n_steps, tile_of):
    slot = step % 2

    def copy(slot_, r, row):
        return pltpu.make_async_copy(ys_hbm.at[pl.ds(row * SLAB_PITCH, SLAB_PITCH), :],
                                     cbuf.at[slot_, pl.ds(r * SLAB_PITCH, SLAB_PITCH), :], csem.at[slot_])

    def start(s, slot_):
        base = tile_of(s) * TM
        for r in range(TM):
            copy(slot_, r, pos_ref[base + r]).start(priority=r % 2)

    @pl.when(step == 0)
    def _():
        start(0, 0)

    for r in range(TM):
        copy(slot, r, 0).wait()

    @pl.when(step + 1 < n_steps)
    def _():
        start(step + 1, 1 - slot)

    return cbuf.at[slot]


def _expert_kernel(src_ref, elo_ref, ehi_ref, nrows_ref, v_hbm, w1a, w3a, w2a, w1b, w3b, w2b, y_ref,
                   xbuf, x2d, in_sem):
    t = pl.program_id(0)
    slot = t % 2

    def in_copy(slot_, r, tok):
        return pltpu.make_async_copy(v_hbm.at[pl.ds(tok * SLAB_PITCH, SLAB_PITCH), :],
                                     xbuf.at[slot_, pl.ds(r * SLAB_PITCH, SLAB_PITCH), :], in_sem.at[slot_])

    def start_in(tile, slot_):
        for r in range(TM):
            in_copy(slot_, r, src_ref[tile * TM + r]).start(priority=r % 2)

    n_prev = jnp.where(t > 0, nrows_ref[jnp.maximum(t - 1, 0)], 0)

    @pl.when(t == 0)
    def _():
        start_in(0, 0)

    @pl.when(jnp.logical_or(t == 0, n_prev > 0))
    def _():
        for r in range(TM):
            in_copy(slot, r, 0).wait()

    @pl.when(nrows_ref[t] > 0)
    def _():
        start_in(t + 1, 1 - slot)
        xs = xbuf.at[slot]
        for c in range(SLAB_ROWS):
            x2d[:, c * LANES:(c + 1) * LANES] = xs[_slab_rows(c), :].astype(BF16)
        g = xs[_slab_rows(SLAB_ROWS), :]
        x = x2d[...]

        def hidden(w1, w3, gate):
            a = jnp.dot(x, w1[...], preferred_element_type=F32)
            b = jnp.dot(x, w3[...], preferred_element_type=F32)
            return (_silu(a) * b * gate).astype(BF16)

        y = jnp.dot(hidden(w1a, w3a, g[:, 0:1]), w2a[...], preferred_element_type=F32)
        y = y + jnp.dot(hidden(w1b, w3b, g[:, 1:2]), w2b[...], preferred_element_type=F32)
        _store_slabs(y_ref, y, jnp.zeros((TM, LANES), F32))

    @pl.when(nrows_ref[t] == 0)
    def _():
        y_ref[...] = jnp.zeros(y_ref.shape, y_ref.dtype)


def _experts(v_slabs, src, e_lo, e_hi, n_rows, w1, w3, w2, layer):
    n_tiles = n_rows.shape[0]
    up = lambda sel: pl.BlockSpec((None, None, D_MODEL, D_EXPERT),
                                  lambda t, s, lo, hi, nr: (layer, (lo, hi)[sel][t], 0, 0))
    down = lambda sel: pl.BlockSpec((None, None, D_EXPERT, D_MODEL),
                                    lambda t, s, lo, hi, nr: (layer, (lo, hi)[sel][t], 0, 0))
    return pl.pallas_call(
        _expert_kernel,
        grid_spec=pltpu.PrefetchScalarGridSpec(
            num_scalar_prefetch=4,
            grid=(n_tiles,),
            in_specs=[pl.BlockSpec(memory_space=pl.ANY), up(0), up(0), down(0), up(1), up(1), down(1)],
            out_specs=pl.BlockSpec((TM * SLAB_PITCH, LANES), lambda t, s, lo, hi, nr: (t, 0)),
            scratch_shapes=[pltpu.VMEM((2, TM * SLAB_PITCH, LANES), F32), pltpu.VMEM((TM, D_MODEL), BF16),
                            pltpu.SemaphoreType.DMA((2,))],
        ),
        out_shape=jax.ShapeDtypeStruct((n_tiles * TM * SLAB_PITCH, LANES), F32),
        compiler_params=_params(),
        name="experts",
    )(src, e_lo, e_hi, n_rows, v_slabs, w1, w3, w2, w1, w3, w2)


def _moe_experts(v_slabs, part, w1, w3, w2, layer):
    n = part.shape[1]
    pos, src, e_lo, e_hi, n_rows = _dispatch_meta(part[0], n + (N_PARTS + 1) * TM)
    return _experts(v_slabs, src, e_lo, e_hi, n_rows, w1, w3, w2, layer), pos


def _latent_tile(j):
    lat_tiles = SEQ // TM
    return (j // lat_tiles) * TILES_PER_BATCH + 1 + j % lat_tiles


def _final_kernel(pos_ref, y_hbm, h_ref, gate_ref, gain_ref, o_ref, hn_ref, cbuf, csem):
    ys = _slab_gather(pos_ref, y_hbm, cbuf, csem, pl.program_id(0), pl.num_programs(0), _latent_tile)
    for c in range(SLAB_ROWS):
        cols = slice(c * LANES, (c + 1) * LANES)
        hn_ref[:, cols] = h_ref[:, cols] + gate_ref[:, cols] * ys[_slab_rows(c), :]
    o_ref[...] = _rms(hn_ref[...], gain_ref[...])


def _final_norm(y_slabs, pos, h, mod, gain, batch):
    lat_tiles = SEQ // TM
    return pl.pallas_call(
        _final_kernel,
        grid_spec=pltpu.PrefetchScalarGridSpec(
            num_scalar_prefetch=1,
            grid=(batch * lat_tiles,),
            in_specs=[
                pl.BlockSpec(memory_space=pl.ANY),
                pl.BlockSpec((TM, D_MODEL), lambda j, p: (_latent_tile(j), 0)),
                pl.BlockSpec((None, 1, D_MODEL), lambda j, p: (5 * MOD_ROWS + j // lat_tiles, 0, 0)),
                pl.BlockSpec((1, D_MODEL), lambda j, p: (0, 0)),
            ],
            out_specs=pl.BlockSpec((TM, D_MODEL), lambda j, p: (j, 0)),
            scratch_shapes=[pltpu.VMEM((TM, D_MODEL), F32), pltpu.VMEM((2, TM * SLAB_PITCH, LANES), F32),
                            pltpu.SemaphoreType.DMA((2,))],
        ),
        out_shape=jax.ShapeDtypeStruct((batch * SEQ, D_MODEL), F32),
        compiler_params=_params(),
        name="final_norm",
    )(pos, y_slabs, h, mod, gain.reshape(1, D_MODEL))


def _rope_tables(rot_dim, layout):
    n_freq = rot_dim // 4
    inv_freq = 1.0 / (ROPE_THETA ** (jnp.arange(n_freq, dtype=F32) / n_freq))
    t = jnp.arange(SEQ, dtype=jnp.int32)
    row = (t // GRID_W).astype(F32)
    col = (t % GRID_W).astype(F32)
    ang = jnp.concatenate([row[:, None] * inv_freq, col[:, None] * inv_freq], axis=-1)
    cos, sin = jnp.cos(ang), jnp.sin(ang)
    if layout == "padded":
        pad1, pad0 = jnp.ones_like(cos), jnp.zeros_like(sin)
        c = jnp.concatenate([cos, pad1, cos, pad1], axis=-1)
        s = jnp.concatenate([-sin, pad0, sin, pad0], axis=-1)
    else:
        c = jnp.concatenate([cos, cos], axis=-1)
        s = jnp.concatenate([-sin, sin], axis=-1)
    c = jnp.concatenate([jnp.ones((CTX_LEN, LANES), F32), c], axis=0)
    s = jnp.concatenate([jnp.zeros((CTX_LEN, LANES), F32), s], axis=0)
    return c, s


def _pad_rope_cols(w):
    half = MLA_ROPE // 2
    z = jnp.zeros(w.shape[:-1] + (half,), w.dtype)
    return jnp.concatenate([w[..., :half], z, w[..., half:], z], axis=-1)


def _first_proj(h, comb, w, **kw):
    if comb is None:
        out = _proj(None, w, first=h, **kw)
    else:
        y_slabs, mod_prev, pos = comb
        out = _proj(None, w, comb=(y_slabs, h, mod_prev, pos), **kw)
    if kw["kinds"][-1][0] == "vt":
        return out[:2], out[2]
    return out[0], out[1]


def _out_proj(o, w_o, h, mod, ffn):
    return _proj(o, w_o.astype(BF16), out_dtype=F32, kinds=[("resid",)] * (D_MODEL // LANES),
                 res=h, mod=mod, mod_gate=2, ffn=ffn)


def _gqa_layer(h, comb, mod, norm_gain, w_qkv, q_gain, k_gain, w_o, tabs, ffn, batch):
    kinds = ([("normrope", 0, 0)] * GQA_HEADS + [("normrope", 1, 1)] * GQA_KV_HEADS
             + [("vt", j) for j in range(GQA_KV_HEADS)])
    (qk, vt), h = _first_proj(h, comb, w_qkv.astype(BF16), out_dtype=BF16, kinds=kinds, prologue="normmod",
                              gain=norm_gain, mod=mod, mod_sc=1, mod_sh=0, tabs=tabs,
                              hgain=jnp.stack([q_gain, k_gain]))
    o = _attention(qk, qk, vt, batch=batch, groups=GQA_KV_HEADS // GQA_KV_PER_STEP,
                   heads_per_step=GQA_GROUP * GQA_KV_PER_STEP, kv_per_step=GQA_KV_PER_STEP, dq=HEAD_DIM,
                   k_blk0=GQA_HEADS // GQA_KV_PER_STEP)
    return _out_proj(o, w_o, h, mod, ffn)


def _mla_layer(h, comb, mod, norm_gain, w_a, q_gain, kv_gain, w_uq, w_ukv, w_o, tabs, ffn, batch):
    n_lat = 2 * MLA_RANK // LANES
    w_a_p = jnp.concatenate([w_a[:, :2 * MLA_RANK], _pad_rope_cols(w_a[:, 2 * MLA_RANK:])], axis=1)
    a, h = _first_proj(h, comb, w_a_p.astype(BF16), out_dtype=F32, kinds=[("plain",)] * n_lat + [("rope", 1)],
                       prologue="normmod", gain=norm_gain, mod=mod, mod_sc=1, mod_sh=0, tabs=tabs,
                       chunk=LANES * (n_lat + 1))
    scale = MLA_QK ** -0.5 * LOG2E
    w_uq_h = w_uq.reshape(MLA_RANK, MLA_HEADS, MLA_QK)
    w_uq_p = jnp.concatenate([w_uq_h[..., :MLA_NOPE], _pad_rope_cols(w_uq_h[..., MLA_NOPE:])], axis=-1)
    q = _proj(a, w_uq_p.reshape(MLA_RANK, MLA_HEADS * 2 * LANES).astype(BF16), out_dtype=BF16,
              kinds=[("scale", scale), ("rope", 0)] * MLA_HEADS, prologue="rms", x_col_block=0, gain=q_gain, tabs=tabs)
    w_ukv_h = w_ukv.reshape(MLA_RANK, MLA_HEADS, 2 * HEAD_DIM)
    w_ukv_p = jnp.concatenate([w_ukv_h[..., :MLA_NOPE].reshape(MLA_RANK, -1),
                               w_ukv_h[..., MLA_NOPE:].reshape(MLA_RANK, -1)], axis=1)
    k_nope, vt = _proj(a, w_ukv_p.astype(BF16), out_dtype=BF16,
                       kinds=[("plain",)] * MLA_HEADS + [("vt", j) for j in range(MLA_HEADS)],
                       prologue="rms", x_col_block=1, gain=kv_gain)
    o = _attention(q, k_nope, vt, batch=batch, groups=MLA_HEADS // MLA_HEADS_PER_STEP,
                   heads_per_step=MLA_HEADS_PER_STEP, kv_per_step=MLA_HEADS_PER_STEP, dq=2 * LANES, k_blk0=0,
                   kpe_arr=a, kpe_blk=n_lat)
    return _out_proj(o, w_o, h, mod, ffn)


def _na_layer(h, comb, mod, norm_gain, w_qkv, rpb, w_o, ffn, batch):
    scale = HEAD_DIM ** -0.5 * LOG2E
    kinds = [("scale", scale)] * NA_HEADS + [("plain",)] * NA_HEADS + [("vt", j) for j in range(NA_HEADS)]
    (qk, vt), h = _first_proj(h, comb, w_qkv.astype(BF16), out_dtype=BF16, kinds=kinds, prologue="normmod",
                              gain=norm_gain, mod=mod, mod_sc=1, mod_sh=0)
    o = _na_attention(qk, vt, rpb, batch)
    return _out_proj(o, w_o, h, mod, ffn)


def kernel(x, c, ctx, c_ctx, mod_w, mod_b, norm_mix, norm_ffn, gqa_wqkv, gqa_q_gain, gqa_k_gain, gqa_wo,
           mla_wa, mla_q_gain, mla_kv_gain, mla_wuq, mla_wukv, mla_wo, na_wqkv, na_rpb, na_wo,
           router_w, router_bias, moe_w1, moe_w3, moe_w2, final_gain):
    batch = x.shape[0]
    assert batch <= CTX_MOD_ROW and x.shape[1:] == (SEQ, D_MODEL) and ctx.shape[1:] == (CTX_LEN, D_MODEL)
    h = (x, ctx)
    cond = jnp.zeros((MOD_ROWS, D_MODEL), F32).at[:batch].set(c).at[CTX_MOD_ROW].set(c_ctx)
    mods = _modulation(cond, mod_w, mod_b)
    mods = mods.reshape(DEPTH, MOD_ROWS, 6, D_MODEL).transpose(0, 2, 1, 3).reshape(DEPTH, 6 * MOD_ROWS, 1, D_MODEL)

    c_full, s_full = _rope_tables(HEAD_DIM, "full")
    gqa_scale = HEAD_DIM ** -0.5 * LOG2E
    gqa_tabs = (jnp.concatenate([c_full * gqa_scale, c_full], axis=1),
                jnp.concatenate([s_full * gqa_scale, s_full], axis=1))
    c_pad, s_pad = _rope_tables(MLA_ROPE, "padded")
    mla_scale = MLA_QK ** -0.5 * LOG2E
    mla_tabs = (jnp.concatenate([c_pad * mla_scale, c_pad], axis=1),
                jnp.concatenate([s_pad * mla_scale, s_pad], axis=1))
    rw_hi = router_w.astype(BF16)
    rw_lo = (router_w - rw_hi.astype(F32)).astype(BF16)
    lane_pad = ((0, 0), (0, LANES - N_EXPERTS))
    router_wt = jnp.concatenate([jnp.pad(rw_hi, lane_pad), jnp.pad(rw_lo, lane_pad)], axis=1)
    w1_b, w3_b, w2_b = moe_w1.astype(BF16), moe_w3.astype(BF16), moe_w2.astype(BF16)

    comb = None
    for i in range(DEPTH):
        mod = mods[i]
        ffn = (norm_ffn[i], router_wt, router_bias)
        kind, j = i % N_MIXERS, i // N_MIXERS
        if kind == 0:
            out = _gqa_layer(h, comb, mod, norm_mix[i], gqa_wqkv[j], gqa_q_gain[j], gqa_k_gain[j], gqa_wo[j],
                             gqa_tabs, ffn, batch)
        elif kind == 1:
            out = _mla_layer(h, comb, mod, norm_mix[i], mla_wa[j], mla_q_gain[j], mla_kv_gain[j], mla_wuq[j],
                             mla_wukv[j], mla_wo[j], mla_tabs, ffn, batch)
        else:
            out = _na_layer(h, comb, mod, norm_mix[i], na_wqkv[j], na_rpb[j], na_wo[j], ffn, batch)
        h, v_slabs, part = out
        y_slabs, pos = _moe_experts(v_slabs, part, w1_b, w3_b, w2_b, i)
        comb = (y_slabs, mod, pos)
    return _final_norm(y_slabs, pos, h, mod, final_gain, batch).reshape(batch, SEQ, D_MODEL)
```

```python
import functools

import numpy as np
import jax
import jax.numpy as jnp
from jax import lax
from jax.experimental import pallas as pl
from jax.experimental.pallas import tpu as pltpu

F32 = jnp.float32
BF16 = jnp.bfloat16

D_MODEL = 2048
SEQ = 2048
GRID_W = 64
GRID_ROWS = SEQ // GRID_W
CTX_LEN = 256
TOK = CTX_LEN + SEQ
DEPTH = 4
N_MIXERS = 3
EPS = 1e-6
ROPE_THETA = 10000.0

GQA_HEADS = 16
GQA_KV_HEADS = 4
HEAD_DIM = 128
GQA_GROUP = GQA_HEADS // GQA_KV_HEADS

MLA_HEADS = 16
MLA_RANK = 512
MLA_NOPE = 128
MLA_ROPE = 64
MLA_QK = MLA_NOPE + MLA_ROPE

NA_HEADS = 16
NA_WIN_ROWS = 8
NA_WIN_COLS = 16
NA_Q_ROWS = 4
NA_K_ROWS = 12
NA_KEYS = NA_K_ROWS * GRID_W
NA_HEADS_PER_STEP = 8
MLA_HEADS_PER_STEP = 8
GQA_KV_PER_STEP = 2
KEY_CHUNK = 256

N_EXPERTS = 16
N_GROUPS = 4
GROUP_SIZE = N_EXPERTS // N_GROUPS
N_PAIRS = 6
N_PARTS = N_GROUPS * N_PAIRS
D_EXPERT = 512
PAIR_LO = (0, 0, 0, 1, 1, 2)
PAIR_HI = (1, 2, 3, 3, 2, 3)

LANES = 128
TM = 256
TILES_PER_BATCH = TOK // TM
SLAB_ROWS = D_MODEL // LANES
SLAB_PITCH = SLAB_ROWS + 1
MOD_ROWS = 16
CTX_MOD_ROW = 8
NEG_BIG = -1e30
LOG2E = 1.4426950408889634
VMEM_LIMIT = 56 * 1024 * 1024


def _params():
    return pltpu.CompilerParams(vmem_limit_bytes=VMEM_LIMIT)


def _mod_row(i):
    return jnp.where(i % TILES_PER_BATCH == 0, CTX_MOD_ROW, i // TILES_PER_BATCH)


def _silu(x):
    return x / (1.0 + jnp.exp(-x))


def _mod_kernel(cond_ref, w_ref, b_ref, o_ref):
    s = _silu(cond_ref[...]).astype(BF16)
    o_ref[...] = jnp.dot(s, w_ref[...].astype(BF16), preferred_element_type=F32) + b_ref[...]


def _modulation(cond, mod_w, mod_b):
    tn = 1024
    n_out = mod_w.shape[-1]
    return pl.pallas_call(
        _mod_kernel,
        grid=(DEPTH, n_out // tn),
        in_specs=[
            pl.BlockSpec((MOD_ROWS, D_MODEL), lambda l, j: (0, 0)),
            pl.BlockSpec((None, D_MODEL, tn), lambda l, j: (l, 0, j)),
            pl.BlockSpec((None, 1, tn), lambda l, j: (l, 0, j)),
        ],
        out_specs=pl.BlockSpec((None, MOD_ROWS, tn), lambda l, j: (l, 0, j)),
        out_shape=jax.ShapeDtypeStruct((DEPTH, MOD_ROWS, n_out), F32),
        compiler_params=_params(),
        name="modulation",
    )(cond, mod_w, mod_b.reshape(DEPTH, 1, n_out))


def _rms(x, gain):
    return x * lax.rsqrt(jnp.mean(x * x, axis=-1, keepdims=True) + EPS) * gain


def _rope(x, c, s):
    return x * c + pltpu.roll(x, LANES // 2, 1) * s


def _runs(kinds):
    out, start = [], 0
    for k in range(1, len(kinds) + 1):
        if k == len(kinds) or kinds[k] != kinds[start] or kinds[start][0] in ("rope", "normrope", "vt"):
            out.append((start, k, kinds[start]))
            start = k
    return out


def _proj_kernel(*refs, prologue, kinds, chunk, has_tab, has_hgain, resid, has_vt, comb, first, ffn):
    it = iter(refs)
    if comb:
        pos_ref, y_hbm, hprev_ref, cgate_ref = next(it), next(it), next(it), next(it)
        cbuf, csem = refs[-2:]
    elif first:
        lat_ref, ctx_ref = next(it), next(it)
    else:
        x_ref = next(it)
    gain_ref = sc_ref = sh_ref = None
    if prologue == "normmod":
        gain_ref, sc_ref, sh_ref = next(it), next(it), next(it)
    elif prologue == "rms":
        gain_ref = next(it)
    w_ref = next(it)
    ctab_ref = stab_ref = hgain_ref = res_ref = gate_ref = None
    if has_tab:
        ctab_ref, stab_ref = next(it), next(it)
    if has_hgain:
        hgain_ref = next(it)
    if resid:
        res_ref, gate_ref = next(it), next(it)
    if ffn:
        fgain_ref, fsc_ref, fsh_ref, rw_ref, rb_ref = (next(it) for _ in range(5))
    o_ref = next(it)
    vt_ref = next(it) if has_vt else None
    hnew_ref = next(it) if (comb or first) else None
    if ffn:
        v_ref, part_ref = next(it), next(it)

    if first:
        is_ctx = pl.program_id(0) % TILES_PER_BATCH == 0
        hnew_ref[...] = jnp.where(is_ctx, ctx_ref[...], lat_ref[...])
        x_ref = hnew_ref
    if comb:
        ys = _slab_gather(pos_ref, y_hbm, cbuf, csem, pl.program_id(0), pl.num_programs(0), lambda s: s)
        for c in range(SLAB_ROWS):
            cols = slice(c * LANES, (c + 1) * LANES)
            hnew_ref[:, cols] = hprev_ref[:, cols] + cgate_ref[:, cols] * ys[_slab_rows(c), :]
        x_ref = hnew_ref

    if prologue == "none":
        u = x_ref[...]
    else:
        y = _rms(x_ref[...], gain_ref[...])
        if prologue == "normmod":
            y = y * (1.0 + sc_ref[...]) + sh_ref[...]
        u = y.astype(BF16)

    for c0 in range(0, len(kinds) * LANES, chunk):
        acc = jnp.dot(u, w_ref[:, c0:c0 + chunk], preferred_element_type=F32)
        b0 = c0 // LANES
        for lo, hi, kind in _runs(kinds[b0:b0 + chunk // LANES]):
            a = acc[:, lo * LANES:hi * LANES]
            cols = slice(c0 + lo * LANES, c0 + hi * LANES)
            if kind[0] == "scale":
                a = a * kind[1]
            elif kind[0] == "rope":
                t = kind[1]
                a = _rope(a, ctab_ref[:, t * LANES:(t + 1) * LANES], stab_ref[:, t * LANES:(t + 1) * LANES])
            elif kind[0] == "normrope":
                t, g = kind[1], kind[2]
                a = _rms(a, hgain_ref[g:g + 1, :])
                a = _rope(a, ctab_ref[:, t * LANES:(t + 1) * LANES], stab_ref[:, t * LANES:(t + 1) * LANES])
            elif kind[0] == "resid":
                a = res_ref[:, cols] + gate_ref[:, cols] * a
            elif kind[0] == "vt":
                vt_ref[kind[1] * LANES:(kind[1] + 1) * LANES, :] = a.T.astype(vt_ref.dtype)
                continue
            o_ref[:, cols] = a.astype(o_ref.dtype)

    if ffn:
        _ffn_prep_tile(o_ref[...], fgain_ref, fsc_ref, fsh_ref, rw_ref, rb_ref, v_ref, part_ref)


def _proj(x, w, *, out_dtype, kinds, prologue="none", x_col_block=0, gain=None, mod=None,
          mod_sc=None, mod_sh=None, tabs=None, hgain=None, res=None, mod_gate=None, chunk=512,
          comb=None, first=None, ffn=None):
    k, n_all = w.shape
    if first:
        n = first[0].shape[0] * TOK
    else:
        n = comb[1].shape[0] if comb else x.shape[0]
    n_vt = sum(kind[0] == "vt" for kind in kinds)
    n_out = n_all - n_vt * LANES
    assert n % TM == 0 and n_all % chunk == 0 and len(kinds) == n_all // LANES
    assert all(kind[0] == "vt" for kind in kinds[n_out // LANES:])
    prefetch, scratch = [], []
    if comb:
        y_slabs, h_prev, mod_prev, pos = comb
        prefetch = [pos]
        scratch = [pltpu.VMEM((2, TM * SLAB_PITCH, LANES), F32), pltpu.SemaphoreType.DMA((2,))]
        args = [y_slabs, h_prev, mod_prev]
        specs = [pl.BlockSpec(memory_space=pl.ANY),
                 pl.BlockSpec((TM, k), lambda i, *_: (i, 0)),
                 pl.BlockSpec((None, 1, k), lambda i, *_: (5 * MOD_ROWS + _mod_row(i), 0, 0))]
    elif first:
        args = list(first)
        specs = [pl.BlockSpec((None, TM, k),
                              lambda i, *_: (i // TILES_PER_BATCH, jnp.maximum(i % TILES_PER_BATCH - 1, 0), 0)),
                 pl.BlockSpec((None, TM, k), lambda i, *_: (i // TILES_PER_BATCH, 0, 0))]
    else:
        args = [x]
        specs = [pl.BlockSpec((TM, k), lambda i, *_: (i, x_col_block))]
    if prologue in ("normmod", "rms"):
        args.append(gain.reshape(1, k))
        specs.append(pl.BlockSpec((1, k), lambda i, *_: (0, 0)))
    if prologue == "normmod":
        for comp in (mod_sc, mod_sh):
            args.append(mod)
            specs.append(pl.BlockSpec((None, 1, k), lambda i, *_, comp=comp: (comp * MOD_ROWS + _mod_row(i), 0, 0)))
    args.append(w)
    specs.append(pl.BlockSpec((k, n_all), lambda i, *_: (0, 0), pipeline_mode=pl.Buffered(1)))
    if tabs is not None:
        for t in tabs:
            args.append(t)
            specs.append(pl.BlockSpec((TM, t.shape[1]), lambda i, *_: (i % TILES_PER_BATCH, 0)))
    if hgain is not None:
        args.append(hgain)
        specs.append(pl.BlockSpec(hgain.shape, lambda i, *_: (0, 0)))
    if res is not None:
        args.append(res)
        specs.append(pl.BlockSpec((TM, n_out), lambda i, *_: (i, 0)))
        args.append(mod)
        specs.append(pl.BlockSpec((None, 1, n_out), lambda i, *_: (mod_gate * MOD_ROWS + _mod_row(i), 0, 0)))
    if ffn:
        f_gain, router_wt, router_bias = ffn
        assert res is not None and n_out == D_MODEL
        args += [f_gain.reshape(1, D_MODEL), mod, mod, router_wt, router_bias.reshape(N_EXPERTS, 1)]
        specs += [
            pl.BlockSpec((1, D_MODEL), lambda i, *_: (0, 0)),
            pl.BlockSpec((None, 1, D_MODEL), lambda i, *_: (4 * MOD_ROWS + _mod_row(i), 0, 0)),
            pl.BlockSpec((None, 1, D_MODEL), lambda i, *_: (3 * MOD_ROWS + _mod_row(i), 0, 0)),
            pl.BlockSpec((D_MODEL, 2 * LANES), lambda i, *_: (0, 0)),
            pl.BlockSpec((N_EXPERTS, 1), lambda i, *_: (0, 0)),
        ]
    kern = functools.partial(_proj_kernel, prologue=prologue, kinds=tuple(kinds), chunk=chunk,
                             has_tab=tabs is not None, has_hgain=hgain is not None, resid=res is not None,
                             has_vt=n_vt > 0, comb=comb is not None, first=first is not None,
                             ffn=ffn is not None)
    out_specs = [pl.BlockSpec((TM, n_out), lambda i, *_: (i, 0))]
    out_shape = [jax.ShapeDtypeStruct((n, n_out), out_dtype)]
    if n_vt:
        out_specs.append(pl.BlockSpec((n_vt * LANES, TM), lambda i, *_: (0, i)))
        out_shape.append(jax.ShapeDtypeStruct((n_vt * LANES, n), out_dtype))
    if comb or first:
        out_specs.append(pl.BlockSpec((TM, k), lambda i, *_: (i, 0)))
        out_shape.append(jax.ShapeDtypeStruct((n, k), F32))
    if ffn:
        out_specs += [pl.BlockSpec((TM * SLAB_PITCH, LANES), lambda i, *_: (i, 0)), pl.BlockSpec((8, TM), lambda i, *_: (0, i))]
        out_shape += [jax.ShapeDtypeStruct((n * SLAB_PITCH, LANES), F32), jax.ShapeDtypeStruct((8, n), jnp.int32)]
    out = pl.pallas_call(
        kern,
        grid_spec=pltpu.PrefetchScalarGridSpec(
            num_scalar_prefetch=len(prefetch), grid=(n // TM,), in_specs=specs, out_specs=out_specs,
            scratch_shapes=scratch),
        out_shape=out_shape,
        compiler_params=_params(),
        name="proj",
    )(*prefetch, *args)
    return out if len(out) > 1 else out[0]


def _kq(k, q):
    return lax.dot_general(k, q, (((1,), (1,)), ((), ())), preferred_element_type=F32)


def _pipelined_heads(n_heads, n_chunks, q_of, k_of, vt_of, bias_of, s_ref, store):
    def scores(h, c, slot):
        s = _kq(k_of(h, c), q_of(h))
        b = bias_of(h, c)
        s_ref[slot, c] = s if b is None else s + b

    for c in range(n_chunks):
        scores(0, c, 0)
    for h in range(n_heads):
        slot = h % 2
        m = functools.reduce(jnp.maximum,
                             [jnp.max(s_ref[slot, c], axis=0, keepdims=True) for c in range(n_chunks)])
        num, den = None, None
        for c in range(n_chunks):
            if h + 1 < n_heads:
                scores(h + 1, c, 1 - slot)
            p = jnp.exp2(s_ref[slot, c] - m)
            l = jnp.sum(p, axis=0, keepdims=True)
            o = jnp.dot(vt_of(h, c), p.astype(BF16), preferred_element_type=F32)
            num = o if num is None else num + o
            den = l if den is None else den + l
        store(h, (num / den).T)


def _attn_kernel(*refs, heads, kv_heads, dq, has_kpe):
    if has_kpe:
        q_ref, k_ref, kpe_ref, vt_ref, o_ref, s_ref, keys_ref = refs
    else:
        q_ref, k_ref, vt_ref, o_ref, s_ref = refs
        keys_ref = None
    qt = pl.program_id(2)

    if has_kpe:
        @pl.when(qt == 0)
        def _():
            kpe = kpe_ref[...].astype(BF16)
            for h in range(heads):
                keys_ref[h, :, 0:HEAD_DIM] = k_ref[:, h * HEAD_DIM:(h + 1) * HEAD_DIM]
                keys_ref[h, :, HEAD_DIM:] = kpe

    def kv_of(h):
        return h // (heads // kv_heads)

    def q_of(h):
        return q_ref[:, h * dq:(h + 1) * dq]

    def k_of(h, c):
        rows = slice(c * KEY_CHUNK, (c + 1) * KEY_CHUNK)
        if has_kpe:
            return keys_ref[h, rows, :]
        return k_ref[rows, kv_of(h) * HEAD_DIM:(kv_of(h) + 1) * HEAD_DIM]

    def vt_of(h, c):
        return vt_ref[kv_of(h) * HEAD_DIM:(kv_of(h) + 1) * HEAD_DIM, c * KEY_CHUNK:(c + 1) * KEY_CHUNK]

    def store(h, o):
        o_ref[:, h * HEAD_DIM:(h + 1) * HEAD_DIM] = o.astype(o_ref.dtype)

    @pl.when(qt == 0)
    def _():
        _pipelined_heads(heads, CTX_LEN // KEY_CHUNK, q_of, k_of, vt_of, lambda h, c: None, s_ref, store)

    @pl.when(qt > 0)
    def _():
        _pipelined_heads(heads, TOK // KEY_CHUNK, q_of, k_of, vt_of, lambda h, c: None, s_ref, store)


def _attention(q_arr, k_arr, vt_arr, *, batch, groups, heads_per_step, kv_per_step, dq, k_blk0,
               kpe_arr=None, kpe_blk=0):
    shared_kv = kpe_arr is None
    kw = kv_per_step * HEAD_DIM
    ow = heads_per_step * HEAD_DIM
    args = [q_arr, k_arr]
    specs = [
        pl.BlockSpec((TM, heads_per_step * dq), lambda b, g, t: (b * TILES_PER_BATCH + t, g)),
        pl.BlockSpec((TOK, kw), lambda b, g, t: (b, k_blk0 + g)),
    ]
    scratch = [pltpu.VMEM((2, TOK // KEY_CHUNK, KEY_CHUNK, TM), F32)]
    if not shared_kv:
        args.append(kpe_arr)
        specs.append(pl.BlockSpec((TOK, LANES), lambda b, g, t: (b, kpe_blk)))
        scratch.append(pltpu.VMEM((heads_per_step, TOK, HEAD_DIM + LANES), BF16))
    args.append(vt_arr)
    specs.append(pl.BlockSpec((kw, TOK), lambda b, g, t: (g, b)))
    kern = functools.partial(_attn_kernel, heads=heads_per_step, kv_heads=kv_per_step, dq=dq,
                             has_kpe=not shared_kv)
    return pl.pallas_call(
        kern,
        grid=(batch, groups, TILES_PER_BATCH),
        in_specs=specs,
        out_specs=pl.BlockSpec((TM, ow), lambda b, g, t: (b * TILES_PER_BATCH + t, g)),
        out_shape=jax.ShapeDtypeStruct((batch * TOK, groups * ow), BF16),
        scratch_shapes=scratch,
        compiler_params=_params(),
        name="attention",
    )(*args)


def _na_kernel(q_ref, k_ref, vt_ref, tbl_ref, o_ref, s_ref):
    t = pl.program_id(2)
    n_lat_tiles = GRID_ROWS // NA_Q_ROWS
    k_row0 = jnp.clip(NA_Q_ROWS * t - NA_WIN_ROWS // 2, 0, GRID_ROWS - NA_K_ROWS)
    start = pl.multiple_of(CTX_LEN + k_row0 * GRID_W, KEY_CHUNK)
    which = jnp.where(t == 0, 0, jnp.where(t == n_lat_tiles - 1, 2, 1))

    def key_rows(c):
        return pl.ds(0, KEY_CHUNK) if c == 0 else pl.ds(start + (c - 1) * KEY_CHUNK, KEY_CHUNK)

    def q_of(h):
        return q_ref[:, h * HEAD_DIM:(h + 1) * HEAD_DIM]

    def k_of(h, c):
        return k_ref[key_rows(c), h * HEAD_DIM:(h + 1) * HEAD_DIM]

    def vt_of(h, c):
        return vt_ref[h * HEAD_DIM:(h + 1) * HEAD_DIM, key_rows(c)]

    def bias_of(h, c):
        return None if c == 0 else tbl_ref[h, which, (c - 1) * KEY_CHUNK:c * KEY_CHUNK, :]

    def store(h, o):
        o_ref[:, h * HEAD_DIM:(h + 1) * HEAD_DIM] = o.astype(o_ref.dtype)

    @pl.when(t == n_lat_tiles)
    def _():
        _pipelined_heads(NA_HEADS_PER_STEP, 1, q_of, k_of, vt_of, bias_of, s_ref, store)

    @pl.when(t < n_lat_tiles)
    def _():
        _pipelined_heads(NA_HEADS_PER_STEP, 1 + NA_KEYS // KEY_CHUNK, q_of, k_of, vt_of, bias_of, s_ref, store)


def _na_bias_table(rpb):
    cols = np.arange(GRID_W)
    c0 = np.clip(cols - NA_WIN_COLS // 2, 0, GRID_W - NA_WIN_COLS)
    col_ok = (cols[:, None] >= c0[None, :]) & (cols[:, None] < c0[None, :] + NA_WIN_COLS)
    cb = np.clip(cols[:, None] - cols[None, :] + NA_WIN_COLS - 1, 0, 2 * NA_WIN_COLS - 2)
    onehot = (cb[None] == np.arange(2 * NA_WIN_COLS - 1)[:, None, None]).astype(np.float32)
    planes = jnp.einsum("hdc,ckq->hdkq", rpb.astype(F32), onehot, precision=lax.Precision.HIGHEST)
    planes = jnp.where(col_ok[None, None], planes * LOG2E, NEG_BIG)
    n_rel = 2 * NA_WIN_ROWS - 1
    planes = jnp.concatenate([planes, jnp.full((NA_HEADS, 1, GRID_W, GRID_W), NEG_BIG, F32)], axis=1)
    n_lat_tiles = GRID_ROWS // NA_Q_ROWS
    which = np.full((3, NA_K_ROWS, NA_Q_ROWS), n_rel, np.int32)
    for kind, t in enumerate((0, 1, n_lat_tiles - 1)):
        k_row0 = int(np.clip(NA_Q_ROWS * t - NA_WIN_ROWS // 2, 0, GRID_ROWS - NA_K_ROWS))
        for kk in range(NA_K_ROWS):
            for i in range(NA_Q_ROWS):
                qr, kr = NA_Q_ROWS * t + i, k_row0 + kk
                r0 = int(np.clip(qr - NA_WIN_ROWS // 2, 0, GRID_ROWS - NA_WIN_ROWS))
                if r0 <= kr < r0 + NA_WIN_ROWS:
                    which[kind, kk, i] = kr - qr + NA_WIN_ROWS - 1
    tbl = jnp.concatenate([planes[:, which[:, :, i]] for i in range(NA_Q_ROWS)], axis=-1)
    return tbl.reshape(NA_HEADS, 3, NA_KEYS, TM)


def _na_attention(qk, vt, rpb, batch):
    tbl = _na_bias_table(rpb)
    n_lat_tiles = GRID_ROWS // NA_Q_ROWS

    def q_tile(b, t):
        return b * TILES_PER_BATCH + jnp.where(t == n_lat_tiles, 0, t + 1)

    gw = NA_HEADS_PER_STEP * HEAD_DIM
    n_groups = NA_HEADS // NA_HEADS_PER_STEP
    return pl.pallas_call(
        _na_kernel,
        grid=(n_groups, batch, n_lat_tiles + 1),
        in_specs=[
            pl.BlockSpec((TM, gw), lambda g, b, t: (q_tile(b, t), g)),
            pl.BlockSpec((TOK, gw), lambda g, b, t: (b, n_groups + g)),
            pl.BlockSpec((gw, TOK), lambda g, b, t: (g, b)),
            pl.BlockSpec((NA_HEADS_PER_STEP, 3, NA_KEYS, TM), lambda g, b, t: (g, 0, 0, 0),
                         pipeline_mode=pl.Buffered(1)),
        ],
        out_specs=pl.BlockSpec((TM, gw), lambda g, b, t: (q_tile(b, t), g)),
        out_shape=jax.ShapeDtypeStruct((batch * TOK, NA_HEADS * HEAD_DIM), BF16),
        scratch_shapes=[pltpu.VMEM((2, 1 + NA_KEYS // KEY_CHUNK, KEY_CHUNK, TM), F32)],
        compiler_params=_params(),
        name="na_attention",
    )(qk, qk, vt, tbl)


def _route(logits, bias):
    scores = 1.0 / (1.0 + jnp.exp(-logits))
    sel = scores + bias
    srow = [sel[e:e + 1, :] for e in range(N_EXPERTS)]
    crow = [scores[e:e + 1, :] for e in range(N_EXPERTS)]
    gscore = []
    for g in range(N_GROUPS):
        a, b, c, d = srow[GROUP_SIZE * g:GROUP_SIZE * (g + 1)]
        gscore.append(jnp.maximum(jnp.maximum(jnp.maximum(a + b, a + c), jnp.maximum(a + d, b + c)),
                                  jnp.maximum(b + d, c + d)))
    best, gidx = gscore[0], jnp.zeros(gscore[0].shape, jnp.int32)
    for g in range(1, N_GROUPS):
        better = gscore[g] > best
        gidx = jnp.where(better, g, gidx)
        best = jnp.where(better, gscore[g], best)
    s, c = [], []
    for j in range(GROUP_SIZE):
        sj, cj = srow[j], crow[j]
        for g in range(1, N_GROUPS):
            sj = jnp.where(gidx == g, srow[GROUP_SIZE * g + j], sj)
            cj = jnp.where(gidx == g, crow[GROUP_SIZE * g + j], cj)
        s.append(sj)
        c.append(cj)
    chosen = []
    for j in range(GROUP_SIZE):
        rank = jnp.zeros(s[j].shape, jnp.int32)
        for k in range(GROUP_SIZE):
            if k < j:
                rank = rank + jnp.where(s[k] >= s[j], 1, 0)
            elif k > j:
                rank = rank + jnp.where(s[k] > s[j], 1, 0)
        chosen.append(rank < 2)
    w = [jnp.where(chosen[j], c[j], 0.0) for j in range(GROUP_SIZE)]
    denom = (w[0] + w[1]) + (w[2] + w[3])
    gate = [w[j] / denom for j in range(GROUP_SIZE)]
    lo = jnp.where(chosen[0], 0, jnp.where(chosen[1], 1, 2))
    hi = jnp.where(chosen[3], 3, jnp.where(chosen[2], 2, 1))
    g_lo = jnp.where(chosen[0], gate[0], jnp.where(chosen[1], gate[1], gate[2]))
    g_hi = jnp.where(chosen[3], gate[3], jnp.where(chosen[2], gate[2], gate[1]))
    pair = jnp.where(lo == 0, hi - 1, jnp.where(lo == 1, 6 - hi, N_PAIRS - 1))
    return gidx * N_PAIRS + pair, g_lo, g_hi


def _slab_rows(c):
    return pl.ds(c, TM, stride=SLAB_PITCH)


def _store_slabs(ref, val, extra):
    for c in range(SLAB_ROWS):
        ref[_slab_rows(c), :] = val[:, c * LANES:(c + 1) * LANES]
    ref[_slab_rows(SLAB_ROWS), :] = extra


def _ffn_prep_tile(h, gain_ref, sc_ref, sh_ref, rw_ref, rb_ref, v_ref, part_ref):
    v = _rms(h, gain_ref[...]) * (1.0 + sc_ref[...]) + sh_ref[...]
    v_hi = v.astype(BF16)
    v_lo = (v - v_hi.astype(F32)).astype(BF16)
    a = jnp.dot(v_hi, rw_ref[...], preferred_element_type=F32)
    b = jnp.dot(v_lo, rw_ref[:, 0:LANES], preferred_element_type=F32)
    logits = (a[:, 0:LANES] + a[:, LANES:] + b).T[0:N_EXPERTS, :]
    part, g_lo, g_hi = _route(logits, rb_ref[...])
    part_ref[...] = jnp.zeros(part_ref.shape, jnp.int32)
    part_ref[0:1, :] = part
    row = lax.broadcasted_iota(jnp.int32, (LANES, TM), 0)
    gates_t = jnp.where(row == 0, g_lo, jnp.where(row == 1, g_hi, 0.0))
    _store_slabs(v_ref, v, gates_t.T)


def _dispatch_meta(part, n_slots):
    n = part.shape[0]
    onehot = (part[:, None] == jnp.arange(N_PARTS, dtype=jnp.int32)[None, :]).astype(jnp.int32)
    csum = jnp.cumsum(onehot, axis=0)
    rank = jnp.take_along_axis(csum, part[:, None], axis=1)[:, 0] - 1
    counts = csum[-1]
    padded = ((counts + TM - 1) // TM) * TM
    ends = jnp.cumsum(padded)
    starts = ends - padded
    pos = (starts[part] + rank).astype(jnp.int32)
    src = jnp.zeros((n_slots,), jnp.int32).at[pos].set(jnp.arange(n, dtype=jnp.int32))
    tile_start = jnp.arange(n_slots // TM, dtype=jnp.int32) * TM
    last_start = jnp.minimum(tile_start, ends[-1] - TM)
    tile_part = jnp.sum((ends[None, :] <= last_start[:, None]).astype(jnp.int32), axis=1)
    n_rows = jnp.clip(counts[tile_part] - (tile_start - starts[tile_part]), 0, TM).astype(jnp.int32)
    group, pair = tile_part // N_PAIRS, tile_part % N_PAIRS
    e_lo = group * GROUP_SIZE + jnp.asarray(PAIR_LO, jnp.int32)[pair]
    e_hi = group * GROUP_SIZE + jnp.asarray(PAIR_HI, jnp.int32)[pair]
    return pos, src, e_lo, e_hi, n_rows


def _slab_gather(pos_ref, ys_hbm, cbuf, csem, step, n_steps, tile_of):
    slot = step % 2

    def copy(slot_, r, row):
        return pltpu.make_async_copy(ys_hbm.at[pl.ds(row * SLAB_PITCH, SLAB_PITCH), :],
                                     cbuf.at[slot_, pl.ds(r * SLAB_PITCH, SLAB_PITCH), :], csem.at[slot_])

    def start(s, slot_):
        base = tile_of(s) * TM
        for r in range(TM):
            copy(slot_, r, pos_ref[base + r]).start(priority=r % 2)

    @pl.when(step == 0)
    def _():
        start(0, 0)

    for r in range(TM):
        copy(slot, r, 0).wait()

    @pl.when(step + 1 < n_steps)
    def _():
        start(step + 1, 1 - slot)

    return cbuf.at[slot]


def _expert_kernel(src_ref, elo_ref, ehi_ref, nrows_ref, v_hbm, w1a, w3a, w2a, w1b, w3b, w2b, y_ref,
                   xbuf, x2d, in_sem):
    t = pl.program_id(0)
    slot = t % 2

    def in_copy(slot_, r, tok):
        return pltpu.make_async_copy(v_hbm.at[pl.ds(tok * SLAB_PITCH, SLAB_PITCH), :],
                                     xbuf.at[slot_, pl.ds(r * SLAB_PITCH, SLAB_PITCH), :], in_sem.at[slot_])

    def start_in(tile, slot_):
        for r in range(TM):
            in_copy(slot_, r, src_ref[tile * TM + r]).start(priority=r % 2)

    n_prev = jnp.where(t > 0, nrows_ref[jnp.maximum(t - 1, 0)], 0)

    @pl.when(t == 0)
    def _():
        start_in(0, 0)

    @pl.when(jnp.logical_or(t == 0, n_prev > 0))
    def _():
        for r in range(TM):
            in_copy(slot, r, 0).wait()

    @pl.when(nrows_ref[t] > 0)
    def _():
        start_in(t + 1, 1 - slot)

    @pl.when(nrows_ref[t] > 0)
    def _():
        xs = xbuf.at[slot]
        for c in range(SLAB_ROWS):
            x2d[:, c * LANES:(c + 1) * LANES] = xs[_slab_rows(c), :].astype(BF16)
        g = xs[_slab_rows(SLAB_ROWS), :]
        x = x2d[...]

        def hidden(w1, w3, gate):
            a = jnp.dot(x, w1[...], preferred_element_type=F32)
            b = jnp.dot(x, w3[...], preferred_element_type=F32)
            return (_silu(a) * b * gate).astype(BF16)

        y = jnp.dot(hidden(w1a, w3a, g[:, 0:1]), w2a[...], preferred_element_type=F32)
        y = y + jnp.dot(hidden(w1b, w3b, g[:, 1:2]), w2b[...], preferred_element_type=F32)
        _store_slabs(y_ref, y, jnp.zeros((TM, LANES), F32))

    @pl.when(nrows_ref[t] == 0)
    def _():
        y_ref[...] = jnp.zeros(y_ref.shape, y_ref.dtype)


def _experts(v_slabs, src, e_lo, e_hi, n_rows, w1, w3, w2, layer):
    n_tiles = n_rows.shape[0]
    up = lambda sel: pl.BlockSpec((None, None, D_MODEL, D_EXPERT),
                                  lambda t, s, lo, hi, nr: (layer, (lo, hi)[sel][t], 0, 0))
    down = lambda sel: pl.BlockSpec((None, None, D_EXPERT, D_MODEL),
                                    lambda t, s, lo, hi, nr: (layer, (lo, hi)[sel][t], 0, 0))
    return pl.pallas_call(
        _expert_kernel,
        grid_spec=pltpu.PrefetchScalarGridSpec(
            num_scalar_prefetch=4,
            grid=(n_tiles,),
            in_specs=[pl.BlockSpec(memory_space=pl.ANY), up(0), up(0), down(0), up(1), up(1), down(1)],
            out_specs=pl.BlockSpec((TM * SLAB_PITCH, LANES), lambda t, s, lo, hi, nr: (t, 0)),
            scratch_shapes=[pltpu.VMEM((2, TM * SLAB_PITCH, LANES), F32), pltpu.VMEM((TM, D_MODEL), BF16),
                            pltpu.SemaphoreType.DMA((2,))],
        ),
        out_shape=jax.ShapeDtypeStruct((n_tiles * TM * SLAB_PITCH, LANES), F32),
        compiler_params=_params(),
        name="experts",
    )(src, e_lo, e_hi, n_rows, v_slabs, w1, w3, w2, w1, w3, w2)


def _moe_experts(v_slabs, part, w1, w3, w2, layer):
    n = part.shape[1]
    pos, src, e_lo, e_hi, n_rows = _dispatch_meta(part[0], n + (N_PARTS + 1) * TM)
    return _experts(v_slabs, src, e_lo, e_hi, n_rows, w1, w3, w2, layer), pos


def _latent_tile(j):
    lat_tiles = SEQ // TM
    return (j // lat_tiles) * TILES_PER_BATCH + 1 + j % lat_tiles


def _final_kernel(pos_ref, y_hbm, h_ref, gate_ref, gain_ref, o_ref, hn_ref, cbuf, csem):
    ys = _slab_gather(pos_ref, y_hbm, cbuf, csem, pl.program_id(0), pl.num_programs(0), _latent_tile)
    for c in range(SLAB_ROWS):
        cols = slice(c * LANES, (c + 1) * LANES)
        hn_ref[:, cols] = h_ref[:, cols] + gate_ref[:, cols] * ys[_slab_rows(c), :]
    o_ref[...] = _rms(hn_ref[...], gain_ref[...])


def _final_norm(y_slabs, pos, h, mod, gain, batch):
    lat_tiles = SEQ // TM
    return pl.pallas_call(
        _final_kernel,
        grid_spec=pltpu.PrefetchScalarGridSpec(
            num_scalar_prefetch=1,
            grid=(batch * lat_tiles,),
            in_specs=[
                pl.BlockSpec(memory_space=pl.ANY),
                pl.BlockSpec((TM, D_MODEL), lambda j, p: (_latent_tile(j), 0)),
                pl.BlockSpec((None, 1, D_MODEL), lambda j, p: (5 * MOD_ROWS + j // lat_tiles, 0, 0)),
                pl.BlockSpec((1, D_MODEL), lambda j, p: (0, 0)),
            ],
            out_specs=pl.BlockSpec((TM, D_MODEL), lambda j, p: (j, 0)),
            scratch_shapes=[pltpu.VMEM((TM, D_MODEL), F32), pltpu.VMEM((2, TM * SLAB_PITCH, LANES), F32),
                            pltpu.SemaphoreType.DMA((2,))],
        ),
        out_shape=jax.ShapeDtypeStruct((batch * SEQ, D_MODEL), F32),
        compiler_params=_params(),
        name="final_norm",
    )(pos, y_slabs, h, mod, gain.reshape(1, D_MODEL))


def _rope_tables(rot_dim, layout):
    n_freq = rot_dim // 4
    inv_freq = 1.0 / (ROPE_THETA ** (jnp.arange(n_freq, dtype=F32) / n_freq))
    t = jnp.arange(SEQ, dtype=jnp.int32)
    row = (t // GRID_W).astype(F32)
    col = (t % GRID_W).astype(F32)
    ang = jnp.concatenate([row[:, None] * inv_freq, col[:, None] * inv_freq], axis=-1)
    cos, sin = jnp.cos(ang), jnp.sin(ang)
    if layout == "padded":
        pad1, pad0 = jnp.ones_like(cos), jnp.zeros_like(sin)
        c = jnp.concatenate([cos, pad1, cos, pad1], axis=-1)
        s = jnp.concatenate([-sin, pad0, sin, pad0], axis=-1)
    else:
        c = jnp.concatenate([cos, cos], axis=-1)
        s = jnp.concatenate([-sin, sin], axis=-1)
    c = jnp.concatenate([jnp.ones((CTX_LEN, LANES), F32), c], axis=0)
    s = jnp.concatenate([jnp.zeros((CTX_LEN, LANES), F32), s], axis=0)
    return c, s


def _pad_rope_cols(w):
    half = MLA_ROPE // 2
    z = jnp.zeros(w.shape[:-1] + (half,), w.dtype)
    return jnp.concatenate([w[..., :half], z, w[..., half:], z], axis=-1)


def _first_proj(h, comb, w, **kw):
    if comb is None:
        out = _proj(None, w, first=h, **kw)
    else:
        y_slabs, mod_prev, pos = comb
        out = _proj(None, w, comb=(y_slabs, h, mod_prev, pos), **kw)
    if kw["kinds"][-1][0] == "vt":
        return out[:2], out[2]
    return out[0], out[1]


def _out_proj(o, w_o, h, mod, ffn):
    return _proj(o, w_o.astype(BF16), out_dtype=F32, kinds=[("resid",)] * (D_MODEL // LANES),
                 res=h, mod=mod, mod_gate=2, ffn=ffn)


def _gqa_layer(h, comb, mod, norm_gain, w_qkv, q_gain, k_gain, w_o, tabs, ffn, batch):
    kinds = ([("normrope", 0, 0)] * GQA_HEADS + [("normrope", 1, 1)] * GQA_KV_HEADS
             + [("vt", j) for j in range(GQA_KV_HEADS)])
    (qk, vt), h = _first_proj(h, comb, w_qkv.astype(BF16), out_dtype=BF16, kinds=kinds, prologue="normmod",
                              gain=norm_gain, mod=mod, mod_sc=1, mod_sh=0, tabs=tabs,
                              hgain=jnp.stack([q_gain, k_gain]))
    o = _attention(qk, qk, vt, batch=batch, groups=GQA_KV_HEADS // GQA_KV_PER_STEP,
                   heads_per_step=GQA_GROUP * GQA_KV_PER_STEP, kv_per_step=GQA_KV_PER_STEP, dq=HEAD_DIM,
                   k_blk0=GQA_HEADS // GQA_KV_PER_STEP)
    return _out_proj(o, w_o, h, mod, ffn)


def _mla_layer(h, comb, mod, norm_gain, w_a, q_gain, kv_gain, w_uq, w_ukv, w_o, tabs, ffn, batch):
    n_lat = 2 * MLA_RANK // LANES
    w_a_p = jnp.concatenate([w_a[:, :2 * MLA_RANK], _pad_rope_cols(w_a[:, 2 * MLA_RANK:])], axis=1)
    a, h = _first_proj(h, comb, w_a_p.astype(BF16), out_dtype=F32, kinds=[("plain",)] * n_lat + [("rope", 1)],
                       prologue="normmod", gain=norm_gain, mod=mod, mod_sc=1, mod_sh=0, tabs=tabs,
                       chunk=LANES * (n_lat + 1))
    scale = MLA_QK ** -0.5 * LOG2E
    w_uq_h = w_uq.reshape(MLA_RANK, MLA_HEADS, MLA_QK)
    w_uq_p = jnp.concatenate([w_uq_h[..., :MLA_NOPE], _pad_rope_cols(w_uq_h[..., MLA_NOPE:])], axis=-1)
    q = _proj(a, w_uq_p.reshape(MLA_RANK, MLA_HEADS * 2 * LANES).astype(BF16), out_dtype=BF16,
              kinds=[("scale", scale), ("rope", 0)] * MLA_HEADS, prologue="rms", x_col_block=0, gain=q_gain, tabs=tabs)
    w_ukv_h = w_ukv.reshape(MLA_RANK, MLA_HEADS, 2 * HEAD_DIM)
    w_ukv_p = jnp.concatenate([w_ukv_h[..., :MLA_NOPE].reshape(MLA_RANK, -1),
                               w_ukv_h[..., MLA_NOPE:].reshape(MLA_RANK, -1)], axis=1)
    k_nope, vt = _proj(a, w_ukv_p.astype(BF16), out_dtype=BF16,
                       kinds=[("plain",)] * MLA_HEADS + [("vt", j) for j in range(MLA_HEADS)],
                       prologue="rms", x_col_block=1, gain=kv_gain)
    o = _attention(q, k_nope, vt, batch=batch, groups=MLA_HEADS // MLA_HEADS_PER_STEP,
                   heads_per_step=MLA_HEADS_PER_STEP, kv_per_step=MLA_HEADS_PER_STEP, dq=2 * LANES, k_blk0=0,
                   kpe_arr=a, kpe_blk=n_lat)
    return _out_proj(o, w_o, h, mod, ffn)


def _na_layer(h, comb, mod, norm_gain, w_qkv, rpb, w_o, ffn, batch):
    scale = HEAD_DIM ** -0.5 * LOG2E
    kinds = [("scale", scale)] * NA_HEADS + [("plain",)] * NA_HEADS + [("vt", j) for j in range(NA_HEADS)]
    (qk, vt), h = _first_proj(h, comb, w_qkv.astype(BF16), out_dtype=BF16, kinds=kinds, prologue="normmod",
                              gain=norm_gain, mod=mod, mod_sc=1, mod_sh=0)
    o = _na_attention(qk, vt, rpb, batch)
    return _out_proj(o, w_o, h, mod, ffn)


def kernel(x, c, ctx, c_ctx, mod_w, mod_b, norm_mix, norm_ffn, gqa_wqkv, gqa_q_gain, gqa_k_gain, gqa_wo,
           mla_wa, mla_q_gain, mla_kv_gain, mla_wuq, mla_wukv, mla_wo, na_wqkv, na_rpb, na_wo,
           router_w, router_bias, moe_w1, moe_w3, moe_w2, final_gain):
    batch = x.shape[0]
    assert batch <= CTX_MOD_ROW and x.shape[1:] == (SEQ, D_MODEL) and ctx.shape[1:] == (CTX_LEN, D_MODEL)
    h = (x, ctx)
    cond = jnp.zeros((MOD_ROWS, D_MODEL), F32).at[:batch].set(c).at[CTX_MOD_ROW].set(c_ctx)
    mods = _modulation(cond, mod_w, mod_b)
    mods = mods.reshape(DEPTH, MOD_ROWS, 6, D_MODEL).transpose(0, 2, 1, 3).reshape(DEPTH, 6 * MOD_ROWS, 1, D_MODEL)

    c_full, s_full = _rope_tables(HEAD_DIM, "full")
    gqa_scale = HEAD_DIM ** -0.5 * LOG2E
    gqa_tabs = (jnp.concatenate([c_full * gqa_scale, c_full], axis=1),
                jnp.concatenate([s_full * gqa_scale, s_full], axis=1))
    c_pad, s_pad = _rope_tables(MLA_ROPE, "padded")
    mla_scale = MLA_QK ** -0.5 * LOG2E
    mla_tabs = (jnp.concatenate([c_pad * mla_scale, c_pad], axis=1),
                jnp.concatenate([s_pad * mla_scale, s_pad], axis=1))
    rw_hi = router_w.astype(BF16)
    rw_lo = (router_w - rw_hi.astype(F32)).astype(BF16)
    lane_pad = ((0, 0), (0, LANES - N_EXPERTS))
    router_wt = jnp.concatenate([jnp.pad(rw_hi, lane_pad), jnp.pad(rw_lo, lane_pad)], axis=1)
    w1_b, w3_b, w2_b = moe_w1.astype(BF16), moe_w3.astype(BF16), moe_w2.astype(BF16)

    comb = None
    for i in range(DEPTH):
        mod = mods[i]
        ffn = (norm_ffn[i], router_wt, router_bias)
        kind, j = i % N_MIXERS, i // N_MIXERS
        if kind == 0:
            out = _gqa_layer(h, comb, mod, norm_mix[i], gqa_wqkv[j], gqa_q_gain[j], gqa_k_gain[j], gqa_wo[j],
                             gqa_tabs, ffn, batch)
        elif kind == 1:
            out = _mla_layer(h, comb, mod, norm_mix[i], mla_wa[j], mla_q_gain[j], mla_kv_gain[j], mla_wuq[j],
                             mla_wukv[j], mla_wo[j], mla_tabs, ffn, batch)
        else:
            out = _na_layer(h, comb, mod, norm_mix[i], na_wqkv[j], na_rpb[j], na_wo[j], ffn, batch)
        h, v_slabs, part = out
        y_slabs, pos = _moe_experts(v_slabs, part, w1_b, w3_b, w2_b, i)
        comb = (y_slabs, mod, pos)
    return _final_norm(y_slabs, pos, h, mod, final_gain, batch).reshape(batch, SEQ, D_MODEL)
```

```python
import functools

import numpy as np
import jax
import jax.numpy as jnp
from jax import lax
from jax.experimental import pallas as pl
from jax.experimental.pallas import tpu as pltpu

F32 = jnp.float32
BF16 = jnp.bfloat16

D_MODEL = 2048
SEQ = 2048
GRID_W = 64
GRID_ROWS = SEQ // GRID_W
CTX_LEN = 256
TOK = CTX_LEN + SEQ
DEPTH = 4
N_MIXERS = 3
EPS = 1e-6
ROPE_THETA = 10000.0

GQA_HEADS = 16
GQA_KV_HEADS = 4
HEAD_DIM = 128
GQA_GROUP = GQA_HEADS // GQA_KV_HEADS

MLA_HEADS = 16
MLA_RANK = 512
MLA_NOPE = 128
MLA_ROPE = 64
MLA_QK = MLA_NOPE + MLA_ROPE

NA_HEADS = 16
NA_WIN_ROWS = 8
NA_WIN_COLS = 16
NA_Q_ROWS = 4
NA_K_ROWS = 12
NA_KEYS = NA_K_ROWS * GRID_W
NA_HEADS_PER_STEP = 8
MLA_HEADS_PER_STEP = 8
GQA_KV_PER_STEP = 2
KEY_CHUNK = 256

N_EXPERTS = 16
N_GROUPS = 4
GROUP_SIZE = N_EXPERTS // N_GROUPS
N_PAIRS = 6
N_PARTS = N_GROUPS * N_PAIRS
D_EXPERT = 512
PAIR_LO = (0, 0, 0, 1, 1, 2)
PAIR_HI = (1, 2, 3, 3, 2, 3)

LANES = 128
TM = 256
TILES_PER_BATCH = TOK // TM
SLAB_ROWS = D_MODEL // LANES
SLAB_PITCH = SLAB_ROWS + 1
SLAB_DMA_PRIORITY = 1
MOD_ROWS = 16
CTX_MOD_ROW = 8
NEG_BIG = -1e30
LOG2E = 1.4426950408889634
VMEM_LIMIT = 56 * 1024 * 1024


def _params():
    return pltpu.CompilerParams(vmem_limit_bytes=VMEM_LIMIT)


def _mod_row(i):
    return jnp.where(i % TILES_PER_BATCH == 0, CTX_MOD_ROW, i // TILES_PER_BATCH)


def _silu(x):
    return x / (1.0 + jnp.exp(-x))


def _mod_kernel(cond_ref, w_ref, b_ref, o_ref):
    s = _silu(cond_ref[...]).astype(BF16)
    o_ref[...] = jnp.dot(s, w_ref[...].astype(BF16), preferred_element_type=F32) + b_ref[...]


def _modulation(cond, mod_w, mod_b):
    tn = 1024
    n_out = mod_w.shape[-1]
    return pl.pallas_call(
        _mod_kernel,
        grid=(DEPTH, n_out // tn),
        in_specs=[
            pl.BlockSpec((MOD_ROWS, D_MODEL), lambda l, j: (0, 0)),
            pl.BlockSpec((None, D_MODEL, tn), lambda l, j: (l, 0, j)),
            pl.BlockSpec((None, 1, tn), lambda l, j: (l, 0, j)),
        ],
        out_specs=pl.BlockSpec((None, MOD_ROWS, tn), lambda l, j: (l, 0, j)),
        out_shape=jax.ShapeDtypeStruct((DEPTH, MOD_ROWS, n_out), F32),
        compiler_params=_params(),
        name="modulation",
    )(cond, mod_w, mod_b.reshape(DEPTH, 1, n_out))


def _rms(x, gain):
    return x * lax.rsqrt(jnp.mean(x * x, axis=-1, keepdims=True) + EPS) * gain


def _rope(x, c, s):
    return x * c + pltpu.roll(x, LANES // 2, 1) * s


def _runs(kinds):
    out, start = [], 0
    for k in range(1, len(kinds) + 1):
        if k == len(kinds) or kinds[k] != kinds[start] or kinds[start][0] in ("rope", "normrope", "vt"):
            out.append((start, k, kinds[start]))
            start = k
    return out


def _proj_kernel(*refs, prologue, kinds, chunk, has_tab, has_hgain, resid, has_vt, comb, first, ffn):
    it = iter(refs)
    if comb:
        pos_ref, y_hbm, hprev_ref, cgate_ref = next(it), next(it), next(it), next(it)
        cbuf, csem = refs[-2:]
    elif first:
        lat_ref, ctx_ref = next(it), next(it)
    else:
        x_ref = next(it)
    gain_ref = sc_ref = sh_ref = None
    if prologue == "normmod":
        gain_ref, sc_ref, sh_ref = next(it), next(it), next(it)
    elif prologue == "rms":
        gain_ref = next(it)
    w_ref = next(it)
    ctab_ref = stab_ref = hgain_ref = res_ref = gate_ref = None
    if has_tab:
        ctab_ref, stab_ref = next(it), next(it)
    if has_hgain:
        hgain_ref = next(it)
    if resid:
        res_ref, gate_ref = next(it), next(it)
    if ffn:
        fgain_ref, fsc_ref, fsh_ref, rw_ref, rb_ref = (next(it) for _ in range(5))
    o_ref = next(it)
    vt_ref = next(it) if has_vt else None
    hnew_ref = next(it) if (comb or first) else None
    if ffn:
        v_ref, part_ref = next(it), next(it)

    if first:
        is_ctx = pl.program_id(0) % TILES_PER_BATCH == 0
        hnew_ref[...] = jnp.where(is_ctx, ctx_ref[...], lat_ref[...])
        x_ref = hnew_ref
    if comb:
        ys = _slab_gather(pos_ref, y_hbm, cbuf, csem, pl.program_id(0), pl.num_programs(0), lambda s: s)
        for c in range(SLAB_ROWS):
            cols = slice(c * LANES, (c + 1) * LANES)
            hnew_ref[:, cols] = hprev_ref[:, cols] + cgate_ref[:, cols] * ys[_slab_rows(c), :]
        x_ref = hnew_ref

    if prologue == "none":
        u = x_ref[...]
    else:
        y = _rms(x_ref[...], gain_ref[...])
        if prologue == "normmod":
            y = y * (1.0 + sc_ref[...]) + sh_ref[...]
        u = y.astype(BF16)

    for c0 in range(0, len(kinds) * LANES, chunk):
        acc = jnp.dot(u, w_ref[:, c0:c0 + chunk], preferred_element_type=F32)
        b0 = c0 // LANES
        for lo, hi, kind in _runs(kinds[b0:b0 + chunk // LANES]):
            a = acc[:, lo * LANES:hi * LANES]
            cols = slice(c0 + lo * LANES, c0 + hi * LANES)
            if kind[0] == "scale":
                a = a * kind[1]
            elif kind[0] == "rope":
                t = kind[1]
                a = _rope(a, ctab_ref[:, t * LANES:(t + 1) * LANES], stab_ref[:, t * LANES:(t + 1) * LANES])
            elif kind[0] == "normrope":
                t, g = kind[1], kind[2]
                a = _rms(a, hgain_ref[g:g + 1, :])
                a = _rope(a, ctab_ref[:, t * LANES:(t + 1) * LANES], stab_ref[:, t * LANES:(t + 1) * LANES])
            elif kind[0] == "resid":
                a = res_ref[:, cols] + gate_ref[:, cols] * a
            elif kind[0] == "vt":
                vt_ref[kind[1] * LANES:(kind[1] + 1) * LANES, :] = a.T.astype(vt_ref.dtype)
                continue
            o_ref[:, cols] = a.astype(o_ref.dtype)

    if ffn:
        _ffn_prep_tile(o_ref[...], fgain_ref, fsc_ref, fsh_ref, rw_ref, rb_ref, v_ref, part_ref)


def _proj(x, w, *, out_dtype, kinds, prologue="none", x_col_block=0, gain=None, mod=None,
          mod_sc=None, mod_sh=None, tabs=None, hgain=None, res=None, mod_gate=None, chunk=512,
          comb=None, first=None, ffn=None):
    k, n_all = w.shape
    if first:
        n = first[0].shape[0] * TOK
    else:
        n = comb[1].shape[0] if comb else x.shape[0]
    n_vt = sum(kind[0] == "vt" for kind in kinds)
    n_out = n_all - n_vt * LANES
    assert n % TM == 0 and n_all % chunk == 0 and len(kinds) == n_all // LANES
    assert all(kind[0] == "vt" for kind in kinds[n_out // LANES:])
    prefetch, scratch = [], []
    if comb:
        y_slabs, h_prev, mod_prev, pos = comb
        prefetch = [pos]
        scratch = [pltpu.VMEM((2, TM * SLAB_PITCH, LANES), F32), pltpu.SemaphoreType.DMA((2,))]
        args = [y_slabs, h_prev, mod_prev]
        specs = [pl.BlockSpec(memory_space=pl.ANY),
                 pl.BlockSpec((TM, k), lambda i, *_: (i, 0)),
                 pl.BlockSpec((None, 1, k), lambda i, *_: (5 * MOD_ROWS + _mod_row(i), 0, 0))]
    elif first:
        args = list(first)
        specs = [pl.BlockSpec((None, TM, k),
                              lambda i, *_: (i // TILES_PER_BATCH, jnp.maximum(i % TILES_PER_BATCH - 1, 0), 0)),
                 pl.BlockSpec((None, TM, k), lambda i, *_: (i // TILES_PER_BATCH, 0, 0))]
    else:
        args = [x]
        specs = [pl.BlockSpec((TM, k), lambda i, *_: (i, x_col_block))]
    if prologue in ("normmod", "rms"):
        args.append(gain.reshape(1, k))
        specs.append(pl.BlockSpec((1, k), lambda i, *_: (0, 0)))
    if prologue == "normmod":
        for comp in (mod_sc, mod_sh):
            args.append(mod)
            specs.append(pl.BlockSpec((None, 1, k), lambda i, *_, comp=comp: (comp * MOD_ROWS + _mod_row(i), 0, 0)))
    args.append(w)
    specs.append(pl.BlockSpec((k, n_all), lambda i, *_: (0, 0), pipeline_mode=pl.Buffered(1)))
    if tabs is not None:
        for t in tabs:
            args.append(t)
            specs.append(pl.BlockSpec((TM, t.shape[1]), lambda i, *_: (i % TILES_PER_BATCH, 0)))
    if hgain is not None:
        args.append(hgain)
        specs.append(pl.BlockSpec(hgain.shape, lambda i, *_: (0, 0)))
    if res is not None:
        args.append(res)
        specs.append(pl.BlockSpec((TM, n_out), lambda i, *_: (i, 0)))
        args.append(mod)
        specs.append(pl.BlockSpec((None, 1, n_out), lambda i, *_: (mod_gate * MOD_ROWS + _mod_row(i), 0, 0)))
    if ffn:
        f_gain, router_wt, router_bias = ffn
        assert res is not None and n_out == D_MODEL
        args += [f_gain.reshape(1, D_MODEL), mod, mod, router_wt, router_bias.reshape(N_EXPERTS, 1)]
        specs += [
            pl.BlockSpec((1, D_MODEL), lambda i, *_: (0, 0)),
            pl.BlockSpec((None, 1, D_MODEL), lambda i, *_: (4 * MOD_ROWS + _mod_row(i), 0, 0)),
            pl.BlockSpec((None, 1, D_MODEL), lambda i, *_: (3 * MOD_ROWS + _mod_row(i), 0, 0)),
            pl.BlockSpec((D_MODEL, 2 * LANES), lambda i, *_: (0, 0)),
            pl.BlockSpec((N_EXPERTS, 1), lambda i, *_: (0, 0)),
        ]
    kern = functools.partial(_proj_kernel, prologue=prologue, kinds=tuple(kinds), chunk=chunk,
                             has_tab=tabs is not None, has_hgain=hgain is not None, resid=res is not None,
                             has_vt=n_vt > 0, comb=comb is not None, first=first is not None,
                             ffn=ffn is not None)
    out_specs = [pl.BlockSpec((TM, n_out), lambda i, *_: (i, 0))]
    out_shape = [jax.ShapeDtypeStruct((n, n_out), out_dtype)]
    if n_vt:
        out_specs.append(pl.BlockSpec((n_vt * LANES, TM), lambda i, *_: (0, i)))
        out_shape.append(jax.ShapeDtypeStruct((n_vt * LANES, n), out_dtype))
    if comb or first:
        out_specs.append(pl.BlockSpec((TM, k), lambda i, *_: (i, 0)))
        out_shape.append(jax.ShapeDtypeStruct((n, k), F32))
    if ffn:
        out_specs += [pl.BlockSpec((TM * SLAB_PITCH, LANES), lambda i, *_: (i, 0)), pl.BlockSpec((8, TM), lambda i, *_: (0, i))]
        out_shape += [jax.ShapeDtypeStruct((n * SLAB_PITCH, LANES), F32), jax.ShapeDtypeStruct((8, n), jnp.int32)]
    out = pl.pallas_call(
        kern,
        grid_spec=pltpu.PrefetchScalarGridSpec(
            num_scalar_prefetch=len(prefetch), grid=(n // TM,), in_specs=specs, out_specs=out_specs,
            scratch_shapes=scratch),
        out_shape=out_shape,
        compiler_params=_params(),
        name="proj",
    )(*prefetch, *args)
    return out if len(out) > 1 else out[0]


def _kq(k, q):
    return lax.dot_general(k, q, (((1,), (1,)), ((), ())), preferred_element_type=F32)


def _pipelined_heads(n_heads, n_chunks, q_of, k_of, vt_of, bias_of, s_ref, store):
    def scores(h, c, slot):
        s = _kq(k_of(h, c), q_of(h))
        b = bias_of(h, c)
        s_ref[slot, c] = s if b is None else s + b

    for c in range(n_chunks):
        scores(0, c, 0)
    for h in range(n_heads):
        slot = h % 2
        m = functools.reduce(jnp.maximum,
                             [jnp.max(s_ref[slot, c], axis=0, keepdims=True) for c in range(n_chunks)])
        num, den = None, None
        for c in range(n_chunks):
            if h + 1 < n_heads:
                scores(h + 1, c, 1 - slot)
            p = jnp.exp2(s_ref[slot, c] - m)
            l = jnp.sum(p, axis=0, keepdims=True)
            o = jnp.dot(vt_of(h, c), p.astype(BF16), preferred_element_type=F32)
            num = o if num is None else num + o
            den = l if den is None else den + l
        store(h, (num / den).T)


def _attn_kernel(*refs, heads, kv_heads, dq, has_kpe):
    if has_kpe:
        q_ref, k_ref, kpe_ref, vt_ref, o_ref, s_ref, keys_ref = refs
    else:
        q_ref, k_ref, vt_ref, o_ref, s_ref = refs
        keys_ref = None
    qt = pl.program_id(2)

    if has_kpe:
        @pl.when(qt == 0)
        def _():
            kpe = kpe_ref[...].astype(BF16)
            for h in range(heads):
                keys_ref[h, :, 0:HEAD_DIM] = k_ref[:, h * HEAD_DIM:(h + 1) * HEAD_DIM]
                keys_ref[h, :, HEAD_DIM:] = kpe

    def kv_of(h):
        return h // (heads // kv_heads)

    def q_of(h):
        return q_ref[:, h * dq:(h + 1) * dq]

    def k_of(h, c):
        rows = slice(c * KEY_CHUNK, (c + 1) * KEY_CHUNK)
        if has_kpe:
            return keys_ref[h, rows, :]
        return k_ref[rows, kv_of(h) * HEAD_DIM:(kv_of(h) + 1) * HEAD_DIM]

    def vt_of(h, c):
        return vt_ref[kv_of(h) * HEAD_DIM:(kv_of(h) + 1) * HEAD_DIM, c * KEY_CHUNK:(c + 1) * KEY_CHUNK]

    def store(h, o):
        o_ref[:, h * HEAD_DIM:(h + 1) * HEAD_DIM] = o.astype(o_ref.dtype)

    @pl.when(qt == 0)
    def _():
        _pipelined_heads(heads, CTX_LEN // KEY_CHUNK, q_of, k_of, vt_of, lambda h, c: None, s_ref, store)

    @pl.when(qt > 0)
    def _():
        _pipelined_heads(heads, TOK // KEY_CHUNK, q_of, k_of, vt_of, lambda h, c: None, s_ref, store)


def _attention(q_arr, k_arr, vt_arr, *, batch, groups, heads_per_step, kv_per_step, dq, k_blk0,
               kpe_arr=None, kpe_blk=0):
    shared_kv = kpe_arr is None
    kw = kv_per_step * HEAD_DIM
    ow = heads_per_step * HEAD_DIM
    args = [q_arr, k_arr]
    specs = [
        pl.BlockSpec((TM, heads_per_step * dq), lambda b, g, t: (b * TILES_PER_BATCH + t, g)),
        pl.BlockSpec((TOK, kw), lambda b, g, t: (b, k_blk0 + g)),
    ]
    scratch = [pltpu.VMEM((2, TOK // KEY_CHUNK, KEY_CHUNK, TM), F32)]
    if not shared_kv:
        args.append(kpe_arr)
        specs.append(pl.BlockSpec((TOK, LANES), lambda b, g, t: (b, kpe_blk)))
        scratch.append(pltpu.VMEM((heads_per_step, TOK, HEAD_DIM + LANES), BF16))
    args.append(vt_arr)
    specs.append(pl.BlockSpec((kw, TOK), lambda b, g, t: (g, b)))
    kern = functools.partial(_attn_kernel, heads=heads_per_step, kv_heads=kv_per_step, dq=dq,
                             has_kpe=not shared_kv)
    return pl.pallas_call(
        kern,
        grid=(batch, groups, TILES_PER_BATCH),
        in_specs=specs,
        out_specs=pl.BlockSpec((TM, ow), lambda b, g, t: (b * TILES_PER_BATCH + t, g)),
        out_shape=jax.ShapeDtypeStruct((batch * TOK, groups * ow), BF16),
        scratch_shapes=scratch,
        compiler_params=_params(),
        name="attention",
    )(*args)


def _na_kernel(q_ref, k_ref, vt_ref, tbl_ref, o_ref, s_ref):
    t = pl.program_id(2)
    n_lat_tiles = GRID_ROWS // NA_Q_ROWS
    k_row0 = jnp.clip(NA_Q_ROWS * t - NA_WIN_ROWS // 2, 0, GRID_ROWS - NA_K_ROWS)
    start = pl.multiple_of(CTX_LEN + k_row0 * GRID_W, KEY_CHUNK)
    which = jnp.where(t == 0, 0, jnp.where(t == n_lat_tiles - 1, 2, 1))

    def key_rows(c):
        return pl.ds(0, KEY_CHUNK) if c == 0 else pl.ds(start + (c - 1) * KEY_CHUNK, KEY_CHUNK)

    def q_of(h):
        return q_ref[:, h * HEAD_DIM:(h + 1) * HEAD_DIM]

    def k_of(h, c):
        return k_ref[key_rows(c), h * HEAD_DIM:(h + 1) * HEAD_DIM]

    def vt_of(h, c):
        return vt_ref[h * HEAD_DIM:(h + 1) * HEAD_DIM, key_rows(c)]

    def bias_of(h, c):
        return None if c == 0 else tbl_ref[h, which, (c - 1) * KEY_CHUNK:c * KEY_CHUNK, :]

    def store(h, o):
        o_ref[:, h * HEAD_DIM:(h + 1) * HEAD_DIM] = o.astype(o_ref.dtype)

    @pl.when(t == n_lat_tiles)
    def _():
        _pipelined_heads(NA_HEADS_PER_STEP, 1, q_of, k_of, vt_of, bias_of, s_ref, store)

    @pl.when(t < n_lat_tiles)
    def _():
        _pipelined_heads(NA_HEADS_PER_STEP, 1 + NA_KEYS // KEY_CHUNK, q_of, k_of, vt_of, bias_of, s_ref, store)


def _na_bias_table(rpb):
    cols = np.arange(GRID_W)
    c0 = np.clip(cols - NA_WIN_COLS // 2, 0, GRID_W - NA_WIN_COLS)
    col_ok = (cols[:, None] >= c0[None, :]) & (cols[:, None] < c0[None, :] + NA_WIN_COLS)
    cb = np.clip(cols[:, None] - cols[None, :] + NA_WIN_COLS - 1, 0, 2 * NA_WIN_COLS - 2)
    onehot = (cb[None] == np.arange(2 * NA_WIN_COLS - 1)[:, None, None]).astype(np.float32)
    planes = jnp.einsum("hdc,ckq->hdkq", rpb.astype(F32), onehot, precision=lax.Precision.HIGHEST)
    planes = jnp.where(col_ok[None, None], planes * LOG2E, NEG_BIG)
    n_rel = 2 * NA_WIN_ROWS - 1
    planes = jnp.concatenate([planes, jnp.full((NA_HEADS, 1, GRID_W, GRID_W), NEG_BIG, F32)], axis=1)
    n_lat_tiles = GRID_ROWS // NA_Q_ROWS
    which = np.full((3, NA_K_ROWS, NA_Q_ROWS), n_rel, np.int32)
    for kind, t in enumerate((0, 1, n_lat_tiles - 1)):
        k_row0 = int(np.clip(NA_Q_ROWS * t - NA_WIN_ROWS // 2, 0, GRID_ROWS - NA_K_ROWS))
        for kk in range(NA_K_ROWS):
            for i in range(NA_Q_ROWS):
                qr, kr = NA_Q_ROWS * t + i, k_row0 + kk
                r0 = int(np.clip(qr - NA_WIN_ROWS // 2, 0, GRID_ROWS - NA_WIN_ROWS))
                if r0 <= kr < r0 + NA_WIN_ROWS:
                    which[kind, kk, i] = kr - qr + NA_WIN_ROWS - 1
    tbl = jnp.concatenate([planes[:, which[:, :, i]] for i in range(NA_Q_ROWS)], axis=-1)
    return tbl.reshape(NA_HEADS, 3, NA_KEYS, TM)


def _na_attention(qk, vt, rpb, batch):
    tbl = _na_bias_table(rpb)
    n_lat_tiles = GRID_ROWS // NA_Q_ROWS

    def q_tile(b, t):
        return b * TILES_PER_BATCH + jnp.where(t == n_lat_tiles, 0, t + 1)

    gw = NA_HEADS_PER_STEP * HEAD_DIM
    n_groups = NA_HEADS // NA_HEADS_PER_STEP
    return pl.pallas_call(
        _na_kernel,
        grid=(n_groups, batch, n_lat_tiles + 1),
        in_specs=[
            pl.BlockSpec((TM, gw), lambda g, b, t: (q_tile(b, t), g)),
            pl.BlockSpec((TOK, gw), lambda g, b, t: (b, n_groups + g)),
            pl.BlockSpec((gw, TOK), lambda g, b, t: (g, b)),
            pl.BlockSpec((NA_HEADS_PER_STEP, 3, NA_KEYS, TM), lambda g, b, t: (g, 0, 0, 0),
                         pipeline_mode=pl.Buffered(1)),
        ],
        out_specs=pl.BlockSpec((TM, gw), lambda g, b, t: (q_tile(b, t), g)),
        out_shape=jax.ShapeDtypeStruct((batch * TOK, NA_HEADS * HEAD_DIM), BF16),
        scratch_shapes=[pltpu.VMEM((2, 1 + NA_KEYS // KEY_CHUNK, KEY_CHUNK, TM), F32)],
        compiler_params=_params(),
        name="na_attention",
    )(qk, qk, vt, tbl)


def _route(logits, bias):
    scores = 1.0 / (1.0 + jnp.exp(-logits))
    sel = scores + bias
    srow = [sel[e:e + 1, :] for e in range(N_EXPERTS)]
    crow = [scores[e:e + 1, :] for e in range(N_EXPERTS)]
    gscore = []
    for g in range(N_GROUPS):
        a, b, c, d = srow[GROUP_SIZE * g:GROUP_SIZE * (g + 1)]
        gscore.append(jnp.maximum(jnp.maximum(jnp.maximum(a + b, a + c), jnp.maximum(a + d, b + c)),
                                  jnp.maximum(b + d, c + d)))
    best, gidx = gscore[0], jnp.zeros(gscore[0].shape, jnp.int32)
    for g in range(1, N_GROUPS):
        better = gscore[g] > best
        gidx = jnp.where(better, g, gidx)
        best = jnp.where(better, gscore[g], best)
    s, c = [], []
    for j in range(GROUP_SIZE):
        sj, cj = srow[j], crow[j]
        for g in range(1, N_GROUPS):
            sj = jnp.where(gidx == g, srow[GROUP_SIZE * g + j], sj)
            cj = jnp.where(gidx == g, crow[GROUP_SIZE * g + j], cj)
        s.append(sj)
        c.append(cj)
    chosen = []
    for j in range(GROUP_SIZE):
        rank = jnp.zeros(s[j].shape, jnp.int32)
        for k in range(GROUP_SIZE):
            if k < j:
                rank = rank + jnp.where(s[k] >= s[j], 1, 0)
            elif k > j:
                rank = rank + jnp.where(s[k] > s[j], 1, 0)
        chosen.append(rank < 2)
    w = [jnp.where(chosen[j], c[j], 0.0) for j in range(GROUP_SIZE)]
    denom = (w[0] + w[1]) + (w[2] + w[3])
    gate = [w[j] / denom for j in range(GROUP_SIZE)]
    lo = jnp.where(chosen[0], 0, jnp.where(chosen[1], 1, 2))
    hi = jnp.where(chosen[3], 3, jnp.where(chosen[2], 2, 1))
    g_lo = jnp.where(chosen[0], gate[0], jnp.where(chosen[1], gate[1], gate[2]))
    g_hi = jnp.where(chosen[3], gate[3], jnp.where(chosen[2], gate[2], gate[1]))
    pair = jnp.where(lo == 0, hi - 1, jnp.where(lo == 1, 6 - hi, N_PAIRS - 1))
    return gidx * N_PAIRS + pair, g_lo, g_hi


def _slab_rows(c):
    return pl.ds(c, TM, stride=SLAB_PITCH)


def _store_slabs(ref, val, extra):
    for c in range(SLAB_ROWS):
        ref[_slab_rows(c), :] = val[:, c * LANES:(c + 1) * LANES]
    ref[_slab_rows(SLAB_ROWS), :] = extra


def _ffn_prep_tile(h, gain_ref, sc_ref, sh_ref, rw_ref, rb_ref, v_ref, part_ref):
    v = _rms(h, gain_ref[...]) * (1.0 + sc_ref[...]) + sh_ref[...]
    v_hi = v.astype(BF16)
    v_lo = (v - v_hi.astype(F32)).astype(BF16)
    a = jnp.dot(v_hi, rw_ref[...], preferred_element_type=F32)
    b = jnp.dot(v_lo, rw_ref[:, 0:LANES], preferred_element_type=F32)
    logits = (a[:, 0:LANES] + a[:, LANES:] + b).T[0:N_EXPERTS, :]
    part, g_lo, g_hi = _route(logits, rb_ref[...])
    part_ref[...] = jnp.zeros(part_ref.shape, jnp.int32)
    part_ref[0:1, :] = part
    row = lax.broadcasted_iota(jnp.int32, (LANES, TM), 0)
    gates_t = jnp.where(row == 0, g_lo, jnp.where(row == 1, g_hi, 0.0))
    _store_slabs(v_ref, v, gates_t.T)


def _dispatch_meta(part, n_slots):
    n = part.shape[0]
    onehot = (part[:, None] == jnp.arange(N_PARTS, dtype=jnp.int32)[None, :]).astype(jnp.int32)
    csum = jnp.cumsum(onehot, axis=0)
    rank = jnp.take_along_axis(csum, part[:, None], axis=1)[:, 0] - 1
    counts = csum[-1]
    padded = ((counts + TM - 1) // TM) * TM
    ends = jnp.cumsum(padded)
    starts = ends - padded
    pos = (starts[part] + rank).astype(jnp.int32)
    src = jnp.zeros((n_slots,), jnp.int32).at[pos].set(jnp.arange(n, dtype=jnp.int32))
    tile_start = jnp.arange(n_slots // TM, dtype=jnp.int32) * TM
    last_start = jnp.minimum(tile_start, ends[-1] - TM)
    tile_part = jnp.sum((ends[None, :] <= last_start[:, None]).astype(jnp.int32), axis=1)
    n_rows = jnp.clip(counts[tile_part] - (tile_start - starts[tile_part]), 0, TM).astype(jnp.int32)
    group, pair = tile_part // N_PAIRS, tile_part % N_PAIRS
    e_lo = group * GROUP_SIZE + jnp.asarray(PAIR_LO, jnp.int32)[pair]
    e_hi = group * GROUP_SIZE + jnp.asarray(PAIR_HI, jnp.int32)[pair]
    return pos, src, e_lo, e_hi, n_rows


def _slab_gather(pos_ref, ys_hbm, cbuf, csem, step, n_steps, tile_of):
    slot = step % 2

    def copy(slot_, r, row):
        return pltpu.make_async_copy(ys_hbm.at[pl.ds(row * SLAB_PITCH, SLAB_PITCH), :],
                                     cbuf.at[slot_, pl.ds(r * SLAB_PITCH, SLAB_PITCH), :], csem.at[slot_])

    def start(s, slot_):
        base = tile_of(s) * TM
        for r in range(TM):
            copy(slot_, r, pos_ref[base + r]).start(priority=SLAB_DMA_PRIORITY)

    @pl.when(step == 0)
    def _():
        start(0, 0)

    for r in range(TM):
        copy(slot, r, 0).wait()

    @pl.when(step + 1 < n_steps)
    def _():
        start(step + 1, 1 - slot)

    return cbuf.at[slot]


def _expert_kernel(src_ref, elo_ref, ehi_ref, nrows_ref, v_hbm, w1a, w3a, w2a, w1b, w3b, w2b, y_ref,
                   xbuf, x2d, in_sem, *, prio):
    t = pl.program_id(0)
    slot = t % 2

    def in_copy(slot_, r, tok):
        return pltpu.make_async_copy(v_hbm.at[pl.ds(tok * SLAB_PITCH, SLAB_PITCH), :],
                                     xbuf.at[slot_, pl.ds(r * SLAB_PITCH, SLAB_PITCH), :], in_sem.at[slot_])

    def start_in(tile, slot_):
        for r in range(TM):
            in_copy(slot_, r, src_ref[tile * TM + r]).start(priority=(r % 2 if prio is None else prio))

    n_prev = jnp.where(t > 0, nrows_ref[jnp.maximum(t - 1, 0)], 0)

    @pl.when(t == 0)
    def _():
        start_in(0, 0)

    @pl.when(jnp.logical_or(t == 0, n_prev > 0))
    def _():
        for r in range(TM):
            in_copy(slot, r, 0).wait()

    @pl.when(nrows_ref[t] > 0)
    def _():
        start_in(t + 1, 1 - slot)

    @pl.when(nrows_ref[t] > 0)
    def _():
        xs = xbuf.at[slot]
        for c in range(SLAB_ROWS):
            x2d[:, c * LANES:(c + 1) * LANES] = xs[_slab_rows(c), :].astype(BF16)
        g = xs[_slab_rows(SLAB_ROWS), :]
        x = x2d[...]

        def hidden(w1, w3, gate):
            a = jnp.dot(x, w1[...], preferred_element_type=F32)
            b = jnp.dot(x, w3[...], preferred_element_type=F32)
            return (_silu(a) * b * gate).astype(BF16)

        y = jnp.dot(hidden(w1a, w3a, g[:, 0:1]), w2a[...], preferred_element_type=F32)
        y = y + jnp.dot(hidden(w1b, w3b, g[:, 1:2]), w2b[...], preferred_element_type=F32)
        _store_slabs(y_ref, y, jnp.zeros((TM, LANES), F32))

    @pl.when(nrows_ref[t] == 0)
    def _():
        y_ref[...] = jnp.zeros(y_ref.shape, y_ref.dtype)


def _experts(v_slabs, src, e_lo, e_hi, n_rows, w1, w3, w2, layer):
    n_tiles = n_rows.shape[0]
    up = lambda sel: pl.BlockSpec((None, None, D_MODEL, D_EXPERT),
                                  lambda t, s, lo, hi, nr: (layer, (lo, hi)[sel][t], 0, 0))
    down = lambda sel: pl.BlockSpec((None, None, D_EXPERT, D_MODEL),
                                    lambda t, s, lo, hi, nr: (layer, (lo, hi)[sel][t], 0, 0))
    return pl.pallas_call(
        functools.partial(_expert_kernel, prio=(1, 0, None, 1)[layer]),
        grid_spec=pltpu.PrefetchScalarGridSpec(
            num_scalar_prefetch=4,
            grid=(n_tiles,),
            in_specs=[pl.BlockSpec(memory_space=pl.ANY), up(0), up(0), down(0), up(1), up(1), down(1)],
            out_specs=pl.BlockSpec((TM * SLAB_PITCH, LANES), lambda t, s, lo, hi, nr: (t, 0)),
            scratch_shapes=[pltpu.VMEM((2, TM * SLAB_PITCH, LANES), F32), pltpu.VMEM((TM, D_MODEL), BF16),
                            pltpu.SemaphoreType.DMA((2,))],
        ),
        out_shape=jax.ShapeDtypeStruct((n_tiles * TM * SLAB_PITCH, LANES), F32),
        compiler_params=_params(),
        name="experts",
    )(src, e_lo, e_hi, n_rows, v_slabs, w1, w3, w2, w1, w3, w2)


def _moe_experts(v_slabs, part, w1, w3, w2, layer):
    n = part.shape[1]
    pos, src, e_lo, e_hi, n_rows = _dispatch_meta(part[0], n + (N_PARTS + 1) * TM)
    return _experts(v_slabs, src, e_lo, e_hi, n_rows, w1, w3, w2, layer), pos


def _latent_tile(j):
    lat_tiles = SEQ // TM
    return (j // lat_tiles) * TILES_PER_BATCH + 1 + j % lat_tiles


def _final_kernel(pos_ref, y_hbm, h_ref, gate_ref, gain_ref, o_ref, hn_ref, cbuf, csem):
    ys = _slab_gather(pos_ref, y_hbm, cbuf, csem, pl.program_id(0), pl.num_programs(0), _latent_tile)
    for c in range(SLAB_ROWS):
        cols = slice(c * LANES, (c + 1) * LANES)
        hn_ref[:, cols] = h_ref[:, cols] + gate_ref[:, cols] * ys[_slab_rows(c), :]
    o_ref[...] = _rms(hn_ref[...], gain_ref[...])


def _final_norm(y_slabs, pos, h, mod, gain, batch):
    lat_tiles = SEQ // TM
    return pl.pallas_call(
        _final_kernel,
        grid_spec=pltpu.PrefetchScalarGridSpec(
            num_scalar_prefetch=1,
            grid=(batch * lat_tiles,),
            in_specs=[
                pl.BlockSpec(memory_space=pl.ANY),
                pl.BlockSpec((TM, D_MODEL), lambda j, p: (_latent_tile(j), 0)),
                pl.BlockSpec((None, 1, D_MODEL), lambda j, p: (5 * MOD_ROWS + j // lat_tiles, 0, 0)),
                pl.BlockSpec((1, D_MODEL), lambda j, p: (0, 0)),
            ],
            out_specs=pl.BlockSpec((TM, D_MODEL), lambda j, p: (j, 0)),
            scratch_shapes=[pltpu.VMEM((TM, D_MODEL), F32), pltpu.VMEM((2, TM * SLAB_PITCH, LANES), F32),
                            pltpu.SemaphoreType.DMA((2,))],
        ),
        out_shape=jax.ShapeDtypeStruct((batch * SEQ, D_MODEL), F32),
        compiler_params=_params(),
        name="final_norm",
    )(pos, y_slabs, h, mod, gain.reshape(1, D_MODEL))


def _rope_tables(rot_dim, layout):
    n_freq = rot_dim // 4
    inv_freq = 1.0 / (ROPE_THETA ** (jnp.arange(n_freq, dtype=F32) / n_freq))
    t = jnp.arange(SEQ, dtype=jnp.int32)
    row = (t // GRID_W).astype(F32)
    col = (t % GRID_W).astype(F32)
    ang = jnp.concatenate([row[:, None] * inv_freq, col[:, None] * inv_freq], axis=-1)
    cos, sin = jnp.cos(ang), jnp.sin(ang)
    if layout == "padded":
        pad1, pad0 = jnp.ones_like(cos), jnp.zeros_like(sin)
        c = jnp.concatenate([cos, pad1, cos, pad1], axis=-1)
        s = jnp.concatenate([-sin, pad0, sin, pad0], axis=-1)
    else:
        c = jnp.concatenate([cos, cos], axis=-1)
        s = jnp.concatenate([-sin, sin], axis=-1)
    c = jnp.concatenate([jnp.ones((CTX_LEN, LANES), F32), c], axis=0)
    s = jnp.concatenate([jnp.zeros((CTX_LEN, LANES), F32), s], axis=0)
    return c, s


def _pad_rope_cols(w):
    half = MLA_ROPE // 2
    z = jnp.zeros(w.shape[:-1] + (half,), w.dtype)
    return jnp.concatenate([w[..., :half], z, w[..., half:], z], axis=-1)


def _first_proj(h, comb, w, **kw):
    if comb is None:
        out = _proj(None, w, first=h, **kw)
    else:
        y_slabs, mod_prev, pos = comb
        out = _proj(None, w, comb=(y_slabs, h, mod_prev, pos), **kw)
    if kw["kinds"][-1][0] == "vt":
        return out[:2], out[2]
    return out[0], out[1]


def _out_proj(o, w_o, h, mod, ffn):
    return _proj(o, w_o.astype(BF16), out_dtype=F32, kinds=[("resid",)] * (D_MODEL // LANES),
                 res=h, mod=mod, mod_gate=2, ffn=ffn)


def _gqa_layer(h, comb, mod, norm_gain, w_qkv, q_gain, k_gain, w_o, tabs, ffn, batch):
    kinds = ([("normrope", 0, 0)] * GQA_HEADS + [("normrope", 1, 1)] * GQA_KV_HEADS
             + [("vt", j) for j in range(GQA_KV_HEADS)])
    (qk, vt), h = _first_proj(h, comb, w_qkv.astype(BF16), out_dtype=BF16, kinds=kinds, prologue="normmod",
                              gain=norm_gain, mod=mod, mod_sc=1, mod_sh=0, tabs=tabs,
                              hgain=jnp.stack([q_gain, k_gain]))
    o = _attention(qk, qk, vt, batch=batch, groups=GQA_KV_HEADS // GQA_KV_PER_STEP,
                   heads_per_step=GQA_GROUP * GQA_KV_PER_STEP, kv_per_step=GQA_KV_PER_STEP, dq=HEAD_DIM,
                   k_blk0=GQA_HEADS // GQA_KV_PER_STEP)
    return _out_proj(o, w_o, h, mod, ffn)


def _mla_layer(h, comb, mod, norm_gain, w_a, q_gain, kv_gain, w_uq, w_ukv, w_o, tabs, ffn, batch):
    n_lat = 2 * MLA_RANK // LANES
    w_a_p = jnp.concatenate([w_a[:, :2 * MLA_RANK], _pad_rope_cols(w_a[:, 2 * MLA_RANK:])], axis=1)
    a, h = _first_proj(h, comb, w_a_p.astype(BF16), out_dtype=F32, kinds=[("plain",)] * n_lat + [("rope", 1)],
                       prologue="normmod", gain=norm_gain, mod=mod, mod_sc=1, mod_sh=0, tabs=tabs,
                       chunk=LANES * (n_lat + 1))
    scale = MLA_QK ** -0.5 * LOG2E
    w_uq_h = w_uq.reshape(MLA_RANK, MLA_HEADS, MLA_QK)
    w_uq_p = jnp.concatenate([w_uq_h[..., :MLA_NOPE], _pad_rope_cols(w_uq_h[..., MLA_NOPE:])], axis=-1)
    q = _proj(a, w_uq_p.reshape(MLA_RANK, MLA_HEADS * 2 * LANES).astype(BF16), out_dtype=BF16,
              kinds=[("scale", scale), ("rope", 0)] * MLA_HEADS, prologue="rms", x_col_block=0, gain=q_gain, tabs=tabs)
    w_ukv_h = w_ukv.reshape(MLA_RANK, MLA_HEADS, 2 * HEAD_DIM)
    w_ukv_p = jnp.concatenate([w_ukv_h[..., :MLA_NOPE].reshape(MLA_RANK, -1),
                               w_ukv_h[..., MLA_NOPE:].reshape(MLA_RANK, -1)], axis=1)
    k_nope, vt = _proj(a, w_ukv_p.astype(BF16), out_dtype=BF16,
                       kinds=[("plain",)] * MLA_HEADS + [("vt", j) for j in range(MLA_HEADS)],
                       prologue="rms", x_col_block=1, gain=kv_gain)
    o = _attention(q, k_nope, vt, batch=batch, groups=MLA_HEADS // MLA_HEADS_PER_STEP,
                   heads_per_step=MLA_HEADS_PER_STEP, kv_per_step=MLA_HEADS_PER_STEP, dq=2 * LANES, k_blk0=0,
                   kpe_arr=a, kpe_blk=n_lat)
    return _out_proj(o, w_o, h, mod, ffn)


def _na_layer(h, comb, mod, norm_gain, w_qkv, rpb, w_o, ffn, batch):
    scale = HEAD_DIM ** -0.5 * LOG2E
    kinds = [("scale", scale)] * NA_HEADS + [("plain",)] * NA_HEADS + [("vt", j) for j in range(NA_HEADS)]
    (qk, vt), h = _first_proj(h, comb, w_qkv.astype(BF16), out_dtype=BF16, kinds=kinds, prologue="normmod",
                              gain=norm_gain, mod=mod, mod_sc=1, mod_sh=0)
    o = _na_attention(qk, vt, rpb, batch)
    return _out_proj(o, w_o, h, mod, ffn)


def kernel(x, c, ctx, c_ctx, mod_w, mod_b, norm_mix, norm_ffn, gqa_wqkv, gqa_q_gain, gqa_k_gain, gqa_wo,
           mla_wa, mla_q_gain, mla_kv_gain, mla_wuq, mla_wukv, mla_wo, na_wqkv, na_rpb, na_wo,
           router_w, router_bias, moe_w1, moe_w3, moe_w2, final_gain):
    batch = x.shape[0]
    assert batch <= CTX_MOD_ROW and x.shape[1:] == (SEQ, D_MODEL) and ctx.shape[1:] == (CTX_LEN, D_MODEL)
    h = (x, ctx)
    cond = jnp.zeros((MOD_ROWS, D_MODEL), F32).at[:batch].set(c).at[CTX_MOD_ROW].set(c_ctx)
    mods = _modulation(cond, mod_w, mod_b)
    mods = mods.reshape(DEPTH, MOD_ROWS, 6, D_MODEL).transpose(0, 2, 1, 3).reshape(DEPTH, 6 * MOD_ROWS, 1, D_MODEL)

    c_full, s_full = _rope_tables(HEAD_DIM, "full")
    gqa_scale = HEAD_DIM ** -0.5 * LOG2E
    gqa_tabs = (jnp.concatenate([c_full * gqa_scale, c_full], axis=1),
                jnp.concatenate([s_full * gqa_scale, s_full], axis=1))
    c_pad, s_pad = _rope_tables(MLA_ROPE, "padded")
    mla_scale = MLA_QK ** -0.5 * LOG2E
    mla_tabs = (jnp.concatenate([c_pad * mla_scale, c_pad], axis=1),
                jnp.concatenate([s_pad * mla_scale, s_pad], axis=1))
    rw_hi = router_w.astype(BF16)
    rw_lo = (router_w - rw_hi.astype(F32)).astype(BF16)
    lane_pad = ((0, 0), (0, LANES - N_EXPERTS))
    router_wt = jnp.concatenate([jnp.pad(rw_hi, lane_pad), jnp.pad(rw_lo, lane_pad)], axis=1)
    w1_b, w3_b, w2_b = moe_w1.astype(BF16), moe_w3.astype(BF16), moe_w2.astype(BF16)

    comb = None
    for i in range(DEPTH):
        mod = mods[i]
        ffn = (norm_ffn[i], router_wt, router_bias)
        kind, j = i % N_MIXERS, i // N_MIXERS
        if kind == 0:
            out = _gqa_layer(h, comb, mod, norm_mix[i], gqa_wqkv[j], gqa_q_gain[j], gqa_k_gain[j], gqa_wo[j],
                             gqa_tabs, ffn, batch)
        elif kind == 1:
            out = _mla_layer(h, comb, mod, norm_mix[i], mla_wa[j], mla_q_gain[j], mla_kv_gain[j], mla_wuq[j],
                             mla_wukv[j], mla_wo[j], mla_tabs, ffn, batch)
        else:
            out = _na_layer(h, comb, mod, norm_mix[i], na_wqkv[j], na_rpb[j], na_wo[j], ffn, batch)
        h, v_slabs, part = out
        y_slabs, pos = _moe_experts(v_slabs, part, w1_b, w3_b, w2_b, i)
        comb = (y_slabs, mod, pos)
    return _final_norm(y_slabs, pos, h, mod, final_gain, batch).reshape(batch, SEQ, D_MODEL)
```

```python
import functools

import numpy as np
import jax
import jax.numpy as jnp
from jax import lax
from jax.experimental import pallas as pl
from jax.experimental.pallas import tpu as pltpu

F32 = jnp.float32
BF16 = jnp.bfloat16

D_MODEL = 2048
SEQ = 2048
GRID_W = 64
GRID_ROWS = SEQ // GRID_W
CTX_LEN = 256
TOK = CTX_LEN + SEQ
DEPTH = 4
N_MIXERS = 3
EPS = 1e-6
ROPE_THETA = 10000.0

GQA_HEADS = 16
GQA_KV_HEADS = 4
HEAD_DIM = 128
GQA_GROUP = GQA_HEADS // GQA_KV_HEADS

MLA_HEADS = 16
MLA_RANK = 512
MLA_NOPE = 128
MLA_ROPE = 64
MLA_QK = MLA_NOPE + MLA_ROPE

NA_HEADS = 16
NA_WIN_ROWS = 8
NA_WIN_COLS = 16
NA_Q_ROWS = 4
NA_K_ROWS = 12
NA_KEYS = NA_K_ROWS * GRID_W
NA_HEADS_PER_STEP = 8
MLA_HEADS_PER_STEP = 8
GQA_KV_PER_STEP = 2
KEY_CHUNK = 256

N_EXPERTS = 16
N_GROUPS = 4
GROUP_SIZE = N_EXPERTS // N_GROUPS
N_PAIRS = 6
N_PARTS = N_GROUPS * N_PAIRS
D_EXPERT = 512
PAIR_LO = (0, 0, 0, 1, 1, 2)
PAIR_HI = (1, 2, 3, 3, 2, 3)

LANES = 128
TM = 256
TILES_PER_BATCH = TOK // TM
SLAB_ROWS = D_MODEL // LANES
SLAB_PITCH = SLAB_ROWS + 1
SLAB_DMA_PRIORITY = 1
MOD_ROWS = 16
CTX_MOD_ROW = 8
NEG_BIG = -1e30
LOG2E = 1.4426950408889634
VMEM_LIMIT = 56 * 1024 * 1024


def _params():
    return pltpu.CompilerParams(vmem_limit_bytes=VMEM_LIMIT)


def _mod_row(i):
    return jnp.where(i % TILES_PER_BATCH == 0, CTX_MOD_ROW, i // TILES_PER_BATCH)


def _silu(x):
    return x / (1.0 + jnp.exp(-x))


def _mod_kernel(cond_ref, w_ref, b_ref, o_ref):
    s = _silu(cond_ref[...]).astype(BF16)
    o_ref[...] = jnp.dot(s, w_ref[...].astype(BF16), preferred_element_type=F32) + b_ref[...]


def _modulation(cond, mod_w, mod_b):
    tn = 1024
    n_out = mod_w.shape[-1]
    return pl.pallas_call(
        _mod_kernel,
        grid=(DEPTH, n_out // tn),
        in_specs=[
            pl.BlockSpec((MOD_ROWS, D_MODEL), lambda l, j: (0, 0)),
            pl.BlockSpec((None, D_MODEL, tn), lambda l, j: (l, 0, j)),
            pl.BlockSpec((None, 1, tn), lambda l, j: (l, 0, j)),
        ],
        out_specs=pl.BlockSpec((None, MOD_ROWS, tn), lambda l, j: (l, 0, j)),
        out_shape=jax.ShapeDtypeStruct((DEPTH, MOD_ROWS, n_out), F32),
        compiler_params=_params(),
        name="modulation",
    )(cond, mod_w, mod_b.reshape(DEPTH, 1, n_out))


def _rms(x, gain):
    return x * lax.rsqrt(jnp.mean(x * x, axis=-1, keepdims=True) + EPS) * gain


def _rope(x, c, s):
    return x * c + pltpu.roll(x, LANES // 2, 1) * s


def _runs(kinds):
    out, start = [], 0
    for k in range(1, len(kinds) + 1):
        if k == len(kinds) or kinds[k] != kinds[start] or kinds[start][0] in ("rope", "normrope", "vt"):
            out.append((start, k, kinds[start]))
            start = k
    return out


def _proj_kernel(*refs, prologue, kinds, chunk, has_tab, has_hgain, resid, has_vt, comb, first, ffn):
    it = iter(refs)
    if comb:
        pos_ref, y_hbm, hprev_ref, cgate_ref = next(it), next(it), next(it), next(it)
        cbuf, csem = refs[-2:]
    elif first:
        lat_ref, ctx_ref = next(it), next(it)
    else:
        x_ref = next(it)
    gain_ref = sc_ref = sh_ref = None
    if prologue == "normmod":
        gain_ref, sc_ref, sh_ref = next(it), next(it), next(it)
    elif prologue == "rms":
        gain_ref = next(it)
    w_ref = next(it)
    ctab_ref = stab_ref = hgain_ref = res_ref = gate_ref = None
    if has_tab:
        ctab_ref, stab_ref = next(it), next(it)
    if has_hgain:
        hgain_ref = next(it)
    if resid:
        res_ref, gate_ref = next(it), next(it)
    if ffn:
        fgain_ref, fsc_ref, fsh_ref, rw_ref, rb_ref = (next(it) for _ in range(5))
    o_ref = next(it)
    vt_ref = next(it) if has_vt else None
    hnew_ref = next(it) if (comb or first) else None
    if ffn:
        v_ref, part_ref = next(it), next(it)

    if first:
        is_ctx = pl.program_id(0) % TILES_PER_BATCH == 0
        hnew_ref[...] = jnp.where(is_ctx, ctx_ref[...], lat_ref[...])
        x_ref = hnew_ref
    if comb:
        ys = _slab_gather(pos_ref, y_hbm, cbuf, csem, pl.program_id(0), pl.num_programs(0), lambda s: s)
        for c in range(SLAB_ROWS):
            cols = slice(c * LANES, (c + 1) * LANES)
            hnew_ref[:, cols] = hprev_ref[:, cols] + cgate_ref[:, cols] * ys[_slab_rows(c), :]
        x_ref = hnew_ref

    if prologue == "none":
        u = x_ref[...]
    else:
        y = _rms(x_ref[...], gain_ref[...])
        if prologue == "normmod":
            y = y * (1.0 + sc_ref[...]) + sh_ref[...]
        u = y.astype(BF16)

    for c0 in range(0, len(kinds) * LANES, chunk):
        acc = jnp.dot(u, w_ref[:, c0:c0 + chunk], preferred_element_type=F32)
        b0 = c0 // LANES
        for lo, hi, kind in _runs(kinds[b0:b0 + chunk // LANES]):
            a = acc[:, lo * LANES:hi * LANES]
            cols = slice(c0 + lo * LANES, c0 + hi * LANES)
            if kind[0] == "scale":
                a = a * kind[1]
            elif kind[0] == "rope":
                t = kind[1]
                a = _rope(a, ctab_ref[:, t * LANES:(t + 1) * LANES], stab_ref[:, t * LANES:(t + 1) * LANES])
            elif kind[0] == "normrope":
                t, g = kind[1], kind[2]
                a = _rms(a, hgain_ref[g:g + 1, :])
                a = _rope(a, ctab_ref[:, t * LANES:(t + 1) * LANES], stab_ref[:, t * LANES:(t + 1) * LANES])
            elif kind[0] == "resid":
                a = res_ref[:, cols] + gate_ref[:, cols] * a
            elif kind[0] == "vt":
                vt_ref[kind[1] * LANES:(kind[1] + 1) * LANES, :] = a.T.astype(vt_ref.dtype)
                continue
            o_ref[:, cols] = a.astype(o_ref.dtype)

    if ffn:
        _ffn_prep_tile(o_ref[...], fgain_ref, fsc_ref, fsh_ref, rw_ref, rb_ref, v_ref, part_ref)


def _proj(x, w, *, out_dtype, kinds, prologue="none", x_col_block=0, gain=None, mod=None,
          mod_sc=None, mod_sh=None, tabs=None, hgain=None, res=None, mod_gate=None, chunk=512,
          comb=None, first=None, ffn=None):
    k, n_all = w.shape
    if first:
        n = first[0].shape[0] * TOK
    else:
        n = comb[1].shape[0] if comb else x.shape[0]
    n_vt = sum(kind[0] == "vt" for kind in kinds)
    n_out = n_all - n_vt * LANES
    assert n % TM == 0 and n_all % chunk == 0 and len(kinds) == n_all // LANES
    assert all(kind[0] == "vt" for kind in kinds[n_out // LANES:])
    prefetch, scratch = [], []
    if comb:
        y_slabs, h_prev, mod_prev, pos = comb
        prefetch = [pos]
        scratch = [pltpu.VMEM((2, TM * SLAB_PITCH, LANES), F32), pltpu.SemaphoreType.DMA((2,))]
        args = [y_slabs, h_prev, mod_prev]
        specs = [pl.BlockSpec(memory_space=pl.ANY),
                 pl.BlockSpec((TM, k), lambda i, *_: (i, 0)),
                 pl.BlockSpec((None, 1, k), lambda i, *_: (5 * MOD_ROWS + _mod_row(i), 0, 0))]
    elif first:
        args = list(first)
        specs = [pl.BlockSpec((None, TM, k),
                              lambda i, *_: (i // TILES_PER_BATCH, jnp.maximum(i % TILES_PER_BATCH - 1, 0), 0)),
                 pl.BlockSpec((None, TM, k), lambda i, *_: (i // TILES_PER_BATCH, 0, 0))]
    else:
        args = [x]
        specs = [pl.BlockSpec((TM, k), lambda i, *_: (i, x_col_block))]
    if prologue in ("normmod", "rms"):
        args.append(gain.reshape(1, k))
        specs.append(pl.BlockSpec((1, k), lambda i, *_: (0, 0)))
    if prologue == "normmod":
        for comp in (mod_sc, mod_sh):
            args.append(mod)
            specs.append(pl.BlockSpec((None, 1, k), lambda i, *_, comp=comp: (comp * MOD_ROWS + _mod_row(i), 0, 0)))
    args.append(w)
    specs.append(pl.BlockSpec((k, n_all), lambda i, *_: (0, 0), pipeline_mode=pl.Buffered(1)))
    if tabs is not None:
        for t in tabs:
            args.append(t)
            specs.append(pl.BlockSpec((TM, t.shape[1]), lambda i, *_: (i % TILES_PER_BATCH, 0)))
    if hgain is not None:
        args.append(hgain)
        specs.append(pl.BlockSpec(hgain.shape, lambda i, *_: (0, 0)))
    if res is not None:
        args.append(res)
        specs.append(pl.BlockSpec((TM, n_out), lambda i, *_: (i, 0)))
        args.append(mod)
        specs.append(pl.BlockSpec((None, 1, n_out), lambda i, *_: (mod_gate * MOD_ROWS + _mod_row(i), 0, 0)))
    if ffn:
        f_gain, router_wt, router_bias = ffn
        assert res is not None and n_out == D_MODEL
        args += [f_gain.reshape(1, D_MODEL), mod, mod, router_wt, router_bias.reshape(N_EXPERTS, 1)]
        specs += [
            pl.BlockSpec((1, D_MODEL), lambda i, *_: (0, 0)),
            pl.BlockSpec((None, 1, D_MODEL), lambda i, *_: (4 * MOD_ROWS + _mod_row(i), 0, 0)),
            pl.BlockSpec((None, 1, D_MODEL), lambda i, *_: (3 * MOD_ROWS + _mod_row(i), 0, 0)),
            pl.BlockSpec((D_MODEL, 2 * LANES), lambda i, *_: (0, 0)),
            pl.BlockSpec((N_EXPERTS, 1), lambda i, *_: (0, 0)),
        ]
    kern = functools.partial(_proj_kernel, prologue=prologue, kinds=tuple(kinds), chunk=chunk,
                             has_tab=tabs is not None, has_hgain=hgain is not None, resid=res is not None,
                             has_vt=n_vt > 0, comb=comb is not None, first=first is not None,
                             ffn=ffn is not None)
    out_specs = [pl.BlockSpec((TM, n_out), lambda i, *_: (i, 0))]
    out_shape = [jax.ShapeDtypeStruct((n, n_out), out_dtype)]
    if n_vt:
        out_specs.append(pl.BlockSpec((n_vt * LANES, TM), lambda i, *_: (0, i)))
        out_shape.append(jax.ShapeDtypeStruct((n_vt * LANES, n), out_dtype))
    if comb or first:
        out_specs.append(pl.BlockSpec((TM, k), lambda i, *_: (i, 0)))
        out_shape.append(jax.ShapeDtypeStruct((n, k), F32))
    if ffn:
        out_specs += [pl.BlockSpec((TM * SLAB_PITCH, LANES), lambda i, *_: (i, 0)), pl.BlockSpec((8, TM), lambda i, *_: (0, i))]
        out_shape += [jax.ShapeDtypeStruct((n * SLAB_PITCH, LANES), F32), jax.ShapeDtypeStruct((8, n), jnp.int32)]
    out = pl.pallas_call(
        kern,
        grid_spec=pltpu.PrefetchScalarGridSpec(
            num_scalar_prefetch=len(prefetch), grid=(n // TM,), in_specs=specs, out_specs=out_specs,
            scratch_shapes=scratch),
        out_shape=out_shape,
        compiler_params=_params(),
        name="proj",
    )(*prefetch, *args)
    return out if len(out) > 1 else out[0]


def _kq(k, q):
    return lax.dot_general(k, q, (((1,), (1,)), ((), ())), preferred_element_type=F32)


def _pipelined_heads(n_heads, n_chunks, q_of, k_of, vt_of, bias_of, s_ref, store):
    def scores(h, c, slot):
        s = _kq(k_of(h, c), q_of(h))
        b = bias_of(h, c)
        s_ref[slot, c] = s if b is None else s + b

    for c in range(n_chunks):
        scores(0, c, 0)
    for h in range(n_heads):
        slot = h % 2
        m = functools.reduce(jnp.maximum,
                             [jnp.max(s_ref[slot, c], axis=0, keepdims=True) for c in range(n_chunks)])
        num, den = None, None
        for c in range(n_chunks):
            if h + 1 < n_heads:
                scores(h + 1, c, 1 - slot)
            p = jnp.exp2(s_ref[slot, c] - m)
            l = jnp.sum(p, axis=0, keepdims=True)
            o = jnp.dot(vt_of(h, c), p.astype(BF16), preferred_element_type=F32)
            num = o if num is None else num + o
            den = l if den is None else den + l
        store(h, (num / den).T)


def _attn_kernel(*refs, heads, kv_heads, dq, has_kpe):
    if has_kpe:
        q_ref, k_ref, kpe_ref, vt_ref, o_ref, s_ref, keys_ref = refs
    else:
        q_ref, k_ref, vt_ref, o_ref, s_ref = refs
        keys_ref = None
    qt = pl.program_id(2)

    if has_kpe:
        @pl.when(qt == 0)
        def _():
            kpe = kpe_ref[...].astype(BF16)
            for h in range(heads):
                keys_ref[h, :, 0:HEAD_DIM] = k_ref[:, h * HEAD_DIM:(h + 1) * HEAD_DIM]
                keys_ref[h, :, HEAD_DIM:] = kpe

    def kv_of(h):
        return h // (heads // kv_heads)

    def q_of(h):
        return q_ref[:, h * dq:(h + 1) * dq]

    def k_of(h, c):
        rows = slice(c * KEY_CHUNK, (c + 1) * KEY_CHUNK)
        if has_kpe:
            return keys_ref[h, rows, :]
        return k_ref[rows, kv_of(h) * HEAD_DIM:(kv_of(h) + 1) * HEAD_DIM]

    def vt_of(h, c):
        return vt_ref[kv_of(h) * HEAD_DIM:(kv_of(h) + 1) * HEAD_DIM, c * KEY_CHUNK:(c + 1) * KEY_CHUNK]

    def store(h, o):
        o_ref[:, h * HEAD_DIM:(h + 1) * HEAD_DIM] = o.astype(o_ref.dtype)

    @pl.when(qt == 0)
    def _():
        _pipelined_heads(heads, CTX_LEN // KEY_CHUNK, q_of, k_of, vt_of, lambda h, c: None, s_ref, store)

    @pl.when(qt > 0)
    def _():
        _pipelined_heads(heads, TOK // KEY_CHUNK, q_of, k_of, vt_of, lambda h, c: None, s_ref, store)


def _attention(q_arr, k_arr, vt_arr, *, batch, groups, heads_per_step, kv_per_step, dq, k_blk0,
               kpe_arr=None, kpe_blk=0):
    shared_kv = kpe_arr is None
    kw = kv_per_step * HEAD_DIM
    ow = heads_per_step * HEAD_DIM
    args = [q_arr, k_arr]
    specs = [
        pl.BlockSpec((TM, heads_per_step * dq), lambda b, g, t: (b * TILES_PER_BATCH + t, g)),
        pl.BlockSpec((TOK, kw), lambda b, g, t: (b, k_blk0 + g)),
    ]
    scratch = [pltpu.VMEM((2, TOK // KEY_CHUNK, KEY_CHUNK, TM), F32)]
    if not shared_kv:
        args.append(kpe_arr)
        specs.append(pl.BlockSpec((TOK, LANES), lambda b, g, t: (b, kpe_blk)))
        scratch.append(pltpu.VMEM((heads_per_step, TOK, HEAD_DIM + LANES), BF16))
    args.append(vt_arr)
    specs.append(pl.BlockSpec((kw, TOK), lambda b, g, t: (g, b)))
    kern = functools.partial(_attn_kernel, heads=heads_per_step, kv_heads=kv_per_step, dq=dq,
                             has_kpe=not shared_kv)
    return pl.pallas_call(
        kern,
        grid=(batch, groups, TILES_PER_BATCH),
        in_specs=specs,
        out_specs=pl.BlockSpec((TM, ow), lambda b, g, t: (b * TILES_PER_BATCH + t, g)),
        out_shape=jax.ShapeDtypeStruct((batch * TOK, groups * ow), BF16),
        scratch_shapes=scratch,
        compiler_params=_params(),
        name="attention",
    )(*args)


def _na_kernel(q_ref, k_ref, vt_ref, tbl_ref, o_ref, s_ref):
    t = pl.program_id(2)
    n_lat_tiles = GRID_ROWS // NA_Q_ROWS
    k_row0 = jnp.clip(NA_Q_ROWS * t - NA_WIN_ROWS // 2, 0, GRID_ROWS - NA_K_ROWS)
    start = pl.multiple_of(CTX_LEN + k_row0 * GRID_W, KEY_CHUNK)
    which = jnp.where(t == 0, 0, jnp.where(t == n_lat_tiles - 1, 2, 1))

    def key_rows(c):
        return pl.ds(0, KEY_CHUNK) if c == 0 else pl.ds(start + (c - 1) * KEY_CHUNK, KEY_CHUNK)

    def q_of(h):
        return q_ref[:, h * HEAD_DIM:(h + 1) * HEAD_DIM]

    def k_of(h, c):
        return k_ref[key_rows(c), h * HEAD_DIM:(h + 1) * HEAD_DIM]

    def vt_of(h, c):
        return vt_ref[h * HEAD_DIM:(h + 1) * HEAD_DIM, key_rows(c)]

    def bias_of(h, c):
        return None if c == 0 else tbl_ref[h, which, (c - 1) * KEY_CHUNK:c * KEY_CHUNK, :]

    def store(h, o):
        o_ref[:, h * HEAD_DIM:(h + 1) * HEAD_DIM] = o.astype(o_ref.dtype)

    @pl.when(t == n_lat_tiles)
    def _():
        _pipelined_heads(NA_HEADS_PER_STEP, 1, q_of, k_of, vt_of, bias_of, s_ref, store)

    @pl.when(t < n_lat_tiles)
    def _():
        _pipelined_heads(NA_HEADS_PER_STEP, 1 + NA_KEYS // KEY_CHUNK, q_of, k_of, vt_of, bias_of, s_ref, store)


def _na_bias_table(rpb):
    cols = np.arange(GRID_W)
    c0 = np.clip(cols - NA_WIN_COLS // 2, 0, GRID_W - NA_WIN_COLS)
    col_ok = (cols[:, None] >= c0[None, :]) & (cols[:, None] < c0[None, :] + NA_WIN_COLS)
    cb = np.clip(cols[:, None] - cols[None, :] + NA_WIN_COLS - 1, 0, 2 * NA_WIN_COLS - 2)
    onehot = (cb[None] == np.arange(2 * NA_WIN_COLS - 1)[:, None, None]).astype(np.float32)
    planes = jnp.einsum("hdc,ckq->hdkq", rpb.astype(F32), onehot, precision=lax.Precision.HIGHEST)
    planes = jnp.where(col_ok[None, None], planes * LOG2E, NEG_BIG)
    n_rel = 2 * NA_WIN_ROWS - 1
    planes = jnp.concatenate([planes, jnp.full((NA_HEADS, 1, GRID_W, GRID_W), NEG_BIG, F32)], axis=1)
    n_lat_tiles = GRID_ROWS // NA_Q_ROWS
    which = np.full((3, NA_K_ROWS, NA_Q_ROWS), n_rel, np.int32)
    for kind, t in enumerate((0, 1, n_lat_tiles - 1)):
        k_row0 = int(np.clip(NA_Q_ROWS * t - NA_WIN_ROWS // 2, 0, GRID_ROWS - NA_K_ROWS))
        for kk in range(NA_K_ROWS):
            for i in range(NA_Q_ROWS):
                qr, kr = NA_Q_ROWS * t + i, k_row0 + kk
                r0 = int(np.clip(qr - NA_WIN_ROWS // 2, 0, GRID_ROWS - NA_WIN_ROWS))
                if r0 <= kr < r0 + NA_WIN_ROWS:
                    which[kind, kk, i] = kr - qr + NA_WIN_ROWS - 1
    tbl = jnp.concatenate([planes[:, which[:, :, i]] for i in range(NA_Q_ROWS)], axis=-1)
    return tbl.reshape(NA_HEADS, 3, NA_KEYS, TM)


def _na_attention(qk, vt, rpb, batch):
    tbl = _na_bias_table(rpb)
    n_lat_tiles = GRID_ROWS // NA_Q_ROWS

    def q_tile(b, t):
        return b * TILES_PER_BATCH + jnp.where(t == n_lat_tiles, 0, t + 1)

    gw = NA_HEADS_PER_STEP * HEAD_DIM
    n_groups = NA_HEADS // NA_HEADS_PER_STEP
    return pl.pallas_call(
        _na_kernel,
        grid=(n_groups, batch, n_lat_tiles + 1),
        in_specs=[
            pl.BlockSpec((TM, gw), lambda g, b, t: (q_tile(b, t), g)),
            pl.BlockSpec((TOK, gw), lambda g, b, t: (b, n_groups + g)),
            pl.BlockSpec((gw, TOK), lambda g, b, t: (g, b)),
            pl.BlockSpec((NA_HEADS_PER_STEP, 3, NA_KEYS, TM), lambda g, b, t: (g, 0, 0, 0),
                         pipeline_mode=pl.Buffered(1)),
        ],
        out_specs=pl.BlockSpec((TM, gw), lambda g, b, t: (q_tile(b, t), g)),
        out_shape=jax.ShapeDtypeStruct((batch * TOK, NA_HEADS * HEAD_DIM), BF16),
        scratch_shapes=[pltpu.VMEM((2, 1 + NA_KEYS // KEY_CHUNK, KEY_CHUNK, TM), F32)],
        compiler_params=_params(),
        name="na_attention",
    )(qk, qk, vt, tbl)


def _route(logits, bias):
    scores = 1.0 / (1.0 + jnp.exp(-logits))
    sel = scores + bias
    srow = [sel[e:e + 1, :] for e in range(N_EXPERTS)]
    crow = [scores[e:e + 1, :] for e in range(N_EXPERTS)]
    gscore = []
    for g in range(N_GROUPS):
        a, b, c, d = srow[GROUP_SIZE * g:GROUP_SIZE * (g + 1)]
        gscore.append(jnp.maximum(jnp.maximum(jnp.maximum(a + b, a + c), jnp.maximum(a + d, b + c)),
                                  jnp.maximum(b + d, c + d)))
    best, gidx = gscore[0], jnp.zeros(gscore[0].shape, jnp.int32)
    for g in range(1, N_GROUPS):
        better = gscore[g] > best
        gidx = jnp.where(better, g, gidx)
        best = jnp.where(better, gscore[g], best)
    s, c = [], []
    for j in range(GROUP_SIZE):
        sj, cj = srow[j], crow[j]
        for g in range(1, N_GROUPS):
            sj = jnp.where(gidx == g, srow[GROUP_SIZE * g + j], sj)
            cj = jnp.where(gidx == g, crow[GROUP_SIZE * g + j], cj)
        s.append(sj)
        c.append(cj)
    chosen = []
    for j in range(GROUP_SIZE):
        rank = jnp.zeros(s[j].shape, jnp.int32)
        for k in range(GROUP_SIZE):
            if k < j:
                rank = rank + jnp.where(s[k] >= s[j], 1, 0)
            elif k > j:
                rank = rank + jnp.where(s[k] > s[j], 1, 0)
        chosen.append(rank < 2)
    w = [jnp.where(chosen[j], c[j], 0.0) for j in range(GROUP_SIZE)]
    denom = (w[0] + w[1]) + (w[2] + w[3])
    gate = [w[j] / denom for j in range(GROUP_SIZE)]
    lo = jnp.where(chosen[0], 0, jnp.where(chosen[1], 1, 2))
    hi = jnp.where(chosen[3], 3, jnp.where(chosen[2], 2, 1))
    g_lo = jnp.where(chosen[0], gate[0], jnp.where(chosen[1], gate[1], gate[2]))
    g_hi = jnp.where(chosen[3], gate[3], jnp.where(chosen[2], gate[2], gate[1]))
    pair = jnp.where(lo == 0, hi - 1, jnp.where(lo == 1, 6 - hi, N_PAIRS - 1))
    return gidx * N_PAIRS + pair, g_lo, g_hi


def _slab_rows(c):
    return pl.ds(c, TM, stride=SLAB_PITCH)


def _store_slabs(ref, val, extra):
    for c in range(SLAB_ROWS):
        ref[_slab_rows(c), :] = val[:, c * LANES:(c + 1) * LANES]
    ref[_slab_rows(SLAB_ROWS), :] = extra


def _ffn_prep_tile(h, gain_ref, sc_ref, sh_ref, rw_ref, rb_ref, v_ref, part_ref):
    v = _rms(h, gain_ref[...]) * (1.0 + sc_ref[...]) + sh_ref[...]
    v_hi = v.astype(BF16)
    v_lo = (v - v_hi.astype(F32)).astype(BF16)
    a = jnp.dot(v_hi, rw_ref[...], preferred_element_type=F32)
    b = jnp.dot(v_lo, rw_ref[:, 0:LANES], preferred_element_type=F32)
    logits = (a[:, 0:LANES] + a[:, LANES:] + b).T[0:N_EXPERTS, :]
    part, g_lo, g_hi = _route(logits, rb_ref[...])
    part_ref[...] = jnp.zeros(part_ref.shape, jnp.int32)
    part_ref[0:1, :] = part
    row = lax.broadcasted_iota(jnp.int32, (LANES, TM), 0)
    gates_t = jnp.where(row == 0, g_lo, jnp.where(row == 1, g_hi, 0.0))
    _store_slabs(v_ref, v, gates_t.T)


def _dispatch_meta(part, n_slots):
    n = part.shape[0]
    onehot = (part[:, None] == jnp.arange(N_PARTS, dtype=jnp.int32)[None, :]).astype(jnp.int32)
    csum = jnp.cumsum(onehot, axis=0)
    rank = jnp.take_along_axis(csum, part[:, None], axis=1)[:, 0] - 1
    counts = csum[-1]
    padded = ((counts + TM - 1) // TM) * TM
    ends = jnp.cumsum(padded)
    starts = ends - padded
    pos = (starts[part] + rank).astype(jnp.int32)
    src = jnp.zeros((n_slots,), jnp.int32).at[pos].set(jnp.arange(n, dtype=jnp.int32))
    tile_start = jnp.arange(n_slots // TM, dtype=jnp.int32) * TM
    last_start = jnp.minimum(tile_start, ends[-1] - TM)
    tile_part = jnp.sum((ends[None, :] <= last_start[:, None]).astype(jnp.int32), axis=1)
    n_rows = jnp.clip(counts[tile_part] - (tile_start - starts[tile_part]), 0, TM).astype(jnp.int32)
    group, pair = tile_part // N_PAIRS, tile_part % N_PAIRS
    e_lo = group * GROUP_SIZE + jnp.asarray(PAIR_LO, jnp.int32)[pair]
    e_hi = group * GROUP_SIZE + jnp.asarray(PAIR_HI, jnp.int32)[pair]
    return pos, src, e_lo, e_hi, n_rows


def _slab_gather(pos_ref, ys_hbm, cbuf, csem, step, n_steps, tile_of):
    slot = step % 2

    def copy(slot_, r, row):
        return pltpu.make_async_copy(ys_hbm.at[pl.ds(row * SLAB_PITCH, SLAB_PITCH), :],
                                     cbuf.at[slot_, pl.ds(r * SLAB_PITCH, SLAB_PITCH), :], csem.at[slot_])

    def start(s, slot_):
        base = tile_of(s) * TM
        for r in range(TM):
            copy(slot_, r, pos_ref[base + r]).start(priority=SLAB_DMA_PRIORITY)

    @pl.when(step == 0)
    def _():
        start(0, 0)

    for r in range(TM):
        copy(slot, r, 0).wait()

    @pl.when(step + 1 < n_steps)
    def _():
        start(step + 1, 1 - slot)

    return cbuf.at[slot]


def _expert_kernel(src_ref, elo_ref, ehi_ref, nrows_ref, v_hbm, w1a, w3a, w2a, w1b, w3b, w2b, y_ref,
                   xbuf, x2d, in_sem, *, pitch):
    t = pl.program_id(0)
    slot = t % 2

    def in_copy(slot_, r, tok):
        return pltpu.make_async_copy(v_hbm.at[pl.ds(tok * SLAB_PITCH, SLAB_PITCH), :],
                                     xbuf.at[slot_, pl.ds(r * pitch, SLAB_PITCH), :], in_sem.at[slot_])

    def start_in(tile, slot_):
        for r in range(TM):
            in_copy(slot_, r, src_ref[tile * TM + r]).start(priority=r % 2)

    n_prev = jnp.where(t > 0, nrows_ref[jnp.maximum(t - 1, 0)], 0)

    @pl.when(t == 0)
    def _():
        start_in(0, 0)

    @pl.when(jnp.logical_or(t == 0, n_prev > 0))
    def _():
        for r in range(TM):
            in_copy(slot, r, 0).wait()

    @pl.when(nrows_ref[t] > 0)
    def _():
        start_in(t + 1, 1 - slot)

    @pl.when(nrows_ref[t] > 0)
    def _():
        xs = xbuf.at[slot]
        for c in range(SLAB_ROWS):
            x2d[:, c * LANES:(c + 1) * LANES] = xs[pl.ds(c, TM, stride=pitch), :].astype(BF16)
        g = xs[pl.ds(SLAB_ROWS, TM, stride=pitch), :]
        x = x2d[...]

        def hidden(w1, w3, gate):
            a = jnp.dot(x, w1[...], preferred_element_type=F32)
            b = jnp.dot(x, w3[...], preferred_element_type=F32)
            return (_silu(a) * b * gate).astype(BF16)

        y = jnp.dot(hidden(w1a, w3a, g[:, 0:1]), w2a[...], preferred_element_type=F32)
        y = y + jnp.dot(hidden(w1b, w3b, g[:, 1:2]), w2b[...], preferred_element_type=F32)
        _store_slabs(y_ref, y, jnp.zeros((TM, LANES), F32))

    @pl.when(nrows_ref[t] == 0)
    def _():
        y_ref[...] = jnp.zeros(y_ref.shape, y_ref.dtype)


def _experts(v_slabs, src, e_lo, e_hi, n_rows, w1, w3, w2, layer):
    n_tiles = n_rows.shape[0]
    up = lambda sel: pl.BlockSpec((None, None, D_MODEL, D_EXPERT),
                                  lambda t, s, lo, hi, nr: (layer, (lo, hi)[sel][t], 0, 0))
    down = lambda sel: pl.BlockSpec((None, None, D_EXPERT, D_MODEL),
                                    lambda t, s, lo, hi, nr: (layer, (lo, hi)[sel][t], 0, 0))
    pitch = (24, 24, SLAB_PITCH, SLAB_PITCH)[layer]
    return pl.pallas_call(
        functools.partial(_expert_kernel, pitch=pitch),
        grid_spec=pltpu.PrefetchScalarGridSpec(
            num_scalar_prefetch=4,
            grid=(n_tiles,),
            in_specs=[pl.BlockSpec(memory_space=pl.ANY), up(0), up(0), down(0), up(1), up(1), down(1)],
            out_specs=pl.BlockSpec((TM * SLAB_PITCH, LANES), lambda t, s, lo, hi, nr: (t, 0)),
            scratch_shapes=[pltpu.VMEM((2, TM * pitch, LANES), F32), pltpu.VMEM((TM, D_MODEL), BF16),
                            pltpu.SemaphoreType.DMA((2,))],
        ),
        out_shape=jax.ShapeDtypeStruct((n_tiles * TM * SLAB_PITCH, LANES), F32),
        compiler_params=_params(),
        name="experts",
    )(src, e_lo, e_hi, n_rows, v_slabs, w1, w3, w2, w1, w3, w2)


def _moe_experts(v_slabs, part, w1, w3, w2, layer):
    n = part.shape[1]
    pos, src, e_lo, e_hi, n_rows = _dispatch_meta(part[0], n + (N_PARTS + 1) * TM)
    return _experts(v_slabs, src, e_lo, e_hi, n_rows, w1, w3, w2, layer), pos


def _latent_tile(j):
    lat_tiles = SEQ // TM
    return (j // lat_tiles) * TILES_PER_BATCH + 1 + j % lat_tiles


def _final_kernel(pos_ref, y_hbm, h_ref, gate_ref, gain_ref, o_ref, hn_ref, cbuf, csem):
    ys = _slab_gather(pos_ref, y_hbm, cbuf, csem, pl.program_id(0), pl.num_programs(0), _latent_tile)
    for c in range(SLAB_ROWS):
        cols = slice(c * LANES, (c + 1) * LANES)
        hn_ref[:, cols] = h_ref[:, cols] + gate_ref[:, cols] * ys[_slab_rows(c), :]
    o_ref[...] = _rms(hn_ref[...], gain_ref[...])


def _final_norm(y_slabs, pos, h, mod, gain, batch):
    lat_tiles = SEQ // TM
    return pl.pallas_call(
        _final_kernel,
        grid_spec=pltpu.PrefetchScalarGridSpec(
            num_scalar_prefetch=1,
            grid=(batch * lat_tiles,),
            in_specs=[
                pl.BlockSpec(memory_space=pl.ANY),
                pl.BlockSpec((TM, D_MODEL), lambda j, p: (_latent_tile(j), 0)),
                pl.BlockSpec((None, 1, D_MODEL), lambda j, p: (5 * MOD_ROWS + j // lat_tiles, 0, 0)),
                pl.BlockSpec((1, D_MODEL), lambda j, p: (0, 0)),
            ],
            out_specs=pl.BlockSpec((TM, D_MODEL), lambda j, p: (j, 0)),
            scratch_shapes=[pltpu.VMEM((TM, D_MODEL), F32), pltpu.VMEM((2, TM * SLAB_PITCH, LANES), F32),
                            pltpu.SemaphoreType.DMA((2,))],
        ),
        out_shape=jax.ShapeDtypeStruct((batch * SEQ, D_MODEL), F32),
        compiler_params=_params(),
        name="final_norm",
    )(pos, y_slabs, h, mod, gain.reshape(1, D_MODEL))


def _rope_tables(rot_dim, layout):
    n_freq = rot_dim // 4
    inv_freq = 1.0 / (ROPE_THETA ** (jnp.arange(n_freq, dtype=F32) / n_freq))
    t = jnp.arange(SEQ, dtype=jnp.int32)
    row = (t // GRID_W).astype(F32)
    col = (t % GRID_W).astype(F32)
    ang = jnp.concatenate([row[:, None] * inv_freq, col[:, None] * inv_freq], axis=-1)
    cos, sin = jnp.cos(ang), jnp.sin(ang)
    if layout == "padded":
        pad1, pad0 = jnp.ones_like(cos), jnp.zeros_like(sin)
        c = jnp.concatenate([cos, pad1, cos, pad1], axis=-1)
        s = jnp.concatenate([-sin, pad0, sin, pad0], axis=-1)
    else:
        c = jnp.concatenate([cos, cos], axis=-1)
        s = jnp.concatenate([-sin, sin], axis=-1)
    c = jnp.concatenate([jnp.ones((CTX_LEN, LANES), F32), c], axis=0)
    s = jnp.concatenate([jnp.zeros((CTX_LEN, LANES), F32), s], axis=0)
    return c, s


def _pad_rope_cols(w):
    half = MLA_ROPE // 2
    z = jnp.zeros(w.shape[:-1] + (half,), w.dtype)
    return jnp.concatenate([w[..., :half], z, w[..., half:], z], axis=-1)


def _first_proj(h, comb, w, **kw):
    if comb is None:
        out = _proj(None, w, first=h, **kw)
    else:
        y_slabs, mod_prev, pos = comb
        out = _proj(None, w, comb=(y_slabs, h, mod_prev, pos), **kw)
    if kw["kinds"][-1][0] == "vt":
        return out[:2], out[2]
    return out[0], out[1]


def _out_proj(o, w_o, h, mod, ffn):
    return _proj(o, w_o.astype(BF16), out_dtype=F32, kinds=[("resid",)] * (D_MODEL // LANES),
                 res=h, mod=mod, mod_gate=2, ffn=ffn)


def _gqa_layer(h, comb, mod, norm_gain, w_qkv, q_gain, k_gain, w_o, tabs, ffn, batch):
    kinds = ([("normrope", 0, 0)] * GQA_HEADS + [("normrope", 1, 1)] * GQA_KV_HEADS
             + [("vt", j) for j in range(GQA_KV_HEADS)])
    (qk, vt), h = _first_proj(h, comb, w_qkv.astype(BF16), out_dtype=BF16, kinds=kinds, prologue="normmod",
                              gain=norm_gain, mod=mod, mod_sc=1, mod_sh=0, tabs=tabs,
                              hgain=jnp.stack([q_gain, k_gain]))
    o = _attention(qk, qk, vt, batch=batch, groups=GQA_KV_HEADS // GQA_KV_PER_STEP,
                   heads_per_step=GQA_GROUP * GQA_KV_PER_STEP, kv_per_step=GQA_KV_PER_STEP, dq=HEAD_DIM,
                   k_blk0=GQA_HEADS // GQA_KV_PER_STEP)
    return _out_proj(o, w_o, h, mod, ffn)


def _mla_layer(h, comb, mod, norm_gain, w_a, q_gain, kv_gain, w_uq, w_ukv, w_o, tabs, ffn, batch):
    n_lat = 2 * MLA_RANK // LANES
    w_a_p = jnp.concatenate([w_a[:, :2 * MLA_RANK], _pad_rope_cols(w_a[:, 2 * MLA_RANK:])], axis=1)
    a, h = _first_proj(h, comb, w_a_p.astype(BF16), out_dtype=F32, kinds=[("plain",)] * n_lat + [("rope", 1)],
                       prologue="normmod", gain=norm_gain, mod=mod, mod_sc=1, mod_sh=0, tabs=tabs,
                       chunk=LANES * (n_lat + 1))
    scale = MLA_QK ** -0.5 * LOG2E
    w_uq_h = w_uq.reshape(MLA_RANK, MLA_HEADS, MLA_QK)
    w_uq_p = jnp.concatenate([w_uq_h[..., :MLA_NOPE], _pad_rope_cols(w_uq_h[..., MLA_NOPE:])], axis=-1)
    q = _proj(a, w_uq_p.reshape(MLA_RANK, MLA_HEADS * 2 * LANES).astype(BF16), out_dtype=BF16,
              kinds=[("scale", scale), ("rope", 0)] * MLA_HEADS, prologue="rms", x_col_block=0, gain=q_gain, tabs=tabs)
    w_ukv_h = w_ukv.reshape(MLA_RANK, MLA_HEADS, 2 * HEAD_DIM)
    w_ukv_p = jnp.concatenate([w_ukv_h[..., :MLA_NOPE].reshape(MLA_RANK, -1),
                               w_ukv_h[..., MLA_NOPE:].reshape(MLA_RANK, -1)], axis=1)
    k_nope, vt = _proj(a, w_ukv_p.astype(BF16), out_dtype=BF16,
                       kinds=[("plain",)] * MLA_HEADS + [("vt", j) for j in range(MLA_HEADS)],
                       prologue="rms", x_col_block=1, gain=kv_gain)
    o = _attention(q, k_nope, vt, batch=batch, groups=MLA_HEADS // MLA_HEADS_PER_STEP,
                   heads_per_step=MLA_HEADS_PER_STEP, kv_per_step=MLA_HEADS_PER_STEP, dq=2 * LANES, k_blk0=0,
                   kpe_arr=a, kpe_blk=n_lat)
    return _out_proj(o, w_o, h, mod, ffn)


def _na_layer(h, comb, mod, norm_gain, w_qkv, rpb, w_o, ffn, batch):
    scale = HEAD_DIM ** -0.5 * LOG2E
    kinds = [("scale", scale)] * NA_HEADS + [("plain",)] * NA_HEADS + [("vt", j) for j in range(NA_HEADS)]
    (qk, vt), h = _first_proj(h, comb, w_qkv.astype(BF16), out_dtype=BF16, kinds=kinds, prologue="normmod",
                              gain=norm_gain, mod=mod, mod_sc=1, mod_sh=0)
    o = _na_attention(qk, vt, rpb, batch)
    return _out_proj(o, w_o, h, mod, ffn)


def kernel(x, c, ctx, c_ctx, mod_w, mod_b, norm_mix, norm_ffn, gqa_wqkv, gqa_q_gain, gqa_k_gain, gqa_wo,
           mla_wa, mla_q_gain, mla_kv_gain, mla_wuq, mla_wukv, mla_wo, na_wqkv, na_rpb, na_wo,
           router_w, router_bias, moe_w1, moe_w3, moe_w2, final_gain):
    batch = x.shape[0]
    assert batch <= CTX_MOD_ROW and x.shape[1:] == (SEQ, D_MODEL) and ctx.shape[1:] == (CTX_LEN, D_MODEL)
    h = (x, ctx)
    cond = jnp.zeros((MOD_ROWS, D_MODEL), F32).at[:batch].set(c).at[CTX_MOD_ROW].set(c_ctx)
    mods = _modulation(cond, mod_w, mod_b)
    mods = mods.reshape(DEPTH, MOD_ROWS, 6, D_MODEL).transpose(0, 2, 1, 3).reshape(DEPTH, 6 * MOD_ROWS, 1, D_MODEL)

    c_full, s_full = _rope_tables(HEAD_DIM, "full")
    gqa_scale = HEAD_DIM ** -0.5 * LOG2E
    gqa_tabs = (jnp.concatenate([c_full * gqa_scale, c_full], axis=1),
                jnp.concatenate([s_full * gqa_scale, s_full], axis=1))
    c_pad, s_pad = _rope_tables(MLA_ROPE, "padded")
    mla_scale = MLA_QK ** -0.5 * LOG2E
    mla_tabs = (jnp.concatenate([c_pad * mla_scale, c_pad], axis=1),
                jnp.concatenate([s_pad * mla_scale, s_pad], axis=1))
    rw_hi = router_w.astype(BF16)
    rw_lo = (router_w - rw_hi.astype(F32)).astype(BF16)
    lane_pad = ((0, 0), (0, LANES - N_EXPERTS))
    router_wt = jnp.concatenate([jnp.pad(rw_hi, lane_pad), jnp.pad(rw_lo, lane_pad)], axis=1)
    w1_b, w3_b, w2_b = moe_w1.astype(BF16), moe_w3.astype(BF16), moe_w2.astype(BF16)

    comb = None
    for i in range(DEPTH):
        mod = mods[i]
        ffn = (norm_ffn[i], router_wt, router_bias)
        kind, j = i % N_MIXERS, i // N_MIXERS
        if kind == 0:
            out = _gqa_layer(h, comb, mod, norm_mix[i], gqa_wqkv[j], gqa_q_gain[j], gqa_k_gain[j], gqa_wo[j],
                             gqa_tabs, ffn, batch)
        elif kind == 1:
            out = _mla_layer(h, comb, mod, norm_mix[i], mla_wa[j], mla_q_gain[j], mla_kv_gain[j], mla_wuq[j],
                             mla_wukv[j], mla_wo[j], mla_tabs, ffn, batch)
        else:
            out = _na_layer(h, comb, mod, norm_mix[i], na_wqkv[j], na_rpb[j], na_wo[j], ffn, batch)
        h, v_slabs, part = out
        y_slabs, pos = _moe_experts(v_slabs, part, w1_b, w3_b, w2_b, i)
        comb = (y_slabs, mod, pos)
    return _final_norm(y_slabs, pos, h, mod, final_gain, batch).reshape(batch, SEQ, D_MODEL)
```

```python
import functools

import numpy as np
import jax
import jax.numpy as jnp
from jax import lax
from jax.experimental import pallas as pl
from jax.experimental.pallas import tpu as pltpu

F32 = jnp.float32
BF16 = jnp.bfloat16

D_MODEL = 2048
SEQ = 2048
GRID_W = 64
GRID_ROWS = SEQ // GRID_W
CTX_LEN = 256
TOK = CTX_LEN + SEQ
DEPTH = 4
N_MIXERS = 3
EPS = 1e-6
ROPE_THETA = 10000.0

GQA_HEADS = 16
GQA_KV_HEADS = 4
HEAD_DIM = 128
GQA_GROUP = GQA_HEADS // GQA_KV_HEADS

MLA_HEADS = 16
MLA_RANK = 512
MLA_NOPE = 128
MLA_ROPE = 64
MLA_QK = MLA_NOPE + MLA_ROPE

NA_HEADS = 16
NA_WIN_ROWS = 8
NA_WIN_COLS = 16
NA_Q_ROWS = 4
NA_K_ROWS = 12
NA_KEYS = NA_K_ROWS * GRID_W
NA_HEADS_PER_STEP = 8
MLA_HEADS_PER_STEP = 8
GQA_KV_PER_STEP = 2
KEY_CHUNK = 256

N_EXPERTS = 16
N_GROUPS = 4
GROUP_SIZE = N_EXPERTS // N_GROUPS
N_PAIRS = 6
N_PARTS = N_GROUPS * N_PAIRS
D_EXPERT = 512
PAIR_LO = (0, 0, 0, 1, 1, 2)
PAIR_HI = (1, 2, 3, 2, 3, 3)

LANES = 128
TM = 256
TILES_PER_BATCH = TOK // TM
SLAB_ROWS = D_MODEL // LANES
SLAB_PITCH = SLAB_ROWS + 1
ROW_GROUP = 8
MOD_ROWS = 16
CTX_MOD_ROW = 8
NEG_BIG = -1e30
LOG2E = 1.4426950408889634
VMEM_LIMIT = 56 * 1024 * 1024


def _params():
    return pltpu.CompilerParams(vmem_limit_bytes=VMEM_LIMIT)


def _mod_row(i):
    return jnp.where(i % TILES_PER_BATCH == 0, CTX_MOD_ROW, i // TILES_PER_BATCH)


def _silu(x):
    return x / (1.0 + jnp.exp(-x))


def _mod_kernel(cond_ref, w_ref, b_ref, o_ref):
    s = _silu(cond_ref[...]).astype(BF16)
    o_ref[...] = jnp.dot(s, w_ref[...].astype(BF16), preferred_element_type=F32) + b_ref[...]


def _modulation(cond, mod_w, mod_b):
    tn = 1024
    n_out = mod_w.shape[-1]
    return pl.pallas_call(
        _mod_kernel,
        grid=(DEPTH, n_out // tn),
        in_specs=[
            pl.BlockSpec((MOD_ROWS, D_MODEL), lambda l, j: (0, 0)),
            pl.BlockSpec((None, D_MODEL, tn), lambda l, j: (l, 0, j)),
            pl.BlockSpec((None, 1, tn), lambda l, j: (l, 0, j)),
        ],
        out_specs=pl.BlockSpec((None, MOD_ROWS, tn), lambda l, j: (l, 0, j)),
        out_shape=jax.ShapeDtypeStruct((DEPTH, MOD_ROWS, n_out), F32),
        compiler_params=_params(),
        name="modulation",
    )(cond, mod_w, mod_b.reshape(DEPTH, 1, n_out))


def _rms(x, gain):
    return x * lax.rsqrt(jnp.mean(x * x, axis=-1, keepdims=True) + EPS) * gain


def _rope(x, c, s):
    return x * c + pltpu.roll(x, LANES // 2, 1) * s


def _runs(kinds):
    out, start = [], 0
    for k in range(1, len(kinds) + 1):
        if k == len(kinds) or kinds[k] != kinds[start] or kinds[start][0] in ("rope", "normrope", "vt"):
            out.append((start, k, kinds[start]))
            start = k
    return out


def _proj_kernel(*refs, prologue, kinds, chunk, has_tab, has_hgain, resid, has_vt, comb, first, ffn):
    it = iter(refs)
    if comb:
        y_ref, hprev_ref, cgate_ref = next(it), next(it), next(it)
    elif first:
        lat_ref, ctx_ref = next(it), next(it)
    else:
        x_ref = next(it)
    gain_ref = sc_ref = sh_ref = None
    if prologue == "normmod":
        gain_ref, sc_ref, sh_ref = next(it), next(it), next(it)
    elif prologue == "rms":
        gain_ref = next(it)
    w_ref = next(it)
    ctab_ref = stab_ref = hgain_ref = res_ref = gate_ref = None
    if has_tab:
        ctab_ref, stab_ref = next(it), next(it)
    if has_hgain:
        hgain_ref = next(it)
    if resid:
        res_ref, gate_ref = next(it), next(it)
    if ffn:
        fgain_ref, fsc_ref, fsh_ref, rw_ref, rb_ref = (next(it) for _ in range(5))
    o_ref = next(it)
    vt_ref = next(it) if has_vt else None
    hnew_ref = next(it) if (comb or first) else None
    if ffn:
        v_ref, part_ref = next(it), next(it)

    if first:
        is_ctx = pl.program_id(0) % TILES_PER_BATCH == 0
        hnew_ref[...] = jnp.where(is_ctx, ctx_ref[...], lat_ref[...])
        x_ref = hnew_ref
    if comb:
        for c in range(SLAB_ROWS):
            cols = slice(c * LANES, (c + 1) * LANES)
            hnew_ref[:, cols] = hprev_ref[:, cols] + cgate_ref[:, cols] * y_ref[_slab_rows(c), :]
        x_ref = hnew_ref

    if prologue == "none":
        u = x_ref[...]
    else:
        y = _rms(x_ref[...], gain_ref[...])
        if prologue == "normmod":
            y = y * (1.0 + sc_ref[...]) + sh_ref[...]
        u = y.astype(BF16)

    for c0 in range(0, len(kinds) * LANES, chunk):
        acc = jnp.dot(u, w_ref[:, c0:c0 + chunk], preferred_element_type=F32)
        b0 = c0 // LANES
        for lo, hi, kind in _runs(kinds[b0:b0 + chunk // LANES]):
            a = acc[:, lo * LANES:hi * LANES]
            cols = slice(c0 + lo * LANES, c0 + hi * LANES)
            if kind[0] == "scale":
                a = a * kind[1]
            elif kind[0] == "rope":
                t = kind[1]
                a = _rope(a, ctab_ref[:, t * LANES:(t + 1) * LANES], stab_ref[:, t * LANES:(t + 1) * LANES])
            elif kind[0] == "normrope":
                t, g = kind[1], kind[2]
                a = _rms(a, hgain_ref[g:g + 1, :])
                a = _rope(a, ctab_ref[:, t * LANES:(t + 1) * LANES], stab_ref[:, t * LANES:(t + 1) * LANES])
            elif kind[0] == "resid":
                a = res_ref[:, cols] + gate_ref[:, cols] * a
            elif kind[0] == "vt":
                vt_ref[kind[1] * LANES:(kind[1] + 1) * LANES, :] = a.T.astype(vt_ref.dtype)
                continue
            o_ref[:, cols] = a.astype(o_ref.dtype)

    if ffn:
        _ffn_prep_tile(o_ref[...], fgain_ref, fsc_ref, fsh_ref, rw_ref, rb_ref, v_ref, part_ref)


def _proj(x, w, *, out_dtype, kinds, prologue="none", x_col_block=0, gain=None, mod=None,
          mod_sc=None, mod_sh=None, tabs=None, hgain=None, res=None, mod_gate=None, chunk=512,
          comb=None, first=None, ffn=None):
    k, n_all = w.shape
    if first:
        n = first[0].shape[0] * TOK
    else:
        n = comb[1].shape[0] if comb else x.shape[0]
    n_vt = sum(kind[0] == "vt" for kind in kinds)
    n_out = n_all - n_vt * LANES
    assert n % TM == 0 and n_all % chunk == 0 and len(kinds) == n_all // LANES
    assert all(kind[0] == "vt" for kind in kinds[n_out // LANES:])
    if comb:
        y_slabs, h_prev, mod_prev = comb
        args = [y_slabs, h_prev, mod_prev]
        specs = [pl.BlockSpec((TM * SLAB_PITCH, LANES), lambda i: (i, 0)),
                 pl.BlockSpec((TM, k), lambda i: (i, 0)),
                 pl.BlockSpec((None, 1, k), lambda i: (5 * MOD_ROWS + _mod_row(i), 0, 0))]
    elif first:
        args = list(first)
        specs = [pl.BlockSpec((None, TM, k),
                              lambda i: (i // TILES_PER_BATCH, jnp.maximum(i % TILES_PER_BATCH - 1, 0), 0)),
                 pl.BlockSpec((None, TM, k), lambda i: (i // TILES_PER_BATCH, 0, 0))]
    else:
        args = [x]
        specs = [pl.BlockSpec((TM, k), lambda i: (i, x_col_block))]
    if prologue in ("normmod", "rms"):
        args.append(gain.reshape(1, k))
        specs.append(pl.BlockSpec((1, k), lambda i: (0, 0)))
    if prologue == "normmod":
        for comp in (mod_sc, mod_sh):
            args.append(mod)
            specs.append(pl.BlockSpec((None, 1, k), lambda i, comp=comp: (comp * MOD_ROWS + _mod_row(i), 0, 0)))
    args.append(w)
    specs.append(pl.BlockSpec((k, n_all), lambda i: (0, 0), pipeline_mode=pl.Buffered(1)))
    if tabs is not None:
        for t in tabs:
            args.append(t)
            specs.append(pl.BlockSpec((TM, t.shape[1]), lambda i: (i % TILES_PER_BATCH, 0)))
    if hgain is not None:
        args.append(hgain)
        specs.append(pl.BlockSpec(hgain.shape, lambda i: (0, 0)))
    if res is not None:
        args.append(res)
        specs.append(pl.BlockSpec((TM, n_out), lambda i: (i, 0)))
        args.append(mod)
        specs.append(pl.BlockSpec((None, 1, n_out), lambda i: (mod_gate * MOD_ROWS + _mod_row(i), 0, 0)))
    if ffn:
        f_gain, router_wt, router_bias = ffn
        assert res is not None and n_out == D_MODEL
        args += [f_gain.reshape(1, D_MODEL), mod, mod, router_wt, router_bias.reshape(N_EXPERTS, 1)]
        specs += [
            pl.BlockSpec((1, D_MODEL), lambda i: (0, 0)),
            pl.BlockSpec((None, 1, D_MODEL), lambda i: (4 * MOD_ROWS + _mod_row(i), 0, 0)),
            pl.BlockSpec((None, 1, D_MODEL), lambda i: (3 * MOD_ROWS + _mod_row(i), 0, 0)),
            pl.BlockSpec((D_MODEL, 2 * LANES), lambda i: (0, 0)),
            pl.BlockSpec((N_EXPERTS, 1), lambda i: (0, 0)),
        ]
    kern = functools.partial(_proj_kernel, prologue=prologue, kinds=tuple(kinds), chunk=chunk,
                             has_tab=tabs is not None, has_hgain=hgain is not None, resid=res is not None,
                             has_vt=n_vt > 0, comb=comb is not None, first=first is not None,
                             ffn=ffn is not None)
    out_specs = [pl.BlockSpec((TM, n_out), lambda i: (i, 0))]
    out_shape = [jax.ShapeDtypeStruct((n, n_out), out_dtype)]
    if n_vt:
        out_specs.append(pl.BlockSpec((n_vt * LANES, TM), lambda i: (0, i)))
        out_shape.append(jax.ShapeDtypeStruct((n_vt * LANES, n), out_dtype))
    if comb or first:
        out_specs.append(pl.BlockSpec((TM, k), lambda i: (i, 0)))
        out_shape.append(jax.ShapeDtypeStruct((n, k), F32))
    if ffn:
        out_specs += [pl.BlockSpec((TM * SLAB_PITCH, LANES), lambda i: (i, 0)), pl.BlockSpec((8, TM), lambda i: (0, i))]
        out_shape += [jax.ShapeDtypeStruct((n * SLAB_PITCH, LANES), F32), jax.ShapeDtypeStruct((8, n), jnp.int32)]
    out = pl.pallas_call(
        kern,
        grid=(n // TM,),
        in_specs=specs,
        out_specs=out_specs,
        out_shape=out_shape,
        compiler_params=_params(),
        name="proj",
    )(*args)
    return out if len(out) > 1 else out[0]


def _kq(k, q):
    return lax.dot_general(k, q, (((1,), (1,)), ((), ())), preferred_element_type=F32)


def _pipelined_heads(n_heads, n_chunks, q_of, k_of, vt_of, bias_of, s_ref, store):
    def scores(h, c, slot):
        s = _kq(k_of(h, c), q_of(h))
        b = bias_of(h, c)
        s_ref[slot, c] = s if b is None else s + b

    for c in range(n_chunks):
        scores(0, c, 0)
    for h in range(n_heads):
        slot = h % 2
        m = functools.reduce(jnp.maximum,
                             [jnp.max(s_ref[slot, c], axis=0, keepdims=True) for c in range(n_chunks)])
        num, den = None, None
        for c in range(n_chunks):
            if h + 1 < n_heads:
                scores(h + 1, c, 1 - slot)
            p = jnp.exp2(s_ref[slot, c] - m)
            l = jnp.sum(p, axis=0, keepdims=True)
            o = jnp.dot(vt_of(h, c), p.astype(BF16), preferred_element_type=F32)
            num = o if num is None else num + o
            den = l if den is None else den + l
        store(h, (num / den).T)


def _attn_kernel(*refs, heads, kv_heads, dq, has_kpe):
    if has_kpe:
        q_ref, k_ref, kpe_ref, vt_ref, o_ref, s_ref, keys_ref = refs
    else:
        q_ref, k_ref, vt_ref, o_ref, s_ref = refs
        keys_ref = None
    qt = pl.program_id(2)

    if has_kpe:
        @pl.when(qt == 0)
        def _():
            kpe = kpe_ref[...].astype(BF16)
            for h in range(heads):
                keys_ref[h, :, 0:HEAD_DIM] = k_ref[:, h * HEAD_DIM:(h + 1) * HEAD_DIM]
                keys_ref[h, :, HEAD_DIM:] = kpe

    def kv_of(h):
        return h // (heads // kv_heads)

    def q_of(h):
        return q_ref[:, h * dq:(h + 1) * dq]

    def k_of(h, c):
        rows = slice(c * KEY_CHUNK, (c + 1) * KEY_CHUNK)
        if has_kpe:
            return keys_ref[h, rows, :]
        return k_ref[rows, kv_of(h) * HEAD_DIM:(kv_of(h) + 1) * HEAD_DIM]

    def vt_of(h, c):
        return vt_ref[kv_of(h) * HEAD_DIM:(kv_of(h) + 1) * HEAD_DIM, c * KEY_CHUNK:(c + 1) * KEY_CHUNK]

    def store(h, o):
        o_ref[:, h * HEAD_DIM:(h + 1) * HEAD_DIM] = o.astype(o_ref.dtype)

    @pl.when(qt == 0)
    def _():
        _pipelined_heads(heads, CTX_LEN // KEY_CHUNK, q_of, k_of, vt_of, lambda h, c: None, s_ref, store)

    @pl.when(qt > 0)
    def _():
        _pipelined_heads(heads, TOK // KEY_CHUNK, q_of, k_of, vt_of, lambda h, c: None, s_ref, store)


def _attention(q_arr, k_arr, vt_arr, *, batch, groups, heads_per_step, kv_per_step, dq, k_blk0,
               kpe_arr=None, kpe_blk=0):
    shared_kv = kpe_arr is None
    kw = kv_per_step * HEAD_DIM
    ow = heads_per_step * HEAD_DIM
    args = [q_arr, k_arr]
    specs = [
        pl.BlockSpec((TM, heads_per_step * dq), lambda b, g, t: (b * TILES_PER_BATCH + t, g)),
        pl.BlockSpec((TOK, kw), lambda b, g, t: (b, k_blk0 + g)),
    ]
    scratch = [pltpu.VMEM((2, TOK // KEY_CHUNK, KEY_CHUNK, TM), F32)]
    if not shared_kv:
        args.append(kpe_arr)
        specs.append(pl.BlockSpec((TOK, LANES), lambda b, g, t: (b, kpe_blk)))
        scratch.append(pltpu.VMEM((heads_per_step, TOK, HEAD_DIM + LANES), BF16))
    args.append(vt_arr)
    specs.append(pl.BlockSpec((kw, TOK), lambda b, g, t: (g, b)))
    kern = functools.partial(_attn_kernel, heads=heads_per_step, kv_heads=kv_per_step, dq=dq,
                             has_kpe=not shared_kv)
    return pl.pallas_call(
        kern,
        grid=(batch, groups, TILES_PER_BATCH),
        in_specs=specs,
        out_specs=pl.BlockSpec((TM, ow), lambda b, g, t: (b * TILES_PER_BATCH + t, g)),
        out_shape=jax.ShapeDtypeStruct((batch * TOK, groups * ow), BF16),
        scratch_shapes=scratch,
        compiler_params=_params(),
        name="attention",
    )(*args)


def _na_kernel(q_ref, k_ref, vt_ref, tbl_ref, o_ref, s_ref):
    t = pl.program_id(2)
    n_lat_tiles = GRID_ROWS // NA_Q_ROWS
    k_row0 = jnp.clip(NA_Q_ROWS * t - NA_WIN_ROWS // 2, 0, GRID_ROWS - NA_K_ROWS)
    start = pl.multiple_of(CTX_LEN + k_row0 * GRID_W, KEY_CHUNK)
    which = jnp.where(t == 0, 0, jnp.where(t == n_lat_tiles - 1, 2, 1))

    def key_rows(c):
        return pl.ds(0, KEY_CHUNK) if c == 0 else pl.ds(start + (c - 1) * KEY_CHUNK, KEY_CHUNK)

    def q_of(h):
        return q_ref[:, h * HEAD_DIM:(h + 1) * HEAD_DIM]

    def k_of(h, c):
        return k_ref[key_rows(c), h * HEAD_DIM:(h + 1) * HEAD_DIM]

    def vt_of(h, c):
        return vt_ref[h * HEAD_DIM:(h + 1) * HEAD_DIM, key_rows(c)]

    def bias_of(h, c):
        return None if c == 0 else tbl_ref[h, which, (c - 1) * KEY_CHUNK:c * KEY_CHUNK, :]

    def store(h, o):
        o_ref[:, h * HEAD_DIM:(h + 1) * HEAD_DIM] = o.astype(o_ref.dtype)

    @pl.when(t == n_lat_tiles)
    def _():
        _pipelined_heads(NA_HEADS_PER_STEP, 1, q_of, k_of, vt_of, bias_of, s_ref, store)

    @pl.when(t < n_lat_tiles)
    def _():
        _pipelined_heads(NA_HEADS_PER_STEP, 1 + NA_KEYS // KEY_CHUNK, q_of, k_of, vt_of, bias_of, s_ref, store)


def _na_bias_table(rpb):
    cols = np.arange(GRID_W)
    c0 = np.clip(cols - NA_WIN_COLS // 2, 0, GRID_W - NA_WIN_COLS)
    col_ok = (cols[:, None] >= c0[None, :]) & (cols[:, None] < c0[None, :] + NA_WIN_COLS)
    cb = np.clip(cols[:, None] - cols[None, :] + NA_WIN_COLS - 1, 0, 2 * NA_WIN_COLS - 2)
    onehot = (cb[None] == np.arange(2 * NA_WIN_COLS - 1)[:, None, None]).astype(np.float32)
    planes = jnp.einsum("hdc,ckq->hdkq", rpb.astype(F32), onehot, precision=lax.Precision.HIGHEST)
    planes = jnp.where(col_ok[None, None], planes * LOG2E, NEG_BIG)
    n_rel = 2 * NA_WIN_ROWS - 1
    planes = jnp.concatenate([planes, jnp.full((NA_HEADS, 1, GRID_W, GRID_W), NEG_BIG, F32)], axis=1)
    n_lat_tiles = GRID_ROWS // NA_Q_ROWS
    which = np.full((3, NA_K_ROWS, NA_Q_ROWS), n_rel, np.int32)
    for kind, t in enumerate((0, 1, n_lat_tiles - 1)):
        k_row0 = int(np.clip(NA_Q_ROWS * t - NA_WIN_ROWS // 2, 0, GRID_ROWS - NA_K_ROWS))
        for kk in range(NA_K_ROWS):
            for i in range(NA_Q_ROWS):
                qr, kr = NA_Q_ROWS * t + i, k_row0 + kk
                r0 = int(np.clip(qr - NA_WIN_ROWS // 2, 0, GRID_ROWS - NA_WIN_ROWS))
                if r0 <= kr < r0 + NA_WIN_ROWS:
                    which[kind, kk, i] = kr - qr + NA_WIN_ROWS - 1
    tbl = jnp.concatenate([planes[:, which[:, :, i]] for i in range(NA_Q_ROWS)], axis=-1)
    return tbl.reshape(NA_HEADS, 3, NA_KEYS, TM)


def _na_attention(qk, vt, rpb, batch):
    tbl = _na_bias_table(rpb)
    n_lat_tiles = GRID_ROWS // NA_Q_ROWS

    def q_tile(b, t):
        return b * TILES_PER_BATCH + jnp.where(t == n_lat_tiles, 0, t + 1)

    gw = NA_HEADS_PER_STEP * HEAD_DIM
    n_groups = NA_HEADS // NA_HEADS_PER_STEP
    return pl.pallas_call(
        _na_kernel,
        grid=(n_groups, batch, n_lat_tiles + 1),
        in_specs=[
            pl.BlockSpec((TM, gw), lambda g, b, t: (q_tile(b, t), g)),
            pl.BlockSpec((TOK, gw), lambda g, b, t: (b, n_groups + g)),
            pl.BlockSpec((gw, TOK), lambda g, b, t: (g, b)),
            pl.BlockSpec((NA_HEADS_PER_STEP, 3, NA_KEYS, TM), lambda g, b, t: (g, 0, 0, 0),
                         pipeline_mode=pl.Buffered(1)),
        ],
        out_specs=pl.BlockSpec((TM, gw), lambda g, b, t: (q_tile(b, t), g)),
        out_shape=jax.ShapeDtypeStruct((batch * TOK, NA_HEADS * HEAD_DIM), BF16),
        scratch_shapes=[pltpu.VMEM((2, 1 + NA_KEYS // KEY_CHUNK, KEY_CHUNK, TM), F32)],
        compiler_params=_params(),
        name="na_attention",
    )(qk, qk, vt, tbl)


def _route(logits, bias):
    scores = 1.0 / (1.0 + jnp.exp(-logits))
    sel = scores + bias
    srow = [sel[e:e + 1, :] for e in range(N_EXPERTS)]
    crow = [scores[e:e + 1, :] for e in range(N_EXPERTS)]
    gscore = []
    for g in range(N_GROUPS):
        a, b, c, d = srow[GROUP_SIZE * g:GROUP_SIZE * (g + 1)]
        gscore.append(jnp.maximum(jnp.maximum(jnp.maximum(a + b, a + c), jnp.maximum(a + d, b + c)),
                                  jnp.maximum(b + d, c + d)))
    best, gidx = gscore[0], jnp.zeros(gscore[0].shape, jnp.int32)
    for g in range(1, N_GROUPS):
        better = gscore[g] > best
        gidx = jnp.where(better, g, gidx)
        best = jnp.where(better, gscore[g], best)
    s, c = [], []
    for j in range(GROUP_SIZE):
        sj, cj = srow[j], crow[j]
        for g in range(1, N_GROUPS):
            sj = jnp.where(gidx == g, srow[GROUP_SIZE * g + j], sj)
            cj = jnp.where(gidx == g, crow[GROUP_SIZE * g + j], cj)
        s.append(sj)
        c.append(cj)
    chosen = []
    for j in range(GROUP_SIZE):
        rank = jnp.zeros(s[j].shape, jnp.int32)
        for k in range(GROUP_SIZE):
            if k < j:
                rank = rank + jnp.where(s[k] >= s[j], 1, 0)
            elif k > j:
                rank = rank + jnp.where(s[k] > s[j], 1, 0)
        chosen.append(rank < 2)
    w = [jnp.where(chosen[j], c[j], 0.0) for j in range(GROUP_SIZE)]
    denom = (w[0] + w[1]) + (w[2] + w[3])
    gate = [w[j] / denom for j in range(GROUP_SIZE)]
    lo = jnp.where(chosen[0], 0, jnp.where(chosen[1], 1, 2))
    hi = jnp.where(chosen[3], 3, jnp.where(chosen[2], 2, 1))
    g_lo = jnp.where(chosen[0], gate[0], jnp.where(chosen[1], gate[1], gate[2]))
    g_hi = jnp.where(chosen[3], gate[3], jnp.where(chosen[2], gate[2], gate[1]))
    pair = jnp.where(lo == 0, hi - 1, jnp.where(lo == 1, hi + 1, N_PAIRS - 1))
    return gidx * N_PAIRS + pair, g_lo, g_hi


def _slab_rows(c):
    return pl.ds(c, TM, stride=SLAB_PITCH)


def _store_slabs(ref, val, extra):
    for c in range(SLAB_ROWS):
        ref[_slab_rows(c), :] = val[:, c * LANES:(c + 1) * LANES]
    ref[_slab_rows(SLAB_ROWS), :] = extra


def _ffn_prep_tile(h, gain_ref, sc_ref, sh_ref, rw_ref, rb_ref, v_ref, part_ref):
    v = _rms(h, gain_ref[...]) * (1.0 + sc_ref[...]) + sh_ref[...]
    v_hi = v.astype(BF16)
    v_lo = (v - v_hi.astype(F32)).astype(BF16)
    a = jnp.dot(v_hi, rw_ref[...], preferred_element_type=F32)
    b = jnp.dot(v_lo, rw_ref[:, 0:LANES], preferred_element_type=F32)
    logits = (a[:, 0:LANES] + a[:, LANES:] + b).T[0:N_EXPERTS, :]
    part, g_lo, g_hi = _route(logits, rb_ref[...])
    part_ref[...] = jnp.zeros(part_ref.shape, jnp.int32)
    part_ref[0:1, :] = part
    row = lax.broadcasted_iota(jnp.int32, (LANES, TM), 0)
    gates_t = jnp.where(row == 0, g_lo, jnp.where(row == 1, g_hi, 0.0))
    _store_slabs(v_ref, v, gates_t.T)


def _dispatch_meta(part, n_slots):
    n = part.shape[0]
    onehot = (part[:, None] == jnp.arange(N_PARTS, dtype=jnp.int32)[None, :]).astype(jnp.int32)
    csum = jnp.cumsum(onehot, axis=0)
    rank = jnp.take_along_axis(csum, part[:, None], axis=1)[:, 0] - 1
    counts = csum[-1]
    padded = ((counts + TM - 1) // TM) * TM
    ends = jnp.cumsum(padded)
    starts = ends - padded
    pos = (starts[part] + rank).astype(jnp.int32)
    src = jnp.zeros((n_slots,), jnp.int32).at[pos].set(jnp.arange(n, dtype=jnp.int32))
    tile_start = jnp.arange(n_slots // TM, dtype=jnp.int32) * TM
    last_start = jnp.minimum(tile_start, ends[-1] - TM)
    tile_part = jnp.sum((ends[None, :] <= last_start[:, None]).astype(jnp.int32), axis=1)
    n_rows = jnp.clip(counts[tile_part] - (tile_start - starts[tile_part]), 0, TM).astype(jnp.int32)
    group, pair = tile_part // N_PAIRS, tile_part % N_PAIRS
    e_lo = group * GROUP_SIZE + jnp.asarray(PAIR_LO, jnp.int32)[pair]
    e_hi = group * GROUP_SIZE + jnp.asarray(PAIR_HI, jnp.int32)[pair]
    return src, e_lo, e_hi, n_rows


def _expert_kernel(src_ref, elo_ref, ehi_ref, nrows_ref, v_hbm, w1a, w3a, w2a, w1b, w3b, w2b, y_hbm,
                   xbuf, ybuf, x2d, in_sem, out_sem, *, n_tok):
    t = pl.program_id(0)
    last = pl.num_programs(0) - 1
    slot = t % 2

    def n_groups(tile):
        return (nrows_ref[tile] + ROW_GROUP - 1) // ROW_GROUP

    def in_copy(slot_, r, tok):
        return pltpu.make_async_copy(v_hbm.at[pl.ds(tok * SLAB_PITCH, SLAB_PITCH), :],
                                     xbuf.at[slot_, pl.ds(r * SLAB_PITCH, SLAB_PITCH), :], in_sem.at[slot_])

    def out_copy(slot_, r, tok):
        return pltpu.make_async_copy(ybuf.at[slot_, pl.ds(r * SLAB_PITCH, SLAB_PITCH), :],
                                     y_hbm.at[pl.ds(tok * SLAB_PITCH, SLAB_PITCH), :], out_sem.at[slot_])

    def for_rows(tile, fn):
        def body(g, carry):
            for u in range(ROW_GROUP):
                fn(g * ROW_GROUP + u, u)
            return carry
        lax.fori_loop(0, n_groups(tile), body, 0)

    def start_in(tile, slot_):
        for_rows(tile, lambda r, u: in_copy(slot_, r, src_ref[tile * TM + r]).start(priority=u % 2))

    def start_out(tile, slot_):
        n = nrows_ref[tile]

        def one(r, u):
            tok = jnp.where(r < n, src_ref[tile * TM + r], n_tok + slot_ * ROW_GROUP + u)
            out_copy(slot_, r, tok).start(priority=u % 2)
        for_rows(tile, one)

    @pl.when(t == 0)
    def _():
        xbuf[...] = jnp.zeros(xbuf.shape, xbuf.dtype)
        spare = 2 * ROW_GROUP * SLAB_PITCH
        fill = pltpu.make_async_copy(xbuf.at[1, pl.ds(0, spare), :],
                                     y_hbm.at[pl.ds(n_tok * SLAB_PITCH, spare), :], out_sem.at[0])
        fill.start()
        fill.wait()
        start_in(0, 0)

    @pl.when(jnp.logical_and(t < last, nrows_ref[jnp.minimum(t + 1, last)] > 0))
    def _():
        start_in(t + 1, 1 - slot)

    @pl.when(nrows_ref[t] > 0)
    def _():
        for_rows(t, lambda r, u: in_copy(slot, r, 0).wait())
        xs = xbuf.at[slot]
        for c in range(SLAB_ROWS):
            x2d[:, c * LANES:(c + 1) * LANES] = xs[_slab_rows(c), :].astype(BF16)
        g = xs[_slab_rows(SLAB_ROWS), :]
        x = x2d[...]

        def hidden(w1, w3, gate):
            a = jnp.dot(x, w1[...], preferred_element_type=F32)
            b = jnp.dot(x, w3[...], preferred_element_type=F32)
            return (_silu(a) * b * gate).astype(BF16)

        y = jnp.dot(hidden(w1a, w3a, g[:, 0:1]), w2a[...], preferred_element_type=F32)
        y = y + jnp.dot(hidden(w1b, w3b, g[:, 1:2]), w2b[...], preferred_element_type=F32)
        _store_slabs(ybuf.at[slot], y, jnp.zeros((TM, LANES), F32))
        start_out(t, slot)

    @pl.when(jnp.logical_and(t > 0, nrows_ref[jnp.maximum(t - 1, 0)] > 0))
    def _():
        for_rows(t - 1, lambda r, u: out_copy(1 - slot, r, 0).wait())

    @pl.when(jnp.logical_and(t == last, nrows_ref[t] > 0))
    def _():
        for_rows(t, lambda r, u: out_copy(slot, r, 0).wait())


def _experts(v_slabs, src, e_lo, e_hi, n_rows, w1, w3, w2, layer, n_tok):
    n_tiles = n_rows.shape[0]
    up = lambda sel: pl.BlockSpec((None, None, D_MODEL, D_EXPERT),
                                  lambda t, s, lo, hi, nr: (layer, (lo, hi)[sel][t], 0, 0))
    down = lambda sel: pl.BlockSpec((None, None, D_EXPERT, D_MODEL),
                                    lambda t, s, lo, hi, nr: (layer, (lo, hi)[sel][t], 0, 0))
    buf = pltpu.VMEM((2, TM * SLAB_PITCH, LANES), F32)
    return pl.pallas_call(
        functools.partial(_expert_kernel, n_tok=n_tok),
        grid_spec=pltpu.PrefetchScalarGridSpec(
            num_scalar_prefetch=4,
            grid=(n_tiles,),
            in_specs=[pl.BlockSpec(memory_space=pl.ANY), up(0), up(0), down(0), up(1), up(1), down(1)],
            out_specs=pl.BlockSpec(memory_space=pl.ANY),
            scratch_shapes=[buf, buf, pltpu.VMEM((TM, D_MODEL), BF16),
                            pltpu.SemaphoreType.DMA((2,)), pltpu.SemaphoreType.DMA((2,))],
        ),
        out_shape=jax.ShapeDtypeStruct(((n_tok + 2 * ROW_GROUP) * SLAB_PITCH, LANES), F32),
        compiler_params=_params(),
        name="experts",
    )(src, e_lo, e_hi, n_rows, v_slabs, w1, w3, w2, w1, w3, w2)


def _moe_experts(v_slabs, part, w1, w3, w2, layer):
    n = part.shape[1]
    src, e_lo, e_hi, n_rows = _dispatch_meta(part[0], n + N_PARTS * TM)
    return _experts(v_slabs, src, e_lo, e_hi, n_rows, w1, w3, w2, layer, n)


def _final_kernel(y_ref, h_ref, gate_ref, gain_ref, o_ref, hn_ref):
    for c in range(SLAB_ROWS):
        cols = slice(c * LANES, (c + 1) * LANES)
        hn_ref[:, cols] = h_ref[:, cols] + gate_ref[:, cols] * y_ref[_slab_rows(c), :]
    o_ref[...] = _rms(hn_ref[...], gain_ref[...])


def _final_norm(y_slabs, h, mod, gain, batch):
    lat_tiles = SEQ // TM
    tile = lambda b, t: b * TILES_PER_BATCH + 1 + t
    return pl.pallas_call(
        _final_kernel,
        grid=(batch, lat_tiles),
        in_specs=[
            pl.BlockSpec((TM * SLAB_PITCH, LANES), lambda b, t: (tile(b, t), 0)),
            pl.BlockSpec((TM, D_MODEL), lambda b, t: (tile(b, t), 0)),
            pl.BlockSpec((None, 1, D_MODEL), lambda b, t: (5 * MOD_ROWS + b, 0, 0)),
            pl.BlockSpec((1, D_MODEL), lambda b, t: (0, 0)),
        ],
        out_specs=pl.BlockSpec((TM, D_MODEL), lambda b, t: (b * lat_tiles + t, 0)),
        out_shape=jax.ShapeDtypeStruct((batch * SEQ, D_MODEL), F32),
        scratch_shapes=[pltpu.VMEM((TM, D_MODEL), F32)],
        compiler_params=_params(),
        name="final_norm",
    )(y_slabs, h, mod, gain.reshape(1, D_MODEL))


def _rope_tables(rot_dim, layout):
    n_freq = rot_dim // 4
    inv_freq = 1.0 / (ROPE_THETA ** (jnp.arange(n_freq, dtype=F32) / n_freq))
    t = jnp.arange(SEQ, dtype=jnp.int32)
    row = (t // GRID_W).astype(F32)
    col = (t % GRID_W).astype(F32)
    ang = jnp.concatenate([row[:, None] * inv_freq, col[:, None] * inv_freq], axis=-1)
    cos, sin = jnp.cos(ang), jnp.sin(ang)
    if layout == "padded":
        pad1, pad0 = jnp.ones_like(cos), jnp.zeros_like(sin)
        c = jnp.concatenate([cos, pad1, cos, pad1], axis=-1)
        s = jnp.concatenate([-sin, pad0, sin, pad0], axis=-1)
    else:
        c = jnp.concatenate([cos, cos], axis=-1)
        s = jnp.concatenate([-sin, sin], axis=-1)
    c = jnp.concatenate([jnp.ones((CTX_LEN, LANES), F32), c], axis=0)
    s = jnp.concatenate([jnp.zeros((CTX_LEN, LANES), F32), s], axis=0)
    return c, s


def _pad_rope_cols(w):
    half = MLA_ROPE // 2
    z = jnp.zeros(w.shape[:-1] + (half,), w.dtype)
    return jnp.concatenate([w[..., :half], z, w[..., half:], z], axis=-1)


def _first_proj(h, comb, w, **kw):
    if comb is None:
        out = _proj(None, w, first=h, **kw)
    else:
        out = _proj(None, w, comb=(comb[0], h, comb[1]), **kw)
    if kw["kinds"][-1][0] == "vt":
        return out[:2], out[2]
    return out[0], out[1]


def _out_proj(o, w_o, h, mod, ffn):
    return _proj(o, w_o.astype(BF16), out_dtype=F32, kinds=[("resid",)] * (D_MODEL // LANES),
                 res=h, mod=mod, mod_gate=2, ffn=ffn)


def _gqa_layer(h, comb, mod, norm_gain, w_qkv, q_gain, k_gain, w_o, tabs, ffn, batch):
    kinds = ([("normrope", 0, 0)] * GQA_HEADS + [("normrope", 1, 1)] * GQA_KV_HEADS
             + [("vt", j) for j in range(GQA_KV_HEADS)])
    (qk, vt), h = _first_proj(h, comb, w_qkv.astype(BF16), out_dtype=BF16, kinds=kinds, prologue="normmod",
                              gain=norm_gain, mod=mod, mod_sc=1, mod_sh=0, tabs=tabs,
                              hgain=jnp.stack([q_gain, k_gain]))
    o = _attention(qk, qk, vt, batch=batch, groups=GQA_KV_HEADS // GQA_KV_PER_STEP,
                   heads_per_step=GQA_GROUP * GQA_KV_PER_STEP, kv_per_step=GQA_KV_PER_STEP, dq=HEAD_DIM,
                   k_blk0=GQA_HEADS // GQA_KV_PER_STEP)
    return _out_proj(o, w_o, h, mod, ffn)


def _mla_layer(h, comb, mod, norm_gain, w_a, q_gain, kv_gain, w_uq, w_ukv, w_o, tabs, ffn, batch):
    n_lat = 2 * MLA_RANK // LANES
    w_a_p = jnp.concatenate([w_a[:, :2 * MLA_RANK], _pad_rope_cols(w_a[:, 2 * MLA_RANK:])], axis=1)
    a, h = _first_proj(h, comb, w_a_p.astype(BF16), out_dtype=F32, kinds=[("plain",)] * n_lat + [("rope", 1)],
                       prologue="normmod", gain=norm_gain, mod=mod, mod_sc=1, mod_sh=0, tabs=tabs,
                       chunk=LANES * (n_lat + 1))
    scale = MLA_QK ** -0.5 * LOG2E
    w_uq_h = w_uq.reshape(MLA_RANK, MLA_HEADS, MLA_QK)
    w_uq_p = jnp.concatenate([w_uq_h[..., :MLA_NOPE], _pad_rope_cols(w_uq_h[..., MLA_NOPE:])], axis=-1)
    q = _proj(a, w_uq_p.reshape(MLA_RANK, MLA_HEADS * 2 * LANES).astype(BF16), out_dtype=BF16,
              kinds=[("scale", scale), ("rope", 0)] * MLA_HEADS, prologue="rms", x_col_block=0, gain=q_gain, tabs=tabs)
    w_ukv_h = w_ukv.reshape(MLA_RANK, MLA_HEADS, 2 * HEAD_DIM)
    w_ukv_p = jnp.concatenate([w_ukv_h[..., :MLA_NOPE].reshape(MLA_RANK, -1),
                               w_ukv_h[..., MLA_NOPE:].reshape(MLA_RANK, -1)], axis=1)
    k_nope, vt = _proj(a, w_ukv_p.astype(BF16), out_dtype=BF16,
                       kinds=[("plain",)] * MLA_HEADS + [("vt", j) for j in range(MLA_HEADS)],
                       prologue="rms", x_col_block=1, gain=kv_gain)
    o = _attention(q, k_nope, vt, batch=batch, groups=MLA_HEADS // MLA_HEADS_PER_STEP,
                   heads_per_step=MLA_HEADS_PER_STEP, kv_per_step=MLA_HEADS_PER_STEP, dq=2 * LANES, k_blk0=0,
                   kpe_arr=a, kpe_blk=n_lat)
    return _out_proj(o, w_o, h, mod, ffn)


def _na_layer(h, comb, mod, norm_gain, w_qkv, rpb, w_o, ffn, batch):
    scale = HEAD_DIM ** -0.5 * LOG2E
    kinds = [("scale", scale)] * NA_HEADS + [("plain",)] * NA_HEADS + [("vt", j) for j in range(NA_HEADS)]
    (qk, vt), h = _first_proj(h, comb, w_qkv.astype(BF16), out_dtype=BF16, kinds=kinds, prologue="normmod",
                              gain=norm_gain, mod=mod, mod_sc=1, mod_sh=0)
    o = _na_attention(qk, vt, rpb, batch)
    return _out_proj(o, w_o, h, mod, ffn)


def kernel(x, c, ctx, c_ctx, mod_w, mod_b, norm_mix, norm_ffn, gqa_wqkv, gqa_q_gain, gqa_k_gain, gqa_wo,
           mla_wa, mla_q_gain, mla_kv_gain, mla_wuq, mla_wukv, mla_wo, na_wqkv, na_rpb, na_wo,
           router_w, router_bias, moe_w1, moe_w3, moe_w2, final_gain):
    batch = x.shape[0]
    assert batch <= CTX_MOD_ROW and x.shape[1:] == (SEQ, D_MODEL) and ctx.shape[1:] == (CTX_LEN, D_MODEL)
    h = (x, ctx)
    cond = jnp.zeros((MOD_ROWS, D_MODEL), F32).at[:batch].set(c).at[CTX_MOD_ROW].set(c_ctx)
    mods = _modulation(cond, mod_w, mod_b)
    mods = mods.reshape(DEPTH, MOD_ROWS, 6, D_MODEL).transpose(0, 2, 1, 3).reshape(DEPTH, 6 * MOD_ROWS, 1, D_MODEL)

    c_full, s_full = _rope_tables(HEAD_DIM, "full")
    gqa_scale = HEAD_DIM ** -0.5 * LOG2E
    gqa_tabs = (jnp.concatenate([c_full * gqa_scale, c_full], axis=1),
                jnp.concatenate([s_full * gqa_scale, s_full], axis=1))
    c_pad, s_pad = _rope_tables(MLA_ROPE, "padded")
    mla_scale = MLA_QK ** -0.5 * LOG2E
    mla_tabs = (jnp.concatenate([c_pad * mla_scale, c_pad], axis=1),
                jnp.concatenate([s_pad * mla_scale, s_pad], axis=1))
    rw_hi = router_w.astype(BF16)
    rw_lo = (router_w - rw_hi.astype(F32)).astype(BF16)
    lane_pad = ((0, 0), (0, LANES - N_EXPERTS))
    router_wt = jnp.concatenate([jnp.pad(rw_hi, lane_pad), jnp.pad(rw_lo, lane_pad)], axis=1)
    w1_b, w3_b, w2_b = moe_w1.astype(BF16), moe_w3.astype(BF16), moe_w2.astype(BF16)

    comb = None
    for i in range(DEPTH):
        mod = mods[i]
        ffn = (norm_ffn[i], router_wt, router_bias)
        kind, j = i % N_MIXERS, i // N_MIXERS
        if kind == 0:
            out = _gqa_layer(h, comb, mod, norm_mix[i], gqa_wqkv[j], gqa_q_gain[j], gqa_k_gain[j], gqa_wo[j],
                             gqa_tabs, ffn, batch)
        elif kind == 1:
            out = _mla_layer(h, comb, mod, norm_mix[i], mla_wa[j], mla_q_gain[j], mla_kv_gain[j], mla_wuq[j],
                             mla_wukv[j], mla_wo[j], mla_tabs, ffn, batch)
        else:
            out = _na_layer(h, comb, mod, norm_mix[i], na_wqkv[j], na_rpb[j], na_wo[j], ffn, batch)
        h, v_slabs, part = out
        comb = (_moe_experts(v_slabs, part, w1_b, w3_b, w2_b, i), mod)
    return _final_norm(comb[0], h, comb[1], final_gain, batch).reshape(batch, SEQ, D_MODEL)
```

```python
import functools

import numpy as np
import jax
import jax.numpy as jnp
from jax import lax
from jax.experimental import pallas as pl
from jax.experimental.pallas import tpu as pltpu

F32 = jnp.float32
BF16 = jnp.bfloat16

D_MODEL = 2048
SEQ = 2048
GRID_W = 64
GRID_ROWS = SEQ // GRID_W
CTX_LEN = 256
TOK = CTX_LEN + SEQ
DEPTH = 4
N_MIXERS = 3
EPS = 1e-6
ROPE_THETA = 10000.0

GQA_HEADS = 16
GQA_KV_HEADS = 4
HEAD_DIM = 128
GQA_GROUP = GQA_HEADS // GQA_KV_HEADS

MLA_HEADS = 16
MLA_RANK = 512
MLA_NOPE = 128
MLA_ROPE = 64
MLA_QK = MLA_NOPE + MLA_ROPE

NA_HEADS = 16
NA_WIN_ROWS = 8
NA_WIN_COLS = 16
NA_Q_ROWS = 4
NA_K_ROWS = 12
NA_KEYS = NA_K_ROWS * GRID_W
NA_HEADS_PER_STEP = 8
MLA_HEADS_PER_STEP = 8
GQA_KV_PER_STEP = 4
KEY_CHUNK = 256

N_EXPERTS = 16
N_GROUPS = 4
GROUP_SIZE = N_EXPERTS // N_GROUPS
N_PAIRS = 6
N_PARTS = N_GROUPS * N_PAIRS
D_EXPERT = 512
PAIR_LO = (0, 0, 0, 1, 1, 2)
PAIR_HI = (1, 2, 3, 2, 3, 3)

LANES = 128
TM = 256
TILES_PER_BATCH = TOK // TM
SLAB_ROWS = D_MODEL // LANES
SLAB_PITCH = SLAB_ROWS + 1
ROW_GROUP = 8
MOD_ROWS = 16
CTX_MOD_ROW = 8
NEG_BIG = -1e30
LOG2E = 1.4426950408889634
VMEM_LIMIT = 56 * 1024 * 1024


def _params():
    return pltpu.CompilerParams(vmem_limit_bytes=VMEM_LIMIT)


def _mod_row(i):
    return jnp.where(i % TILES_PER_BATCH == 0, CTX_MOD_ROW, i // TILES_PER_BATCH)


def _silu(x):
    return x / (1.0 + jnp.exp(-x))


def _mod_kernel(cond_ref, w_ref, b_ref, o_ref):
    s = _silu(cond_ref[...]).astype(BF16)
    o_ref[...] = jnp.dot(s, w_ref[...].astype(BF16), preferred_element_type=F32) + b_ref[...]


def _modulation(cond, mod_w, mod_b):
    tn = 1024
    n_out = mod_w.shape[-1]
    return pl.pallas_call(
        _mod_kernel,
        grid=(DEPTH, n_out // tn),
        in_specs=[
            pl.BlockSpec((MOD_ROWS, D_MODEL), lambda l, j: (0, 0)),
            pl.BlockSpec((None, D_MODEL, tn), lambda l, j: (l, 0, j)),
            pl.BlockSpec((None, 1, tn), lambda l, j: (l, 0, j)),
        ],
        out_specs=pl.BlockSpec((None, MOD_ROWS, tn), lambda l, j: (l, 0, j)),
        out_shape=jax.ShapeDtypeStruct((DEPTH, MOD_ROWS, n_out), F32),
        compiler_params=_params(),
        name="modulation",
    )(cond, mod_w, mod_b.reshape(DEPTH, 1, n_out))


def _rms(x, gain):
    return x * lax.rsqrt(jnp.mean(x * x, axis=-1, keepdims=True) + EPS) * gain


def _rope(x, c, s):
    return x * c + pltpu.roll(x, LANES // 2, 1) * s


def _runs(kinds):
    out, start = [], 0
    for k in range(1, len(kinds) + 1):
        if k == len(kinds) or kinds[k] != kinds[start] or kinds[start][0] in ("rope", "normrope", "vt"):
            out.append((start, k, kinds[start]))
            start = k
    return out


def _proj_kernel(*refs, prologue, kinds, chunk, has_tab, has_hgain, resid, has_vt, comb, first, ffn):
    it = iter(refs)
    if comb:
        y_ref, hprev_ref, cgate_ref = next(it), next(it), next(it)
    elif first:
        lat_ref, ctx_ref = next(it), next(it)
    else:
        x_ref = next(it)
    gain_ref = sc_ref = sh_ref = None
    if prologue == "normmod":
        gain_ref, sc_ref, sh_ref = next(it), next(it), next(it)
    elif prologue == "rms":
        gain_ref = next(it)
    w_ref = next(it)
    ctab_ref = stab_ref = hgain_ref = res_ref = gate_ref = None
    if has_tab:
        ctab_ref, stab_ref = next(it), next(it)
    if has_hgain:
        hgain_ref = next(it)
    if resid:
        res_ref, gate_ref = next(it), next(it)
    if ffn:
        fgain_ref, fsc_ref, fsh_ref, rw_ref, rb_ref = (next(it) for _ in range(5))
    o_ref = next(it)
    vt_ref = next(it) if has_vt else None
    hnew_ref = next(it) if (comb or first) else None
    if ffn:
        v_ref, part_ref = next(it), next(it)

    if first:
        is_ctx = pl.program_id(0) % TILES_PER_BATCH == 0
        hnew_ref[...] = jnp.where(is_ctx, ctx_ref[...], lat_ref[...])
        x_ref = hnew_ref
    if comb:
        for c in range(SLAB_ROWS):
            cols = slice(c * LANES, (c + 1) * LANES)
            hnew_ref[:, cols] = hprev_ref[:, cols] + cgate_ref[:, cols] * y_ref[_slab_rows(c), :]
        x_ref = hnew_ref

    if prologue == "none":
        u = x_ref[...]
    else:
        y = _rms(x_ref[...], gain_ref[...])
        if prologue == "normmod":
            y = y * (1.0 + sc_ref[...]) + sh_ref[...]
        u = y.astype(BF16)

    for c0 in range(0, len(kinds) * LANES, chunk):
        acc = jnp.dot(u, w_ref[:, c0:c0 + chunk], preferred_element_type=F32)
        b0 = c0 // LANES
        for lo, hi, kind in _runs(kinds[b0:b0 + chunk // LANES]):
            a = acc[:, lo * LANES:hi * LANES]
            cols = slice(c0 + lo * LANES, c0 + hi * LANES)
            if kind[0] == "scale":
                a = a * kind[1]
            elif kind[0] == "rope":
                t = kind[1]
                a = _rope(a, ctab_ref[:, t * LANES:(t + 1) * LANES], stab_ref[:, t * LANES:(t + 1) * LANES])
            elif kind[0] == "normrope":
                t, g = kind[1], kind[2]
                a = _rms(a, hgain_ref[g:g + 1, :])
                a = _rope(a, ctab_ref[:, t * LANES:(t + 1) * LANES], stab_ref[:, t * LANES:(t + 1) * LANES])
            elif kind[0] == "resid":
                a = res_ref[:, cols] + gate_ref[:, cols] * a
            elif kind[0] == "vt":
                vt_ref[kind[1] * LANES:(kind[1] + 1) * LANES, :] = a.T.astype(vt_ref.dtype)
                continue
            o_ref[:, cols] = a.astype(o_ref.dtype)

    if ffn:
        _ffn_prep_tile(o_ref[...], fgain_ref, fsc_ref, fsh_ref, rw_ref, rb_ref, v_ref, part_ref)


def _proj(x, w, *, out_dtype, kinds, prologue="none", x_col_block=0, gain=None, mod=None,
          mod_sc=None, mod_sh=None, tabs=None, hgain=None, res=None, mod_gate=None, chunk=512,
          comb=None, first=None, ffn=None):
    k, n_all = w.shape
    if first:
        n = first[0].shape[0] * TOK
    else:
        n = comb[1].shape[0] if comb else x.shape[0]
    n_vt = sum(kind[0] == "vt" for kind in kinds)
    n_out = n_all - n_vt * LANES
    assert n % TM == 0 and n_all % chunk == 0 and len(kinds) == n_all // LANES
    assert all(kind[0] == "vt" for kind in kinds[n_out // LANES:])
    if comb:
        y_slabs, h_prev, mod_prev = comb
        args = [y_slabs, h_prev, mod_prev]
        specs = [pl.BlockSpec((TM * SLAB_PITCH, LANES), lambda i: (i, 0)),
                 pl.BlockSpec((TM, k), lambda i: (i, 0)),
                 pl.BlockSpec((None, 1, k), lambda i: (5 * MOD_ROWS + _mod_row(i), 0, 0))]
    elif first:
        args = list(first)
        specs = [pl.BlockSpec((None, TM, k),
                              lambda i: (i // TILES_PER_BATCH, jnp.maximum(i % TILES_PER_BATCH - 1, 0), 0)),
                 pl.BlockSpec((None, TM, k), lambda i: (i // TILES_PER_BATCH, 0, 0))]
    else:
        args = [x]
        specs = [pl.BlockSpec((TM, k), lambda i: (i, x_col_block))]
    if prologue in ("normmod", "rms"):
        args.append(gain.reshape(1, k))
        specs.append(pl.BlockSpec((1, k), lambda i: (0, 0)))
    if prologue == "normmod":
        for comp in (mod_sc, mod_sh):
            args.append(mod)
            specs.append(pl.BlockSpec((None, 1, k), lambda i, comp=comp: (comp * MOD_ROWS + _mod_row(i), 0, 0)))
    args.append(w)
    specs.append(pl.BlockSpec((k, n_all), lambda i: (0, 0), pipeline_mode=pl.Buffered(1)))
    if tabs is not None:
        for t in tabs:
            args.append(t)
            specs.append(pl.BlockSpec((TM, t.shape[1]), lambda i: (i % TILES_PER_BATCH, 0)))
    if hgain is not None:
        args.append(hgain)
        specs.append(pl.BlockSpec(hgain.shape, lambda i: (0, 0)))
    if res is not None:
        args.append(res)
        specs.append(pl.BlockSpec((TM, n_out), lambda i: (i, 0)))
        args.append(mod)
        specs.append(pl.BlockSpec((None, 1, n_out), lambda i: (mod_gate * MOD_ROWS + _mod_row(i), 0, 0)))
    if ffn:
        f_gain, router_wt, router_bias = ffn
        assert res is not None and n_out == D_MODEL
        args += [f_gain.reshape(1, D_MODEL), mod, mod, router_wt, router_bias.reshape(N_EXPERTS, 1)]
        specs += [
            pl.BlockSpec((1, D_MODEL), lambda i: (0, 0)),
            pl.BlockSpec((None, 1, D_MODEL), lambda i: (4 * MOD_ROWS + _mod_row(i), 0, 0)),
            pl.BlockSpec((None, 1, D_MODEL), lambda i: (3 * MOD_ROWS + _mod_row(i), 0, 0)),
            pl.BlockSpec((D_MODEL, 2 * LANES), lambda i: (0, 0)),
            pl.BlockSpec((N_EXPERTS, 1), lambda i: (0, 0)),
        ]
    kern = functools.partial(_proj_kernel, prologue=prologue, kinds=tuple(kinds), chunk=chunk,
                             has_tab=tabs is not None, has_hgain=hgain is not None, resid=res is not None,
                             has_vt=n_vt > 0, comb=comb is not None, first=first is not None,
                             ffn=ffn is not None)
    out_specs = [pl.BlockSpec((TM, n_out), lambda i: (i, 0))]
    out_shape = [jax.ShapeDtypeStruct((n, n_out), out_dtype)]
    if n_vt:
        out_specs.append(pl.BlockSpec((n_vt * LANES, TM), lambda i: (0, i)))
        out_shape.append(jax.ShapeDtypeStruct((n_vt * LANES, n), out_dtype))
    if comb or first:
        out_specs.append(pl.BlockSpec((TM, k), lambda i: (i, 0)))
        out_shape.append(jax.ShapeDtypeStruct((n, k), F32))
    if ffn:
        out_specs += [pl.BlockSpec((TM * SLAB_PITCH, LANES), lambda i: (i, 0)), pl.BlockSpec((8, TM), lambda i: (0, i))]
        out_shape += [jax.ShapeDtypeStruct((n * SLAB_PITCH, LANES), F32), jax.ShapeDtypeStruct((8, n), jnp.int32)]
    out = pl.pallas_call(
        kern,
        grid=(n // TM,),
        in_specs=specs,
        out_specs=out_specs,
        out_shape=out_shape,
        compiler_params=_params(),
        name="proj",
    )(*args)
    return out if len(out) > 1 else out[0]


def _kq(k, q):
    return lax.dot_general(k, q, (((1,), (1,)), ((), ())), preferred_element_type=F32)


def _pipelined_heads(n_heads, n_chunks, q_of, k_of, vt_of, bias_of, s_ref, store):
    def scores(h, c, slot):
        s = _kq(k_of(h, c), q_of(h))
        b = bias_of(h, c)
        s_ref[slot, c] = s if b is None else s + b

    for c in range(n_chunks):
        scores(0, c, 0)
    for h in range(n_heads):
        slot = h % 2
        m = functools.reduce(jnp.maximum,
                             [jnp.max(s_ref[slot, c], axis=0, keepdims=True) for c in range(n_chunks)])
        num, den = None, None
        for c in range(n_chunks):
            if h + 1 < n_heads:
                scores(h + 1, c, 1 - slot)
            p = jnp.exp2(s_ref[slot, c] - m)
            l = jnp.sum(p, axis=0, keepdims=True)
            o = jnp.dot(vt_of(h, c), p.astype(BF16), preferred_element_type=F32)
            num = o if num is None else num + o
            den = l if den is None else den + l
        store(h, (num / den).T)


def _attn_kernel(*refs, heads, kv_heads, dq, has_kpe):
    if has_kpe:
        q_ref, k_ref, kpe_ref, vt_ref, o_ref, s_ref, keys_ref = refs
    else:
        q_ref, k_ref, vt_ref, o_ref, s_ref = refs
        keys_ref = None
    qt = pl.program_id(2)

    if has_kpe:
        @pl.when(qt == 0)
        def _():
            kpe = kpe_ref[...].astype(BF16)
            for h in range(heads):
                keys_ref[h, :, 0:HEAD_DIM] = k_ref[:, h * HEAD_DIM:(h + 1) * HEAD_DIM]
                keys_ref[h, :, HEAD_DIM:] = kpe

    def kv_of(h):
        return h // (heads // kv_heads)

    def q_of(h):
        return q_ref[:, h * dq:(h + 1) * dq]

    def k_of(h, c):
        rows = slice(c * KEY_CHUNK, (c + 1) * KEY_CHUNK)
        if has_kpe:
            return keys_ref[h, rows, :]
        return k_ref[rows, kv_of(h) * HEAD_DIM:(kv_of(h) + 1) * HEAD_DIM]

    def vt_of(h, c):
        return vt_ref[kv_of(h) * HEAD_DIM:(kv_of(h) + 1) * HEAD_DIM, c * KEY_CHUNK:(c + 1) * KEY_CHUNK]

    def store(h, o):
        o_ref[:, h * HEAD_DIM:(h + 1) * HEAD_DIM] = o.astype(o_ref.dtype)

    @pl.when(qt == 0)
    def _():
        _pipelined_heads(heads, CTX_LEN // KEY_CHUNK, q_of, k_of, vt_of, lambda h, c: None, s_ref, store)

    @pl.when(qt > 0)
    def _():
        _pipelined_heads(heads, TOK // KEY_CHUNK, q_of, k_of, vt_of, lambda h, c: None, s_ref, store)


def _attention(q_arr, k_arr, vt_arr, *, batch, groups, heads_per_step, kv_per_step, dq, k_blk0,
               kpe_arr=None, kpe_blk=0):
    shared_kv = kpe_arr is None
    kw = kv_per_step * HEAD_DIM
    ow = heads_per_step * HEAD_DIM
    args = [q_arr, k_arr]
    specs = [
        pl.BlockSpec((TM, heads_per_step * dq), lambda b, g, t: (b * TILES_PER_BATCH + t, g)),
        pl.BlockSpec((TOK, kw), lambda b, g, t: (b, k_blk0 + g)),
    ]
    scratch = [pltpu.VMEM((2, TOK // KEY_CHUNK, KEY_CHUNK, TM), F32)]
    if not shared_kv:
        args.append(kpe_arr)
        specs.append(pl.BlockSpec((TOK, LANES), lambda b, g, t: (b, kpe_blk)))
        scratch.append(pltpu.VMEM((heads_per_step, TOK, HEAD_DIM + LANES), BF16))
    args.append(vt_arr)
    specs.append(pl.BlockSpec((kw, TOK), lambda b, g, t: (g, b)))
    kern = functools.partial(_attn_kernel, heads=heads_per_step, kv_heads=kv_per_step, dq=dq,
                             has_kpe=not shared_kv)
    return pl.pallas_call(
        kern,
        grid=(batch, groups, TILES_PER_BATCH),
        in_specs=specs,
        out_specs=pl.BlockSpec((TM, ow), lambda b, g, t: (b * TILES_PER_BATCH + t, g)),
        out_shape=jax.ShapeDtypeStruct((batch * TOK, groups * ow), BF16),
        scratch_shapes=scratch,
        compiler_params=_params(),
        name="attention",
    )(*args)


def _na_kernel(q_ref, k_ref, vt_ref, tbl_ref, o_ref, s_ref):
    t = pl.program_id(2)
    n_lat_tiles = GRID_ROWS // NA_Q_ROWS
    k_row0 = jnp.clip(NA_Q_ROWS * t - NA_WIN_ROWS // 2, 0, GRID_ROWS - NA_K_ROWS)
    start = pl.multiple_of(CTX_LEN + k_row0 * GRID_W, KEY_CHUNK)
    which = jnp.where(t == 0, 0, jnp.where(t == n_lat_tiles - 1, 2, 1))

    def key_rows(c):
        return pl.ds(0, KEY_CHUNK) if c == 0 else pl.ds(start + (c - 1) * KEY_CHUNK, KEY_CHUNK)

    def q_of(h):
        return q_ref[:, h * HEAD_DIM:(h + 1) * HEAD_DIM]

    def k_of(h, c):
        return k_ref[key_rows(c), h * HEAD_DIM:(h + 1) * HEAD_DIM]

    def vt_of(h, c):
        return vt_ref[h * HEAD_DIM:(h + 1) * HEAD_DIM, key_rows(c)]

    def bias_of(h, c):
        return None if c == 0 else tbl_ref[h, which, (c - 1) * KEY_CHUNK:c * KEY_CHUNK, :]

    def store(h, o):
        o_ref[:, h * HEAD_DIM:(h + 1) * HEAD_DIM] = o.astype(o_ref.dtype)

    @pl.when(t == n_lat_tiles)
    def _():
        _pipelined_heads(NA_HEADS_PER_STEP, 1, q_of, k_of, vt_of, bias_of, s_ref, store)

    @pl.when(t < n_lat_tiles)
    def _():
        _pipelined_heads(NA_HEADS_PER_STEP, 1 + NA_KEYS // KEY_CHUNK, q_of, k_of, vt_of, bias_of, s_ref, store)


def _na_bias_table(rpb):
    cols = np.arange(GRID_W)
    c0 = np.clip(cols - NA_WIN_COLS // 2, 0, GRID_W - NA_WIN_COLS)
    col_ok = (cols[:, None] >= c0[None, :]) & (cols[:, None] < c0[None, :] + NA_WIN_COLS)
    cb = np.clip(cols[:, None] - cols[None, :] + NA_WIN_COLS - 1, 0, 2 * NA_WIN_COLS - 2)
    onehot = (cb[None] == np.arange(2 * NA_WIN_COLS - 1)[:, None, None]).astype(np.float32)
    planes = jnp.einsum("hdc,ckq->hdkq", rpb.astype(F32), onehot, precision=lax.Precision.HIGHEST)
    planes = jnp.where(col_ok[None, None], planes * LOG2E, NEG_BIG)
    n_rel = 2 * NA_WIN_ROWS - 1
    planes = jnp.concatenate([planes, jnp.full((NA_HEADS, 1, GRID_W, GRID_W), NEG_BIG, F32)], axis=1)
    n_lat_tiles = GRID_ROWS // NA_Q_ROWS
    which = np.full((3, NA_K_ROWS, NA_Q_ROWS), n_rel, np.int32)
    for kind, t in enumerate((0, 1, n_lat_tiles - 1)):
        k_row0 = int(np.clip(NA_Q_ROWS * t - NA_WIN_ROWS // 2, 0, GRID_ROWS - NA_K_ROWS))
        for kk in range(NA_K_ROWS):
            for i in range(NA_Q_ROWS):
                qr, kr = NA_Q_ROWS * t + i, k_row0 + kk
                r0 = int(np.clip(qr - NA_WIN_ROWS // 2, 0, GRID_ROWS - NA_WIN_ROWS))
                if r0 <= kr < r0 + NA_WIN_ROWS:
                    which[kind, kk, i] = kr - qr + NA_WIN_ROWS - 1
    tbl = jnp.concatenate([planes[:, which[:, :, i]] for i in range(NA_Q_ROWS)], axis=-1)
    return tbl.reshape(NA_HEADS, 3, NA_KEYS, TM)


def _na_attention(qk, vt, rpb, batch):
    tbl = _na_bias_table(rpb)
    n_lat_tiles = GRID_ROWS // NA_Q_ROWS

    def q_tile(b, t):
        return b * TILES_PER_BATCH + jnp.where(t == n_lat_tiles, 0, t + 1)

    gw = NA_HEADS_PER_STEP * HEAD_DIM
    n_groups = NA_HEADS // NA_HEADS_PER_STEP
    return pl.pallas_call(
        _na_kernel,
        grid=(n_groups, batch, n_lat_tiles + 1),
        in_specs=[
            pl.BlockSpec((TM, gw), lambda g, b, t: (q_tile(b, t), g)),
            pl.BlockSpec((TOK, gw), lambda g, b, t: (b, n_groups + g)),
            pl.BlockSpec((gw, TOK), lambda g, b, t: (g, b)),
            pl.BlockSpec((NA_HEADS_PER_STEP, 3, NA_KEYS, TM), lambda g, b, t: (g, 0, 0, 0),
                         pipeline_mode=pl.Buffered(1)),
        ],
        out_specs=pl.BlockSpec((TM, gw), lambda g, b, t: (q_tile(b, t), g)),
        out_shape=jax.ShapeDtypeStruct((batch * TOK, NA_HEADS * HEAD_DIM), BF16),
        scratch_shapes=[pltpu.VMEM((2, 1 + NA_KEYS // KEY_CHUNK, KEY_CHUNK, TM), F32)],
        compiler_params=_params(),
        name="na_attention",
    )(qk, qk, vt, tbl)


def _route(logits, bias):
    scores = 1.0 / (1.0 + jnp.exp(-logits))
    sel = scores + bias
    srow = [sel[e:e + 1, :] for e in range(N_EXPERTS)]
    crow = [scores[e:e + 1, :] for e in range(N_EXPERTS)]
    gscore = []
    for g in range(N_GROUPS):
        a, b, c, d = srow[GROUP_SIZE * g:GROUP_SIZE * (g + 1)]
        gscore.append(jnp.maximum(jnp.maximum(jnp.maximum(a + b, a + c), jnp.maximum(a + d, b + c)),
                                  jnp.maximum(b + d, c + d)))
    best, gidx = gscore[0], jnp.zeros(gscore[0].shape, jnp.int32)
    for g in range(1, N_GROUPS):
        better = gscore[g] > best
        gidx = jnp.where(better, g, gidx)
        best = jnp.where(better, gscore[g], best)
    s, c = [], []
    for j in range(GROUP_SIZE):
        sj, cj = srow[j], crow[j]
        for g in range(1, N_GROUPS):
            sj = jnp.where(gidx == g, srow[GROUP_SIZE * g + j], sj)
            cj = jnp.where(gidx == g, crow[GROUP_SIZE * g + j], cj)
        s.append(sj)
        c.append(cj)
    chosen = []
    for j in range(GROUP_SIZE):
        rank = jnp.zeros(s[j].shape, jnp.int32)
        for k in range(GROUP_SIZE):
            if k < j:
                rank = rank + jnp.where(s[k] >= s[j], 1, 0)
            elif k > j:
                rank = rank + jnp.where(s[k] > s[j], 1, 0)
        chosen.append(rank < 2)
    w = [jnp.where(chosen[j], c[j], 0.0) for j in range(GROUP_SIZE)]
    denom = (w[0] + w[1]) + (w[2] + w[3])
    gate = [w[j] / denom for j in range(GROUP_SIZE)]
    lo = jnp.where(chosen[0], 0, jnp.where(chosen[1], 1, 2))
    hi = jnp.where(chosen[3], 3, jnp.where(chosen[2], 2, 1))
    g_lo = jnp.where(chosen[0], gate[0], jnp.where(chosen[1], gate[1], gate[2]))
    g_hi = jnp.where(chosen[3], gate[3], jnp.where(chosen[2], gate[2], gate[1]))
    pair = jnp.where(lo == 0, hi - 1, jnp.where(lo == 1, hi + 1, N_PAIRS - 1))
    return gidx * N_PAIRS + pair, g_lo, g_hi


def _slab_rows(c):
    return pl.ds(c, TM, stride=SLAB_PITCH)


def _store_slabs(ref, val, extra):
    for c in range(SLAB_ROWS):
        ref[_slab_rows(c), :] = val[:, c * LANES:(c + 1) * LANES]
    ref[_slab_rows(SLAB_ROWS), :] = extra


def _ffn_prep_tile(h, gain_ref, sc_ref, sh_ref, rw_ref, rb_ref, v_ref, part_ref):
    v = _rms(h, gain_ref[...]) * (1.0 + sc_ref[...]) + sh_ref[...]
    v_hi = v.astype(BF16)
    v_lo = (v - v_hi.astype(F32)).astype(BF16)
    a = jnp.dot(v_hi, rw_ref[...], preferred_element_type=F32)
    b = jnp.dot(v_lo, rw_ref[:, 0:LANES], preferred_element_type=F32)
    logits = (a[:, 0:LANES] + a[:, LANES:] + b).T[0:N_EXPERTS, :]
    part, g_lo, g_hi = _route(logits, rb_ref[...])
    part_ref[...] = jnp.zeros(part_ref.shape, jnp.int32)
    part_ref[0:1, :] = part
    row = lax.broadcasted_iota(jnp.int32, (LANES, TM), 0)
    gates_t = jnp.where(row == 0, g_lo, jnp.where(row == 1, g_hi, 0.0))
    _store_slabs(v_ref, v, gates_t.T)


def _dispatch_meta(part, n_slots):
    n = part.shape[0]
    onehot = (part[:, None] == jnp.arange(N_PARTS, dtype=jnp.int32)[None, :]).astype(jnp.int32)
    csum = jnp.cumsum(onehot, axis=0)
    rank = jnp.take_along_axis(csum, part[:, None], axis=1)[:, 0] - 1
    counts = csum[-1]
    padded = ((counts + TM - 1) // TM) * TM
    ends = jnp.cumsum(padded)
    starts = ends - padded
    pos = (starts[part] + rank).astype(jnp.int32)
    src = jnp.zeros((n_slots,), jnp.int32).at[pos].set(jnp.arange(n, dtype=jnp.int32))
    tile_start = jnp.arange(n_slots // TM, dtype=jnp.int32) * TM
    last_start = jnp.minimum(tile_start, ends[-1] - TM)
    tile_part = jnp.sum((ends[None, :] <= last_start[:, None]).astype(jnp.int32), axis=1)
    n_rows = jnp.clip(counts[tile_part] - (tile_start - starts[tile_part]), 0, TM).astype(jnp.int32)
    group, pair = tile_part // N_PAIRS, tile_part % N_PAIRS
    e_lo = group * GROUP_SIZE + jnp.asarray(PAIR_LO, jnp.int32)[pair]
    e_hi = group * GROUP_SIZE + jnp.asarray(PAIR_HI, jnp.int32)[pair]
    return src, e_lo, e_hi, n_rows


def _expert_kernel(src_ref, elo_ref, ehi_ref, nrows_ref, v_hbm, w1a, w3a, w2a, w1b, w3b, w2b, y_hbm,
                   xbuf, ybuf, x2d, in_sem, out_sem, *, n_tok):
    t = pl.program_id(0)
    last = pl.num_programs(0) - 1
    slot = t % 2

    def n_groups(tile):
        return (nrows_ref[tile] + ROW_GROUP - 1) // ROW_GROUP

    def in_copy(slot_, r, tok):
        return pltpu.make_async_copy(v_hbm.at[pl.ds(tok * SLAB_PITCH, SLAB_PITCH), :],
                                     xbuf.at[slot_, pl.ds(r * SLAB_PITCH, SLAB_PITCH), :], in_sem.at[slot_])

    def out_copy(slot_, r, tok):
        return pltpu.make_async_copy(ybuf.at[slot_, pl.ds(r * SLAB_PITCH, SLAB_PITCH), :],
                                     y_hbm.at[pl.ds(tok * SLAB_PITCH, SLAB_PITCH), :], out_sem.at[slot_])

    def for_rows(tile, fn):
        def body(g, carry):
            for u in range(ROW_GROUP):
                fn(g * ROW_GROUP + u, u)
            return carry
        lax.fori_loop(0, n_groups(tile), body, 0)

    def start_in(tile, slot_):
        for_rows(tile, lambda r, u: in_copy(slot_, r, src_ref[tile * TM + r]).start(priority=u % 2))

    def start_out(tile, slot_):
        n = nrows_ref[tile]

        def one(r, u):
            tok = jnp.where(r < n, src_ref[tile * TM + r], n_tok + slot_ * ROW_GROUP + u)
            out_copy(slot_, r, tok).start(priority=u % 2)
        for_rows(tile, one)

    @pl.when(t == 0)
    def _():
        xbuf[...] = jnp.zeros(xbuf.shape, xbuf.dtype)
        spare = 2 * ROW_GROUP * SLAB_PITCH
        fill = pltpu.make_async_copy(xbuf.at[1, pl.ds(0, spare), :],
                                     y_hbm.at[pl.ds(n_tok * SLAB_PITCH, spare), :], out_sem.at[0])
        fill.start()
        fill.wait()
        start_in(0, 0)

    @pl.when(jnp.logical_and(t < last, nrows_ref[jnp.minimum(t + 1, last)] > 0))
    def _():
        start_in(t + 1, 1 - slot)

    @pl.when(nrows_ref[t] > 0)
    def _():
        for_rows(t, lambda r, u: in_copy(slot, r, 0).wait())
        xs = xbuf.at[slot]
        for c in range(SLAB_ROWS):
            x2d[:, c * LANES:(c + 1) * LANES] = xs[_slab_rows(c), :].astype(BF16)
        g = xs[_slab_rows(SLAB_ROWS), :]
        x = x2d[...]

        def hidden(w1, w3, gate):
            a = jnp.dot(x, w1[...], preferred_element_type=F32)
            b = jnp.dot(x, w3[...], preferred_element_type=F32)
            return (_silu(a) * b * gate).astype(BF16)

        y = jnp.dot(hidden(w1a, w3a, g[:, 0:1]), w2a[...], preferred_element_type=F32)
        y = y + jnp.dot(hidden(w1b, w3b, g[:, 1:2]), w2b[...], preferred_element_type=F32)
        _store_slabs(ybuf.at[slot], y, jnp.zeros((TM, LANES), F32))
        start_out(t, slot)

    @pl.when(jnp.logical_and(t > 0, nrows_ref[jnp.maximum(t - 1, 0)] > 0))
    def _():
        for_rows(t - 1, lambda r, u: out_copy(1 - slot, r, 0).wait())

    @pl.when(jnp.logical_and(t == last, nrows_ref[t] > 0))
    def _():
        for_rows(t, lambda r, u: out_copy(slot, r, 0).wait())


def _experts(v_slabs, src, e_lo, e_hi, n_rows, w1, w3, w2, layer, n_tok):
    n_tiles = n_rows.shape[0]
    up = lambda sel: pl.BlockSpec((None, None, D_MODEL, D_EXPERT),
                                  lambda t, s, lo, hi, nr: (layer, (lo, hi)[sel][t], 0, 0))
    down = lambda sel: pl.BlockSpec((None, None, D_EXPERT, D_MODEL),
                                    lambda t, s, lo, hi, nr: (layer, (lo, hi)[sel][t], 0, 0))
    buf = pltpu.VMEM((2, TM * SLAB_PITCH, LANES), F32)
    return pl.pallas_call(
        functools.partial(_expert_kernel, n_tok=n_tok),
        grid_spec=pltpu.PrefetchScalarGridSpec(
            num_scalar_prefetch=4,
            grid=(n_tiles,),
            in_specs=[pl.BlockSpec(memory_space=pl.ANY), up(0), up(0), down(0), up(1), up(1), down(1)],
            out_specs=pl.BlockSpec(memory_space=pl.ANY),
            scratch_shapes=[buf, buf, pltpu.VMEM((TM, D_MODEL), BF16),
                            pltpu.SemaphoreType.DMA((2,)), pltpu.SemaphoreType.DMA((2,))],
        ),
        out_shape=jax.ShapeDtypeStruct(((n_tok + 2 * ROW_GROUP) * SLAB_PITCH, LANES), F32),
        compiler_params=_params(),
        name="experts",
    )(src, e_lo, e_hi, n_rows, v_slabs, w1, w3, w2, w1, w3, w2)


def _moe_experts(v_slabs, part, w1, w3, w2, layer):
    n = part.shape[1]
    src, e_lo, e_hi, n_rows = _dispatch_meta(part[0], n + N_PARTS * TM)
    return _experts(v_slabs, src, e_lo, e_hi, n_rows, w1, w3, w2, layer, n)


def _final_kernel(y_ref, h_ref, gate_ref, gain_ref, o_ref, hn_ref):
    for c in range(SLAB_ROWS):
        cols = slice(c * LANES, (c + 1) * LANES)
        hn_ref[:, cols] = h_ref[:, cols] + gate_ref[:, cols] * y_ref[_slab_rows(c), :]
    o_ref[...] = _rms(hn_ref[...], gain_ref[...])


def _final_norm(y_slabs, h, mod, gain, batch):
    lat_tiles = SEQ // TM
    tile = lambda b, t: b * TILES_PER_BATCH + 1 + t
    return pl.pallas_call(
        _final_kernel,
        grid=(batch, lat_tiles),
        in_specs=[
            pl.BlockSpec((TM * SLAB_PITCH, LANES), lambda b, t: (tile(b, t), 0)),
            pl.BlockSpec((TM, D_MODEL), lambda b, t: (tile(b, t), 0)),
            pl.BlockSpec((None, 1, D_MODEL), lambda b, t: (5 * MOD_ROWS + b, 0, 0)),
            pl.BlockSpec((1, D_MODEL), lambda b, t: (0, 0)),
        ],
        out_specs=pl.BlockSpec((TM, D_MODEL), lambda b, t: (b * lat_tiles + t, 0)),
        out_shape=jax.ShapeDtypeStruct((batch * SEQ, D_MODEL), F32),
        scratch_shapes=[pltpu.VMEM((TM, D_MODEL), F32)],
        compiler_params=_params(),
        name="final_norm",
    )(y_slabs, h, mod, gain.reshape(1, D_MODEL))


def _rope_tables(rot_dim, layout):
    n_freq = rot_dim // 4
    inv_freq = 1.0 / (ROPE_THETA ** (jnp.arange(n_freq, dtype=F32) / n_freq))
    t = jnp.arange(SEQ, dtype=jnp.int32)
    row = (t // GRID_W).astype(F32)
    col = (t % GRID_W).astype(F32)
    ang = jnp.concatenate([row[:, None] * inv_freq, col[:, None] * inv_freq], axis=-1)
    cos, sin = jnp.cos(ang), jnp.sin(ang)
    if layout == "padded":
        pad1, pad0 = jnp.ones_like(cos), jnp.zeros_like(sin)
        c = jnp.concatenate([cos, pad1, cos, pad1], axis=-1)
        s = jnp.concatenate([-sin, pad0, sin, pad0], axis=-1)
    else:
        c = jnp.concatenate([cos, cos], axis=-1)
        s = jnp.concatenate([-sin, sin], axis=-1)
    c = jnp.concatenate([jnp.ones((CTX_LEN, LANES), F32), c], axis=0)
    s = jnp.concatenate([jnp.zeros((CTX_LEN, LANES), F32), s], axis=0)
    return c, s


def _pad_rope_cols(w):
    half = MLA_ROPE // 2
    z = jnp.zeros(w.shape[:-1] + (half,), w.dtype)
    return jnp.concatenate([w[..., :half], z, w[..., half:], z], axis=-1)


def _first_proj(h, comb, w, **kw):
    if comb is None:
        out = _proj(None, w, first=h, **kw)
    else:
        out = _proj(None, w, comb=(comb[0], h, comb[1]), **kw)
    if kw["kinds"][-1][0] == "vt":
        return out[:2], out[2]
    return out[0], out[1]


def _out_proj(o, w_o, h, mod, ffn):
    return _proj(o, w_o.astype(BF16), out_dtype=F32, kinds=[("resid",)] * (D_MODEL // LANES),
                 res=h, mod=mod, mod_gate=2, ffn=ffn)


def _gqa_layer(h, comb, mod, norm_gain, w_qkv, q_gain, k_gain, w_o, tabs, ffn, batch):
    kinds = ([("normrope", 0, 0)] * GQA_HEADS + [("normrope", 1, 1)] * GQA_KV_HEADS
             + [("vt", j) for j in range(GQA_KV_HEADS)])
    (qk, vt), h = _first_proj(h, comb, w_qkv.astype(BF16), out_dtype=BF16, kinds=kinds, prologue="normmod",
                              gain=norm_gain, mod=mod, mod_sc=1, mod_sh=0, tabs=tabs,
                              hgain=jnp.stack([q_gain, k_gain]))
    o = _attention(qk, qk, vt, batch=batch, groups=GQA_KV_HEADS // GQA_KV_PER_STEP,
                   heads_per_step=GQA_GROUP * GQA_KV_PER_STEP, kv_per_step=GQA_KV_PER_STEP, dq=HEAD_DIM,
                   k_blk0=GQA_HEADS // GQA_KV_PER_STEP)
    return _out_proj(o, w_o, h, mod, ffn)


def _mla_layer(h, comb, mod, norm_gain, w_a, q_gain, kv_gain, w_uq, w_ukv, w_o, tabs, ffn, batch):
    n_lat = 2 * MLA_RANK // LANES
    w_a_p = jnp.concatenate([w_a[:, :2 * MLA_RANK], _pad_rope_cols(w_a[:, 2 * MLA_RANK:])], axis=1)
    a, h = _first_proj(h, comb, w_a_p.astype(BF16), out_dtype=F32, kinds=[("plain",)] * n_lat + [("rope", 1)],
                       prologue="normmod", gain=norm_gain, mod=mod, mod_sc=1, mod_sh=0, tabs=tabs,
                       chunk=LANES * (n_lat + 1))
    scale = MLA_QK ** -0.5 * LOG2E
    w_uq_h = w_uq.reshape(MLA_RANK, MLA_HEADS, MLA_QK)
    w_uq_p = jnp.concatenate([w_uq_h[..., :MLA_NOPE], _pad_rope_cols(w_uq_h[..., MLA_NOPE:])], axis=-1)
    q = _proj(a, w_uq_p.reshape(MLA_RANK, MLA_HEADS * 2 * LANES).astype(BF16), out_dtype=BF16,
              kinds=[("scale", scale), ("rope", 0)] * MLA_HEADS, prologue="rms", x_col_block=0, gain=q_gain, tabs=tabs)
    w_ukv_h = w_ukv.reshape(MLA_RANK, MLA_HEADS, 2 * HEAD_DIM)
    w_ukv_p = jnp.concatenate([w_ukv_h[..., :MLA_NOPE].reshape(MLA_RANK, -1),
                               w_ukv_h[..., MLA_NOPE:].reshape(MLA_RANK, -1)], axis=1)
    k_nope, vt = _proj(a, w_ukv_p.astype(BF16), out_dtype=BF16,
                       kinds=[("plain",)] * MLA_HEADS + [("vt", j) for j in range(MLA_HEADS)],
                       prologue="rms", x_col_block=1, gain=kv_gain)
    o = _attention(q, k_nope, vt, batch=batch, groups=MLA_HEADS // MLA_HEADS_PER_STEP,
                   heads_per_step=MLA_HEADS_PER_STEP, kv_per_step=MLA_HEADS_PER_STEP, dq=2 * LANES, k_blk0=0,
                   kpe_arr=a, kpe_blk=n_lat)
    return _out_proj(o, w_o, h, mod, ffn)


def _na_layer(h, comb, mod, norm_gain, w_qkv, rpb, w_o, ffn, batch):
    scale = HEAD_DIM ** -0.5 * LOG2E
    kinds = [("scale", scale)] * NA_HEADS + [("plain",)] * NA_HEADS + [("vt", j) for j in range(NA_HEADS)]
    (qk, vt), h = _first_proj(h, comb, w_qkv.astype(BF16), out_dtype=BF16, kinds=kinds, prologue="normmod",
                              gain=norm_gain, mod=mod, mod_sc=1, mod_sh=0)
    o = _na_attention(qk, vt, rpb, batch)
    return _out_proj(o, w_o, h, mod, ffn)


def kernel(x, c, ctx, c_ctx, mod_w, mod_b, norm_mix, norm_ffn, gqa_wqkv, gqa_q_gain, gqa_k_gain, gqa_wo,
           mla_wa, mla_q_gain, mla_kv_gain, mla_wuq, mla_wukv, mla_wo, na_wqkv, na_rpb, na_wo,
           router_w, router_bias, moe_w1, moe_w3, moe_w2, final_gain):
    batch = x.shape[0]
    assert batch <= CTX_MOD_ROW and x.shape[1:] == (SEQ, D_MODEL) and ctx.shape[1:] == (CTX_LEN, D_MODEL)
    h = (x, ctx)
    cond = jnp.zeros((MOD_ROWS, D_MODEL), F32).at[:batch].set(c).at[CTX_MOD_ROW].set(c_ctx)
    mods = _modulation(cond, mod_w, mod_b)
    mods = mods.reshape(DEPTH, MOD_ROWS, 6, D_MODEL).transpose(0, 2, 1, 3).reshape(DEPTH, 6 * MOD_ROWS, 1, D_MODEL)

    c_full, s_full = _rope_tables(HEAD_DIM, "full")
    gqa_scale = HEAD_DIM ** -0.5 * LOG2E
    gqa_tabs = (jnp.concatenate([c_full * gqa_scale, c_full], axis=1),
                jnp.concatenate([s_full * gqa_scale, s_full], axis=1))
    c_pad, s_pad = _rope_tables(MLA_ROPE, "padded")
    mla_scale = MLA_QK ** -0.5 * LOG2E
    mla_tabs = (jnp.concatenate([c_pad * mla_scale, c_pad], axis=1),
                jnp.concatenate([s_pad * mla_scale, s_pad], axis=1))
    rw_hi = router_w.astype(BF16)
    rw_lo = (router_w - rw_hi.astype(F32)).astype(BF16)
    lane_pad = ((0, 0), (0, LANES - N_EXPERTS))
    router_wt = jnp.concatenate([jnp.pad(rw_hi, lane_pad), jnp.pad(rw_lo, lane_pad)], axis=1)
    w1_b, w3_b, w2_b = moe_w1.astype(BF16), moe_w3.astype(BF16), moe_w2.astype(BF16)

    comb = None
    for i in range(DEPTH):
        mod = mods[i]
        ffn = (norm_ffn[i], router_wt, router_bias)
        kind, j = i % N_MIXERS, i // N_MIXERS
        if kind == 0:
            out = _gqa_layer(h, comb, mod, norm_mix[i], gqa_wqkv[j], gqa_q_gain[j], gqa_k_gain[j], gqa_wo[j],
                             gqa_tabs, ffn, batch)
        elif kind == 1:
            out = _mla_layer(h, comb, mod, norm_mix[i], mla_wa[j], mla_q_gain[j], mla_kv_gain[j], mla_wuq[j],
                             mla_wukv[j], mla_wo[j], mla_tabs, ffn, batch)
        else:
            out = _na_layer(h, comb, mod, norm_mix[i], na_wqkv[j], na_rpb[j], na_wo[j], ffn, batch)
        h, v_slabs, part = out
        comb = (_moe_experts(v_slabs, part, w1_b, w3_b, w2_b, i), mod)
    return _final_norm(comb[0], h, comb[1], final_gain, batch).reshape(batch, SEQ, D_MODEL)
```
